```python
import jax
import jax.numpy as jnp
from jax import lax
import numpy as np

D_MODEL = 1024
BATCH = 4
SEQ = 4096
DEPTH = 2
DEC_BATCH = 128
DEC_SEQ = 1
PAST_LEN = 2048
PAGE_SIZE = 128

N_A_LAYERS = DEPTH // 2
N_B_LAYERS = DEPTH - N_A_LAYERS
POOL_WINDOWS = (2, 4, 8, 16)
N_POOL_GROUPS = len(POOL_WINDOWS)
POOL_GROUP_DIM = D_MODEL // N_POOL_GROUPS
POOL_BUF = max(POOL_WINDOWS) - 1
N_HEADS = 16
HEAD_DIM = D_MODEL // N_HEADS
N_KV_HEADS = 4
Q_PER_KV = N_HEADS // N_KV_HEADS
N_BRANCH = 3
CMP_BLOCK = 32
CMP_STRIDE = 16
CMP_HIDDEN = 2 * HEAD_DIM
SEL_BLOCK = 64
N_SELECT = 16
WINDOW = 512
D_FF = 4 * D_MODEL
Q_BLOCK = 64
RMS_EPS = 1e-6
FORCED_PRIORITY = 1e6

kernel_name = 'yoco_pool_nsa_decoder_step'


def _rmsnorm(x, g):
    xf = x.astype(jnp.float32)
    y = xf * lax.rsqrt(jnp.mean(xf * xf, axis=-1, keepdims=True) + RMS_EPS)
    return (y * g.astype(jnp.float32)).astype(x.dtype)


def _mlp_block(x, g, w_up, w_down):
    h = jnp.square(jax.nn.relu(_rmsnorm(x, g) @ w_up))
    return x + h @ w_down


def _alibi_slopes():
    return jnp.exp2(-8.0 * (jnp.arange(N_HEADS, dtype=jnp.float32) + 1.0) / N_HEADS)


def _pool_mix(u, w, scale):
    b, l, _ = u.shape
    uf = u.astype(jnp.float32).reshape(b, l, N_POOL_GROUPS, POOL_GROUP_DIM)
    cs = jnp.concatenate([jnp.zeros_like(uf[:, :1]), jnp.cumsum(uf, axis=1)], axis=1)
    t = jnp.arange(l)
    means = []
    for gi, win in enumerate(POOL_WINDOWS):
        lo = jnp.maximum(t + 1 - win, 0)
        cnt = jnp.minimum(t + 1, win).astype(jnp.float32)
        means.append((cs[:, 1:, gi] - cs[:, lo, gi]) / cnt[None, :, None])
    diff = (jnp.stack(means, axis=2) - uf).astype(u.dtype)
    y = jnp.einsum('blgc,gcd->blgd', diff, w).reshape(b, l, D_MODEL)
    return y * scale


def _pool_layer(x, past_rows, g, w, scale):
    u = _rmsnorm(x, g)
    t = x.shape[1]
    ucat = u if past_rows is None else jnp.concatenate([past_rows.astype(u.dtype), u], axis=1)
    y = _pool_mix(ucat, w, scale)[:, -t:]
    return x + y, ucat[:, -POOL_BUF:]


def _shared_kv(x, g, w_kv):
    b, t, _ = x.shape
    return (_rmsnorm(x, g) @ w_kv).reshape(b, t, N_BRANCH, 2, N_KV_HEADS, HEAD_DIM)


def _compress(rows, pe, w1, w2):
    b, l, g, d = rows.shape
    r = CMP_BLOCK // CMP_STRIDE
    n_sub = l // CMP_STRIDE
    n_cmp = n_sub - r + 1
    sub = rows[:, : n_sub * CMP_STRIDE].reshape(b, n_sub, CMP_STRIDE, g, d)
    blocks = jnp.concatenate([sub[:, k:k + n_cmp] for k in range(r)], axis=2)
    blocks = blocks + pe[None, None, :, None, :].astype(rows.dtype)
    flat = blocks.transpose(0, 1, 3, 2, 4).reshape(b, n_cmp, g, CMP_BLOCK * d)
    return jax.nn.silu(flat @ w1) @ w2


def _sel_blocks(rows):
    b, l, g, d = rows.shape
    n_sel = -(-l // SEL_BLOCK)
    rows = jnp.pad(rows, ((0, 0), (0, n_sel * SEL_BLOCK - l), (0, 0), (0, 0)))
    return rows.reshape(b, n_sel, SEL_BLOCK, g, d).transpose(0, 3, 1, 2, 4)


def _kv_summaries(rows, cmp_pe, cmp_w1, cmp_w2):
    kc = _compress(rows[:, :, 0, 0], cmp_pe[0], cmp_w1[0], cmp_w2[0])
    vc = _compress(rows[:, :, 0, 1], cmp_pe[1], cmp_w1[1], cmp_w2[1])
    ks = _sel_blocks(rows[:, :, 1, 0])
    vs = _sel_blocks(rows[:, :, 1, 1])
    return kc, vc, ks, vs


def _masked_softmax(s, mask):
    s = jnp.where(mask, s, -jnp.inf)
    m = jnp.max(s, axis=-1, keepdims=True)
    m = jnp.where(jnp.isfinite(m), m, 0.0)
    e = jnp.where(mask, jnp.exp(s - m), 0.0)
    return e / jnp.maximum(jnp.sum(e, axis=-1, keepdims=True), 1e-30)


def _query_side(x, g, w_qg, b_gate):
    b, t, _ = x.shape
    h = _rmsnorm(x, g) @ w_qg
    q = h[..., : N_HEADS * HEAD_DIM].reshape(b, t, N_HEADS, HEAD_DIM)
    gate = jax.nn.sigmoid((h[..., N_HEADS * HEAD_DIM:] + b_gate).astype(jnp.float32))
    return q, gate.reshape(b, t, N_HEADS, N_BRANCH)


def _nsa_attend(q, gate, pos_q, kc, vc, ks_blk, vs_blk, kw, vw, pos_w, slopes):
    b, c = q.shape[:2]
    n_cmp = kc.shape[1]
    n_sel = ks_blk.shape[2]
    dt = q.dtype
    f32 = jnp.float32
    qg = (q * HEAD_DIM ** -0.5).reshape(b, c, N_KV_HEADS, Q_PER_KV, HEAD_DIM)
    sl = slopes.reshape(N_KV_HEADS, Q_PER_KV)[None, None, :, :, None]
    tq = pos_q[:, None]

    cmp_end = jnp.arange(n_cmp) * CMP_STRIDE + (CMP_BLOCK - 1)
    dist_c = (tq - cmp_end[None, :]).astype(f32)
    s_c = jnp.einsum('bcgrd,bngd->bcgrn', qg, kc).astype(f32) - sl * dist_c[None, :, None, None, :]
    p_c = _masked_softmax(s_c, (dist_c >= 0)[None, :, None, None, :])
    o_c = jnp.einsum('bcgrn,bngd->bcgrd', p_c.astype(dt), vc)

    sub = jnp.arange(n_cmp)[:, None] + jnp.arange(CMP_BLOCK // CMP_STRIDE)[None, :]
    cmp_to_sel = jax.nn.one_hot(sub * CMP_STRIDE // SEL_BLOCK, n_sel, dtype=f32).sum(axis=1)
    imp = jnp.einsum('bcgrn,ns->bcgs', p_c, cmp_to_sel)
    blk = jnp.arange(n_sel)[None, :]
    cur = (pos_q // SEL_BLOCK)[:, None]
    forced = (blk == 0) | (blk == cur) | (blk == cur - 1)
    valid = blk * SEL_BLOCK <= tq
    pri = jnp.where(valid[None, :, None, :], jnp.where(forced[None, :, None, :], FORCED_PRIORITY, imp), -1.0)
    _, idx = lax.top_k(pri, min(N_SELECT, n_sel))
    n_k = idx.shape[-1]
    bi = jnp.arange(b)[:, None, None, None]
    gi = jnp.arange(N_KV_HEADS)[None, None, :, None]
    kb = ks_blk[bi, gi, idx]
    vb = vs_blk[bi, gi, idx]
    kpos = idx[..., None] * SEL_BLOCK + jnp.arange(SEL_BLOCK)
    dist_s = (pos_q[None, :, None, None, None] - kpos).astype(f32)[:, :, :, None]
    s_s = jnp.einsum('bcgrd,bcgkjd->bcgrkj', qg, kb).astype(f32) - sl[..., None] * dist_s
    p_s = _masked_softmax(s_s.reshape(b, c, N_KV_HEADS, Q_PER_KV, n_k * SEL_BLOCK),
                          (dist_s >= 0).reshape(b, c, N_KV_HEADS, 1, n_k * SEL_BLOCK))
    o_s = jnp.einsum('bcgrm,bcgmd->bcgrd', p_s.astype(dt),
                     vb.reshape(b, c, N_KV_HEADS, n_k * SEL_BLOCK, HEAD_DIM))

    dist_w = tq - pos_w[None, :]
    mask_w = (dist_w >= 0) & (dist_w < WINDOW) & (pos_w[None, :] >= 0)
    s_w = jnp.einsum('bcgrd,blgd->bcgrl', qg, kw).astype(f32) - sl * dist_w.astype(f32)[None, :, None, None, :]
    p_w = _masked_softmax(s_w, mask_w[None, :, None, None, :])
    o_w = jnp.einsum('bcgrl,blgd->bcgrd', p_w.astype(dt), vw)

    g = gate.reshape(b, c, N_KV_HEADS, Q_PER_KV, N_BRANCH)
    o = (g[..., 0:1] * o_c.astype(f32) + g[..., 1:2] * o_s.astype(f32) + g[..., 2:3] * o_w.astype(f32))
    return o.astype(dt).reshape(b, c, N_HEADS * HEAD_DIM)


def _nsa_prompt(q, gate, kc, vc, ks_blk, vs_blk, win_rows, slopes):
    b, t = q.shape[:2]
    n_chunks = t // Q_BLOCK
    win_pad = jnp.pad(win_rows, ((0, 0), (WINDOW, 0), (0, 0), (0, 0), (0, 0)))

    def chunk(args):
        ci, qc, gc = args
        c0 = ci * Q_BLOCK
        w = lax.dynamic_slice_in_dim(win_pad, c0, WINDOW + Q_BLOCK, axis=1)
        pos_q = c0 + jnp.arange(Q_BLOCK)
        pos_w = c0 - WINDOW + jnp.arange(WINDOW + Q_BLOCK)
        return _nsa_attend(qc, gc, pos_q, kc, vc, ks_blk, vs_blk, w[:, :, 0], w[:, :, 1], pos_w, slopes)

    qc = q.reshape(b, n_chunks, Q_BLOCK, N_HEADS, HEAD_DIM).swapaxes(0, 1)
    gc = gate.reshape(b, n_chunks, Q_BLOCK, N_HEADS, N_BRANCH).swapaxes(0, 1)
    out = lax.map(chunk, (jnp.arange(n_chunks, dtype=jnp.int32), qc, gc))
    return out.swapaxes(0, 1).reshape(b, t, N_HEADS * HEAD_DIM)


def setup_inputs(seed: int = 0) -> dict:
    key = jax.random.key(seed)
    ks = jax.random.split(key, 24)
    nrm = jax.random.normal
    f32 = jnp.float32
    n_pages = PAST_LEN // PAGE_SIZE
    n_used = DEC_BATCH * n_pages
    n_phys = n_used + n_used // 4
    win_buf = min(WINDOW, PAST_LEN)
    qg_out = N_HEADS * HEAD_DIM + N_BRANCH * N_HEADS
    kv_out = N_BRANCH * 2 * N_KV_HEADS * HEAD_DIM
    page_table = jax.random.permutation(ks[5], n_phys)[:n_used].astype(jnp.int32).reshape(DEC_BATCH, n_pages)
    return {
        'x_prompt': nrm(ks[0], (BATCH, SEQ, D_MODEL), f32),
        'x_sample': nrm(ks[1], (DEC_BATCH, DEC_SEQ, D_MODEL), f32),
        'state_pool': nrm(ks[2], (DEC_BATCH, N_A_LAYERS, POOL_BUF, D_MODEL), f32),
        'cache_kv_pages': nrm(ks[3], (n_phys, PAGE_SIZE, 2, 2, N_KV_HEADS, HEAD_DIM), f32),
        'state_win': nrm(ks[4], (DEC_BATCH, win_buf, 2, N_KV_HEADS, HEAD_DIM), f32),
        'page_table': page_table,
        'norm_mix': 1.0 + 0.02 * nrm(ks[6], (DEPTH, D_MODEL), f32),
        'norm_mlp': 1.0 + 0.02 * nrm(ks[7], (DEPTH, D_MODEL), f32),
        'w_up': nrm(ks[8], (DEPTH, D_MODEL, D_FF), f32) * D_MODEL ** -0.5,
        'w_down': nrm(ks[9], (DEPTH, D_FF, D_MODEL), f32) * D_FF ** -0.5,
        'pool_w': nrm(ks[10], (N_A_LAYERS, N_POOL_GROUPS, POOL_GROUP_DIM, POOL_GROUP_DIM), f32) * POOL_GROUP_DIM ** -0.5,
        'pool_scale': 1.0 + 0.02 * nrm(ks[11], (N_A_LAYERS, D_MODEL), f32),
        'norm_kv': 1.0 + 0.02 * nrm(ks[12], (D_MODEL,), f32),
        'w_kv': nrm(ks[13], (D_MODEL, kv_out), f32) * D_MODEL ** -0.5,
        'cmp_pe': 0.1 * nrm(ks[14], (2, CMP_BLOCK, HEAD_DIM), f32),
        'cmp_w1': nrm(ks[15], (2, CMP_BLOCK * HEAD_DIM, CMP_HIDDEN), f32) * (CMP_BLOCK * HEAD_DIM) ** -0.5,
        'cmp_w2': nrm(ks[16], (2, CMP_HIDDEN, HEAD_DIM), f32) * CMP_HIDDEN ** -0.5,
        'w_qg': nrm(ks[17], (N_B_LAYERS, D_MODEL, qg_out), f32) * D_MODEL ** -0.5,
        'b_gate': 0.01 * nrm(ks[18], (N_B_LAYERS, N_BRANCH * N_HEADS), f32),
        'w_o': nrm(ks[19], (N_B_LAYERS, N_HEADS * HEAD_DIM, D_MODEL), f32) * (N_HEADS * HEAD_DIM) ** -0.5,
        'norm_final': 1.0 + 0.02 * nrm(ks[20], (D_MODEL,), f32),
    }


def reference(x_prompt, x_sample, state_pool, cache_kv_pages, state_win, page_table,
              norm_mix, norm_mlp, w_up, w_down, pool_w, pool_scale, norm_kv, w_kv,
              cmp_pe, cmp_w1, cmp_w2, w_qg, b_gate, w_o, norm_final):
    slopes = _alibi_slopes()

    xp = x_prompt
    pool_p = []
    for layer in range(DEPTH):
        if layer < N_A_LAYERS:
            xp, rows = _pool_layer(xp, None, norm_mix[layer], pool_w[layer], pool_scale[layer])
            pool_p.append(rows)
        else:
            if layer == N_A_LAYERS:
                kv_p = _shared_kv(xp, norm_kv, w_kv)
                kv_rows_p = kv_p[:, :, :2]
                kc_p, vc_p, ks_p, vs_p = _kv_summaries(kv_rows_p, cmp_pe, cmp_w1, cmp_w2)
                win_rows_p = kv_p[:, :, 2]
                win_new_p = win_rows_p[:, -min(WINDOW, xp.shape[1]):]
            j = layer - N_A_LAYERS
            q, gate = _query_side(xp, norm_mix[layer], w_qg[j], b_gate[j])
            o = _nsa_prompt(q, gate, kc_p, vc_p, ks_p, vs_p, win_rows_p, slopes)
            xp = xp + o @ w_o[j]
        xp = _mlp_block(xp, norm_mlp[layer], w_up[layer], w_down[layer])
    y_prompt = _rmsnorm(xp, norm_final)

    xs = x_sample
    dec_b, dec_t, _ = xs.shape
    past_len = page_table.shape[1] * cache_kv_pages.shape[1]
    pos_s = past_len + jnp.arange(dec_t)
    wb = state_win.shape[1]
    pool_s = []
    for layer in range(DEPTH):
        if layer < N_A_LAYERS:
            xs, rows = _pool_layer(xs, state_pool[:, layer], norm_mix[layer], pool_w[layer], pool_scale[layer])
            pool_s.append(rows)
        else:
            if layer == N_A_LAYERS:
                kv_s = _shared_kv(xs, norm_kv, w_kv)
                kv_rows_s = kv_s[:, :, :2]
                past = cache_kv_pages[page_table].reshape((dec_b, past_len) + cache_kv_pages.shape[2:]).astype(xs.dtype)
                kc_s, vc_s, ks_s, vs_s = _kv_summaries(jnp.concatenate([past, kv_rows_s], axis=1), cmp_pe, cmp_w1, cmp_w2)
                win_all = jnp.concatenate([state_win.astype(xs.dtype), kv_s[:, :, 2]], axis=1)
                win_new_s = win_all[:, -wb:]
                pos_w = past_len - wb + jnp.arange(wb + dec_t)
            j = layer - N_A_LAYERS
            q, gate = _query_side(xs, norm_mix[layer], w_qg[j], b_gate[j])
            o = _nsa_attend(q, gate, pos_s, kc_s, vc_s, ks_s, vs_s, win_all[:, :, 0], win_all[:, :, 1], pos_w, slopes)
            xs = xs + o @ w_o[j]
        xs = _mlp_block(xs, norm_mlp[layer], w_up[layer], w_down[layer])
    y_sample = _rmsnorm(xs, norm_final)

    return (y_prompt, y_sample, jnp.stack(pool_p, axis=1), jnp.stack(pool_s, axis=1),
            kv_rows_p, kv_rows_s, win_new_p, win_new_s)
```

```python
import functools

import numpy as np
import jax
import jax.numpy as jnp
from jax import lax
from jax.experimental import pallas as pl
from jax.experimental.pallas import tpu as pltpu

D_MODEL = 1024
POOL_WINDOWS = (2, 4, 8, 16)
POOL_GROUP_DIM = D_MODEL // len(POOL_WINDOWS)
POOL_BUF = max(POOL_WINDOWS) - 1
N_HEADS = 16
HEAD_DIM = 64
N_KV_HEADS = 4
Q_PER_KV = N_HEADS // N_KV_HEADS
N_BRANCH = 3
CMP_BLOCK = 32
CMP_STRIDE = 16
CMP_HIDDEN = 2 * HEAD_DIM
SEL_BLOCK = 64
N_SELECT = 16
WINDOW = 512
RMS_EPS = 1e-6
FORCED_PRIORITY = 1e6
KV_WIDTH = N_KV_HEADS * HEAD_DIM
ROW_WIDTH = 2 * 2 * KV_WIDTH

NEG = -1e30
F32 = jnp.float32
BF16 = jnp.bfloat16
MIB = 1024 * 1024


def _params(semantics, vmem_mib):
    return pltpu.CompilerParams(dimension_semantics=semantics, vmem_limit_bytes=vmem_mib * MIB)


def _rms(x, g):
    return x * lax.rsqrt(jnp.mean(x * x, axis=-1, keepdims=True) + RMS_EPS) * g


def _dot(a, b):
    return jnp.dot(a, b, preferred_element_type=F32)


def _dot_nt(a, b):
    return lax.dot_general(a, b, (((1,), (1,)), ((), ())), preferred_element_type=F32)


def _split_bf16(x):
    hi = x.astype(BF16)
    lo = (x - hi.astype(F32)).astype(BF16)
    return hi, lo


def _sigmoid(x):
    return 1.0 / (1.0 + jnp.exp(-x))


def _pool_prompt_kernel(x_ref, xp_ref, g_ref, w_ref, sc_ref, y_ref, ul_ref, u_scr, *, tt):
    i = pl.program_id(1)
    g = g_ref[...]
    x = x_ref[0]
    u = _rms(x, g)
    up = _rms(xp_ref[0], g) * (i > 0).astype(F32)
    u_scr[0:16, :] = up
    u_scr[16:16 + tt, :] = u
    t = i * tt + lax.broadcasted_iota(jnp.int32, (tt, 1), 0)
    for gi, win in enumerate(POOL_WINDOWS):
        cols = slice(gi * POOL_GROUP_DIM, (gi + 1) * POOL_GROUP_DIM)
        acc = u[:, cols]
        for k in range(1, win):
            acc = acc + u_scr[16 - k:16 - k + tt, cols]
        cnt = jnp.minimum(t + 1, win).astype(F32)
        diff = acc / cnt - u[:, cols]
        yg = _dot(diff.astype(BF16), w_ref[gi])
        y_ref[0, :, cols] = x[:, cols] + yg * sc_ref[:, cols]
    ul_ref[0] = u[tt - 16:, :]


def _pool_prompt(x, g, w_bf, scale, *, tt=256):
    b, t, d = x.shape
    n_t = t // tt
    kern = functools.partial(_pool_prompt_kernel, tt=tt)
    return pl.pallas_call(
        kern,
        grid=(b, n_t),
        in_specs=[
            pl.BlockSpec((1, tt, d), lambda bi, i: (bi, i, 0)),
            pl.BlockSpec((1, 16, d), lambda bi, i: (bi, jnp.maximum(i * (tt // 16) - 1, 0), 0)),
            pl.BlockSpec((1, d), lambda bi, i: (0, 0)),
            pl.BlockSpec((len(POOL_WINDOWS), POOL_GROUP_DIM, POOL_GROUP_DIM), lambda bi, i: (0, 0, 0)),
            pl.BlockSpec((1, d), lambda bi, i: (0, 0)),
        ],
        out_specs=[
            pl.BlockSpec((1, tt, d), lambda bi, i: (bi, i, 0)),
            pl.BlockSpec((1, 16, d), lambda bi, i: (bi, 0, 0)),
        ],
        out_shape=[jax.ShapeDtypeStruct((b, t, d), F32), jax.ShapeDtypeStruct((b, 16, d), F32)],
        scratch_shapes=[pltpu.VMEM((16 + tt, d), F32)],
        compiler_params=_params(("parallel", "arbitrary"), 32),
        name="pool_prompt",
    )(x, x, g, w_bf, scale)


def _pool_sample_kernel(x_ref, past_ref, g_ref, w_ref, sc_ref, y_ref, u_ref):
    x = x_ref[...]
    u = _rms(x, g_ref[...])
    for gi, win in enumerate(POOL_WINDOWS):
        cols = slice(gi * POOL_GROUP_DIM, (gi + 1) * POOL_GROUP_DIM)
        acc = u[:, cols]
        for k in range(1, win):
            acc = acc + past_ref[:, POOL_BUF - k, cols]
        diff = acc / float(win) - u[:, cols]
        yg = _dot(diff.astype(BF16), w_ref[gi])
        y_ref[:, cols] = x[:, cols] + yg * sc_ref[:, cols]
    u_ref[...] = u


def _pool_sample(x, past, g, w_bf, scale, *, bt=32):
    nb, d = x.shape
    return pl.pallas_call(
        _pool_sample_kernel,
        grid=(nb // bt,),
        in_specs=[
            pl.BlockSpec((bt, d), lambda i: (i, 0)),
            pl.BlockSpec((bt, POOL_BUF, d), lambda i: (i, 0, 0)),
            pl.BlockSpec((1, d), lambda i: (0, 0)),
            pl.BlockSpec((len(POOL_WINDOWS), POOL_GROUP_DIM, POOL_GROUP_DIM), lambda i: (0, 0, 0)),
            pl.BlockSpec((1, d), lambda i: (0, 0)),
        ],
        out_specs=[pl.BlockSpec((bt, d), lambda i: (i, 0)), pl.BlockSpec((bt, d), lambda i: (i, 0))],
        out_shape=[jax.ShapeDtypeStruct((nb, d), F32), jax.ShapeDtypeStruct((nb, d), F32)],
        compiler_params=_params(("parallel",), 32),
        name="pool_sample",
    )(x, past, g, w_bf, scale)


def _mlp_kernel(x_ref, g_ref, wu_ref, wd_ref, gf_ref, o_ref, xn_scr, acc_scr, *, final_norm):
    j = pl.program_id(1)

    @pl.when(j == 0)
    def _():
        xn_scr[...] = _rms(x_ref[...], g_ref[...]).astype(BF16)
        acc_scr[...] = jnp.zeros_like(acc_scr)

    h = jnp.maximum(_dot(xn_scr[...], wu_ref[...]), 0.0)
    acc_scr[...] += _dot((h * h).astype(BF16), wd_ref[...])

    @pl.when(j == pl.num_programs(1) - 1)
    def _():
        r = x_ref[...] + acc_scr[...]
        o_ref[...] = _rms(r, gf_ref[...]) if final_norm else r


def _mlp(x, g, wu_bf, wd_bf, gf, *, final_norm, tm, tf=1024):
    m, d = x.shape
    f = wu_bf.shape[1]
    kern = functools.partial(_mlp_kernel, final_norm=final_norm)
    return pl.pallas_call(
        kern,
        grid=(m // tm, f // tf),
        in_specs=[
            pl.BlockSpec((tm, d), lambda i, j: (i, 0)),
            pl.BlockSpec((1, d), lambda i, j: (0, 0)),
            pl.BlockSpec((d, tf), lambda i, j: (0, j)),
            pl.BlockSpec((tf, d), lambda i, j: (j, 0)),
            pl.BlockSpec((1, d), lambda i, j: (0, 0)),
        ],
        out_specs=pl.BlockSpec((tm, d), lambda i, j: (i, 0)),
        out_shape=jax.ShapeDtypeStruct((m, d), F32),
        scratch_shapes=[pltpu.VMEM((tm, d), BF16), pltpu.VMEM((tm, d), F32)],
        compiler_params=_params(("parallel", "arbitrary"), 48),
        name="mlp",
    )(x, g, wu_bf, wd_bf, gf)


def _lin_kernel(*refs, norm, bias, residual, splits):
    refs = list(refs)
    x_ref = refs.pop(0)
    g_ref = refs.pop(0) if norm else None
    w_ref = refs.pop(0)
    b_ref = refs.pop(0) if bias else None
    r_ref = refs.pop(0) if residual else None
    x = x_ref[...]
    if norm:
        x = _rms(x, g_ref[...])
    h = _dot(x.astype(BF16), w_ref[...])
    if bias:
        h = h + b_ref[...]
    if residual:
        h = h + r_ref[...]
    c0 = 0
    for o_ref, n in zip(refs, splits):
        o_ref[...] = h[:, c0:c0 + n]
        c0 += n


def _lin(x, w_bf, *, g=None, b=None, res=None, splits=None, tm):
    m, k = x.shape
    n = w_bf.shape[1]
    splits = splits or (n,)
    kern = functools.partial(_lin_kernel, norm=g is not None, bias=b is not None,
                             residual=res is not None, splits=splits)
    args = [x]
    in_specs = [pl.BlockSpec((tm, k), lambda i: (i, 0))]
    if g is not None:
        args.append(g)
        in_specs.append(pl.BlockSpec((1, k), lambda i: (0, 0)))
    args.append(w_bf)
    in_specs.append(pl.BlockSpec((k, n), lambda i: (0, 0)))
    if b is not None:
        args.append(b)
        in_specs.append(pl.BlockSpec((1, n), lambda i: (0, 0)))
    if res is not None:
        args.append(res)
        in_specs.append(pl.BlockSpec((tm, n), lambda i: (i, 0)))
    outs = pl.pallas_call(
        kern,
        grid=(m // tm,),
        in_specs=in_specs,
        out_specs=[pl.BlockSpec((tm, s), lambda i: (i, 0)) for s in splits],
        out_shape=[jax.ShapeDtypeStruct((m, s), F32) for s in splits],
        compiler_params=_params(("parallel",), 48),
        name="lin",
    )(*args)
    return outs


def _compress_kernel(pt_ref, *refs, n_page_refs, rows_per_ref, n_sub):
    del pt_ref
    page_refs = refs[:n_page_refs]
    w1c_ref, w1_ref, pe_ref, w2_ref, kc_ref, vc_ref, acc_scr = refs[n_page_refs:]
    acc_scr[...] = jnp.zeros_like(acc_scr)
    rows = 4 * n_sub

    def tap(j, carry):
        def lane_block(c):
            pieces = [page_refs[4 * p + c][0, pl.ds(j, rows_per_ref // CMP_STRIDE, stride=CMP_STRIDE), :]
                      for p in range(n_page_refs // 4)]
            return pieces[0] if len(pieces) == 1 else jnp.concatenate(pieces, axis=0)

        for kv in range(2):
            lo, hi = lane_block(2 * kv), lane_block(2 * kv + 1)
            lhs = jnp.concatenate([lo[:, :HEAD_DIM], lo[:, HEAD_DIM:], hi[:, :HEAD_DIM], hi[:, HEAD_DIM:]],
                                  axis=0).astype(BF16)
            acc_scr[kv] += _dot(lhs, w1c_ref[kv, j])
        return carry

    lax.fori_loop(0, CMP_STRIDE, tap, 0)
    for kv, o_ref in enumerate((kc_ref, vc_ref)):
        a = acc_scr[kv]
        pew = _dot(pe_ref[kv].astype(BF16), w1_ref[kv])[0:1]
        hid = a[:, :CMP_HIDDEN] + pltpu.roll(a[:, CMP_HIDDEN:], rows - 1, 0) + pew
        act = hid * _sigmoid(hid)
        out = _dot(act.astype(BF16), w2_ref[kv])
        o_ref[0] = out.reshape(N_KV_HEADS, n_sub, HEAD_DIM)


def _compress(pages, page_table, w1c_bf, w1_bf, pe8, w2_bf, *, paged):
    if paged:
        nb, n_pages = page_table.shape
        rows_per_ref = pages.shape[1]
        n_sub = n_pages * rows_per_ref // CMP_STRIDE
        page_specs = [
            pl.BlockSpec((1, rows_per_ref, 128), functools.partial(lambda p, c, b, pt: (pt[b, p], 0, c), p, c))
            for p in range(n_pages) for c in range(4)]
    else:
        nb, rows_per_ref, _ = pages.shape
        n_sub = rows_per_ref // CMP_STRIDE
        page_specs = [pl.BlockSpec((1, rows_per_ref, 128), functools.partial(lambda c, b, pt: (b, 0, c), c))
                      for c in range(4)]
    page_args = [pages] * len(page_specs)
    kern = functools.partial(_compress_kernel, n_page_refs=len(page_args), rows_per_ref=rows_per_ref, n_sub=n_sub)
    const = lambda *shape: pl.BlockSpec(shape, lambda b, pt: (0,) * len(shape))
    out_spec = pl.BlockSpec((1, N_KV_HEADS, n_sub, HEAD_DIM), lambda b, pt: (b, 0, 0, 0))
    return pl.pallas_call(
        kern,
        grid_spec=pltpu.PrefetchScalarGridSpec(
            num_scalar_prefetch=1,
            grid=(nb,),
            in_specs=page_specs + [
                const(2, CMP_STRIDE, HEAD_DIM, 2 * CMP_HIDDEN),
                const(2, CMP_BLOCK * HEAD_DIM, CMP_HIDDEN),
                const(2, 8, CMP_BLOCK * HEAD_DIM),
                const(2, CMP_HIDDEN, HEAD_DIM),
            ],
            out_specs=[out_spec, out_spec],
            scratch_shapes=[pltpu.VMEM((2, 4 * n_sub, 2 * CMP_HIDDEN), F32)],
        ),
        out_shape=[jax.ShapeDtypeStruct((nb, N_KV_HEADS, n_sub, HEAD_DIM), F32)] * 2,
        compiler_params=_params(("parallel",), 48),
        name="compress",
    )(page_table, *page_args, w1c_bf, w1_bf, pe8, w2_bf)


def _attn_prompt_kernel(q_ref, gt_ref, st_ref, kc_ref, vc_ref, ks_ref, vs_ref, kw_ref, vw_ref, cm_ref,
                        o_ref, qa_scr, m_scr, l_scr, acc_scr, *, tq, tk, n_sel):
    c0 = pl.program_id(2) * tq
    rows = Q_PER_KV * tq
    band = WINDOW + tq

    lane = lax.broadcasted_iota(jnp.int32, (tq, 128), 1)
    qb = q_ref[0] * (HEAD_DIM ** -0.5)
    for r in range(Q_PER_KV):
        pair = qb[:, (r // 2) * 128:(r // 2 + 1) * 128]
        if r % 2:
            pair = pltpu.roll(pair, HEAD_DIM, 1)
        qa_scr[r * tq:(r + 1) * tq, 0:128] = jnp.where(lane < HEAD_DIM, pair, st_ref[0, r]).astype(BF16)
    ql = qa_scr[:, 0:128]

    n_cmp_pad = kc_ref.shape[2]
    s = _dot_nt(ql, kc_ref[0, 0])
    trow = c0 + (lax.broadcasted_iota(jnp.int32, (rows, n_cmp_pad), 0) & (tq - 1))
    npos = CMP_STRIDE * lax.broadcasted_iota(jnp.int32, (rows, n_cmp_pad), 1) + (CMP_BLOCK - 1)
    mask = npos <= trow
    s = jnp.where(mask, s, NEG)
    m = jnp.max(s, axis=-1, keepdims=True)
    e = jnp.where(mask, jnp.exp(s - m), 0.0)
    p = e / jnp.maximum(jnp.sum(e, axis=-1, keepdims=True), 1e-30)
    o_c = _dot(p.astype(BF16), vc_ref[0, 0])

    psum = p[0:tq] + p[tq:2 * tq] + p[2 * tq:3 * tq] + p[3 * tq:4 * tq]
    hi, lo = _split_bf16(psum)
    imp = _dot_nt(cm_ref[...], hi) + _dot_nt(cm_ref[...], lo)
    blk = lax.broadcasted_iota(jnp.int32, (n_sel, tq), 0)
    tcol = c0 + lax.broadcasted_iota(jnp.int32, (n_sel, tq), 1)
    cur = tcol >> 6
    forced = (blk == 0) | (blk == cur) | (blk == cur - 1)
    valid = blk * SEL_BLOCK <= tcol
    pri = jnp.where(valid, jnp.where(forced, FORCED_PRIORITY, imp), -1.0)
    rank = jnp.zeros((n_sel, tq), jnp.int32)
    for s2 in range(n_sel):
        row = pri[s2:s2 + 1, :]
        beats = (row > pri) | ((row == pri) & (blk > s2))
        rank = rank + beats.astype(jnp.int32)
    bias = jnp.where(rank < min(N_SELECT, n_sel), 0.0, NEG).T
    right = jnp.concatenate([bias, jnp.zeros((tq, 128 - n_sel), F32)], axis=1).astype(BF16)
    for r in range(Q_PER_KV):
        qa_scr[r * tq:(r + 1) * tq, 128:256] = right

    m_scr[...] = jnp.full_like(m_scr, NEG)
    l_scr[...] = jnp.zeros_like(l_scr)
    acc_scr[...] = jnp.zeros_like(acc_scr)
    dmat = (lax.broadcasted_iota(jnp.int32, (rows, tk), 1)
            - (lax.broadcasted_iota(jnp.int32, (rows, tk), 0) & (tq - 1)))

    def key_tile(j, carry):
        k0 = pl.multiple_of(j * tk, tk)
        st = _dot_nt(qa_scr[...], ks_ref[0, 0, pl.ds(k0, tk), :])
        st = jnp.where(dmat <= c0 - k0, st, NEG)
        m_old = m_scr[...]
        m_new = jnp.maximum(m_old, jnp.max(st, axis=-1, keepdims=True))
        alpha = jnp.exp(m_old - m_new)
        pt = jnp.exp(st - m_new)
        l_scr[...] = alpha * l_scr[...] + jnp.sum(pt, axis=-1, keepdims=True)
        acc_scr[...] = alpha * acc_scr[...] + _dot(pt.astype(BF16), vs_ref[0, 0, pl.ds(k0, tk), :])
        m_scr[...] = m_new
        return carry

    lax.fori_loop(0, (c0 + tq + tk - 1) // tk, key_tile, 0)
    o_s = acc_scr[...] / l_scr[...]

    w0 = pl.multiple_of(jnp.maximum(c0 - WINDOW, 0), 128)
    sw = _dot_nt(ql, kw_ref[0, 0, pl.ds(w0, band), :])
    dist = ((c0 - w0) + (lax.broadcasted_iota(jnp.int32, (rows, band), 0) & (tq - 1))
            - lax.broadcasted_iota(jnp.int32, (rows, band), 1))
    sw = jnp.where((dist >= 0) & (dist < WINDOW), sw, NEG)
    pw = jnp.exp(sw - jnp.max(sw, axis=-1, keepdims=True))
    pw = pw / jnp.sum(pw, axis=-1, keepdims=True)
    o_w = _dot(pw.astype(BF16), vw_ref[0, 0, pl.ds(w0, band), :])

    gate = _sigmoid(gt_ref[0])
    outs = []
    for r in range(Q_PER_KV):
        rs = slice(r * tq, (r + 1) * tq)
        outs.append(gate[:, 3 * r:3 * r + 1] * o_c[rs] + gate[:, 3 * r + 1:3 * r + 2] * o_s[rs]
                    + gate[:, 3 * r + 2:3 * r + 3] * o_w[rs])
    o_ref[0] = jnp.concatenate(outs, axis=1)


def _attn_prompt(q, gate_pre, slope_tab, kc_aug, vc, ks_aug, vs, kw_aug, vw, cmat_t, *, tq=128, tk=512):
    b, t, _ = q.shape
    n_sel = t // SEL_BLOCK
    n_cmp_pad = kc_aug.shape[2]
    kern = functools.partial(_attn_prompt_kernel, tq=tq, tk=tk, n_sel=n_sel)
    per_bg = lambda *shape: pl.BlockSpec((1, 1) + shape, lambda bi, gi, ci: (bi, gi, 0, 0))
    return pl.pallas_call(
        kern,
        grid=(b, N_KV_HEADS, t // tq),
        in_specs=[
            pl.BlockSpec((1, tq, KV_WIDTH), lambda bi, gi, ci: (bi, ci, gi)),
            pl.BlockSpec((1, tq, 128), lambda bi, gi, ci: (bi, ci, gi)),
            pl.BlockSpec((1, Q_PER_KV, tq, 128), lambda bi, gi, ci: (gi, 0, 0, 0)),
            per_bg(n_cmp_pad, 128),
            per_bg(n_cmp_pad, HEAD_DIM),
            per_bg(t, 256),
            per_bg(t, HEAD_DIM),
            per_bg(t, 128),
            per_bg(t, HEAD_DIM),
            pl.BlockSpec((n_sel, n_cmp_pad), lambda bi, gi, ci: (0, 0)),
        ],
        out_specs=pl.BlockSpec((1, tq, KV_WIDTH), lambda bi, gi, ci: (bi, ci, gi)),
        out_shape=jax.ShapeDtypeStruct((b, t, N_HEADS * HEAD_DIM), F32),
        scratch_shapes=[
            pltpu.VMEM((Q_PER_KV * tq, 256), BF16),
            pltpu.VMEM((Q_PER_KV * tq, 1), F32),
            pltpu.VMEM((Q_PER_KV * tq, 1), F32),
            pltpu.VMEM((Q_PER_KV * tq, HEAD_DIM), F32),
        ],
        compiler_params=_params(("parallel", "parallel", "arbitrary"), 48),
        name="attn_prompt",
    )(q, gate_pre, slope_tab, kc_aug, vc, ks_aug, vs, kw_aug, vw, cmat_t)


def _softmax_lanes(s):
    e = jnp.exp(s - jnp.max(s, axis=-1, keepdims=True))
    return e / jnp.sum(e, axis=-1, keepdims=True)


def _attn_sample_kernel(pt_ref, *refs, n_pages, past_len, n_sel):
    del pt_ref
    q_ref, gt_ref, sl_ref, kc_ref, vc_ref = refs[:5]
    page_refs = refs[5:5 + n_pages]
    new_ref, win_ref, ex_ref, cm_ref, rm_ref, o_ref, wn_ref = refs[5 + n_pages:]
    q = q_ref[0]
    sl = sl_ref[:, 0:1]
    new = new_ref[0]

    n_cmp_pad = kc_ref.shape[1]
    dist_c = (past_len - (CMP_BLOCK - 1)
              - CMP_STRIDE * lax.broadcasted_iota(jnp.int32, (N_HEADS, n_cmp_pad), 1)).astype(F32)
    s = _dot_nt(q, kc_ref[0]) - sl * dist_c
    mask = dist_c >= 0
    s = jnp.where(mask, s, NEG)
    e = jnp.where(mask, jnp.exp(s - jnp.max(s, axis=-1, keepdims=True)), 0.0)
    p_c = e / jnp.maximum(jnp.sum(e, axis=-1, keepdims=True), 1e-30)
    o_c = _dot(p_c.astype(BF16), vc_ref[0])

    hi, lo = _split_bf16(p_c)
    imp = _dot(hi, cm_ref[...]) + _dot(lo, cm_ref[...])
    hi, lo = _split_bf16(imp)
    imp = _dot(rm_ref[...], hi) + _dot(rm_ref[...], lo)
    n_blk = imp.shape[1]
    blk = lax.broadcasted_iota(jnp.int32, (N_HEADS, n_blk), 1)
    cur = past_len // SEL_BLOCK
    forced = (blk == 0) | (blk == cur) | (blk == cur - 1)
    valid = blk * SEL_BLOCK <= past_len
    pri = jnp.where(valid, jnp.where(forced, FORCED_PRIORITY, imp), -1.0)
    pri = jnp.where(blk < n_sel, pri, -2.0)
    rank = jnp.zeros((N_HEADS, n_blk), jnp.int32)
    for s2 in range(n_sel):
        col = pri[:, s2:s2 + 1]
        beats = (col > pri) | ((col == pri) & (blk > s2))
        rank = rank + beats.astype(jnp.int32)
    bias = jnp.where(rank < min(N_SELECT, n_sel), 0.0, NEG)

    k_sel = jnp.concatenate([r[0, :, 0:KV_WIDTH] for r in page_refs], axis=0).astype(BF16)
    v_sel = jnp.concatenate([r[0, :, KV_WIDTH:2 * KV_WIDTH] for r in page_refs], axis=0).astype(BF16)
    dist_s = (past_len - lax.broadcasted_iota(jnp.int32, (N_HEADS, past_len), 1)).astype(F32)
    s = _dot_nt(q, k_sel) - sl * dist_s + _dot(bias.astype(BF16), ex_ref[...])
    qf = q.astype(F32)
    k_new = new[:, 2 * KV_WIDTH:3 * KV_WIDTH].astype(BF16).astype(F32)
    v_new = new[:, 3 * KV_WIDTH:4 * KV_WIDTH].astype(BF16).astype(F32)
    s_new = jnp.sum(qf * k_new, axis=-1, keepdims=True) + bias[:, past_len // SEL_BLOCK:past_len // SEL_BLOCK + 1]
    m = jnp.maximum(jnp.max(s, axis=-1, keepdims=True), s_new)
    e = jnp.exp(s - m)
    e_new = jnp.exp(s_new - m)
    l = jnp.sum(e, axis=-1, keepdims=True) + e_new
    o_s = (_dot(e.astype(BF16), v_sel) + e_new.astype(BF16).astype(F32) * v_new) / l

    wb = win_ref.shape[1]
    rolled = pltpu.roll(win_ref[0], wb - 1, 0)
    rowi = lax.broadcasted_iota(jnp.int32, rolled.shape, 0)
    wn = jnp.where(rowi == wb - 1, new[:, 4 * KV_WIDTH:6 * KV_WIDTH], rolled)
    wn_ref[0] = wn
    dist_w = (wb - 1 - lax.broadcasted_iota(jnp.int32, (N_HEADS, wb), 1)).astype(F32)
    p_w = _softmax_lanes(_dot_nt(q, wn[:, 0:KV_WIDTH].astype(BF16)) - sl * dist_w)
    o_w = _dot(p_w.astype(BF16), wn[:, KV_WIDTH:2 * KV_WIDTH].astype(BF16))

    gate = _sigmoid(gt_ref[0])
    o_ref[0] = gate[:, 0:1] * o_c + gate[:, 1:2] * o_s + gate[:, 2:3] * o_w


def _attn_sample(qbd, gate16, slope16, kc_all, vc_all, pages, page_table, new_rows, state_win, expand, cmat, rmat):
    nb, n_pages = page_table.shape
    page_rows = pages.shape[1]
    past_len = n_pages * page_rows
    n_sel = -(-(past_len + 1) // SEL_BLOCK)
    wb = state_win.shape[1]
    n_cmp_pad = kc_all.shape[1]
    kern = functools.partial(_attn_sample_kernel, n_pages=n_pages, past_len=past_len, n_sel=n_sel)
    per_b = lambda *shape: pl.BlockSpec((1,) + shape, lambda b, pt: (b,) + (0,) * len(shape))
    const = lambda *shape: pl.BlockSpec(shape, lambda b, pt: (0,) * len(shape))
    page_specs = [
        pl.BlockSpec((1, page_rows, 2 * KV_WIDTH), functools.partial(lambda p, b, pt: (pt[b, p], 0, 1), p))
        for p in range(n_pages)]
    return pl.pallas_call(
        kern,
        grid_spec=pltpu.PrefetchScalarGridSpec(
            num_scalar_prefetch=1,
            grid=(nb,),
            in_specs=[per_b(N_HEADS, KV_WIDTH), per_b(N_HEADS, 128), const(N_HEADS, 128),
                      per_b(n_cmp_pad, KV_WIDTH), per_b(n_cmp_pad, KV_WIDTH)]
            + page_specs
            + [per_b(1, 6 * KV_WIDTH), per_b(wb, 2 * KV_WIDTH), const(*expand.shape), const(*cmat.shape),
               const(*rmat.shape)],
            out_specs=[per_b(N_HEADS, KV_WIDTH), per_b(wb, 2 * KV_WIDTH)],
        ),
        out_shape=[jax.ShapeDtypeStruct((nb, N_HEADS, KV_WIDTH), F32),
                   jax.ShapeDtypeStruct((nb, wb, 2 * KV_WIDTH), F32)],
        compiler_params=_params(("parallel",), 48),
        name="attn_sample",
    )(page_table, qbd, gate16, slope16, kc_all, vc_all, *([pages] * n_pages), new_rows, state_win, expand, cmat,
      rmat)


def _alibi_slopes():
    return jnp.exp2(-8.0 * (jnp.arange(N_HEADS, dtype=F32) + 1.0) / N_HEADS)


def _cmp_to_sel(n_cmp_pad, n_cmp, n_sel_pad):
    m = np.zeros((n_cmp_pad, n_sel_pad), np.float32)
    for n in range(n_cmp):
        for k in range(CMP_BLOCK // CMP_STRIDE):
            m[n, (n + k) * CMP_STRIDE // SEL_BLOCK] += 1.0
    return m


def _pos_cols(pos, width):
    cols = np.zeros((pos.shape[0], width), np.float32)
    cols[:, 0] = cols[:, 1] = pos % 64
    cols[:, 2] = cols[:, 3] = pos // 64
    return cols


def kernel(x_prompt, x_sample, state_pool, cache_kv_pages, state_win, page_table, norm_mix, norm_mlp, w_up, w_down,
           pool_w, pool_scale, norm_kv, w_kv, cmp_pe, cmp_w1, cmp_w2, w_qg, b_gate, w_o, norm_final):
    b, t, d = x_prompt.shape
    nb = x_sample.shape[0]
    n_phys, page_rows = cache_kv_pages.shape[:2]
    n_q = N_HEADS * HEAD_DIM
    row = lambda v: v.reshape(1, -1)

    wu = w_up.astype(BF16)
    wd = w_down.astype(BF16)
    pw = pool_w[0].astype(BF16)
    wkv = w_kv.astype(BF16)
    wo = w_o[0].astype(BF16)
    hh = np.arange(N_HEADS)
    gcols = ((hh // Q_PER_KV) * 128 + (hh % Q_PER_KV) * N_BRANCH)[:, None] + np.arange(N_BRANCH)[None, :]
    gcols = gcols.reshape(-1)
    wg = jnp.zeros((d, N_KV_HEADS * 128), F32).at[:, gcols].set(w_qg[0][:, n_q:])
    wqg = jnp.concatenate([w_qg[0][:, :n_q], wg], axis=1).astype(BF16)
    bqg = jnp.zeros((1, n_q + N_KV_HEADS * 128), F32).at[0, n_q + gcols].set(b_gate[0])
    w1_bf = cmp_w1.astype(BF16)
    w1r = w1_bf.reshape(2, 2, CMP_STRIDE, HEAD_DIM, CMP_HIDDEN)
    w1c = jnp.concatenate([w1r[:, 0], w1r[:, 1]], axis=-1)
    pe8 = jnp.broadcast_to(cmp_pe.reshape(2, 1, CMP_BLOCK * HEAD_DIM), (2, 8, CMP_BLOCK * HEAD_DIM))
    w2_bf = cmp_w2.astype(BF16)

    slopes = _alibi_slopes()
    s_hi = slopes.astype(BF16).astype(F32)
    s_lo = (slopes - s_hi).astype(BF16).astype(F32)
    slope_cols = jnp.stack([s_hi, s_lo, 64.0 * s_hi, 64.0 * s_lo], axis=-1)

    x1, u_last = _pool_prompt(x_prompt, row(norm_mix[0]), pw, row(pool_scale[0]))
    pool_p = u_last[:, None, 16 - POOL_BUF:, :]
    x2 = _mlp(x1.reshape(b * t, d), row(norm_mlp[0]), wu[0], wd[0], row(norm_final), final_norm=False, tm=512)
    kv_rows, kv_win = _lin(x2, wkv, g=row(norm_kv), splits=(ROW_WIDTH, 2 * KV_WIDTH), tm=512)
    kv_rows_p = kv_rows.reshape(b, t, 2, 2, N_KV_HEADS, HEAD_DIM)
    win_p = kv_win.reshape(b, t, 2, N_KV_HEADS, HEAD_DIM)
    win_new_p = win_p[:, -min(WINDOW, t):]

    dummy_pt = jnp.zeros((b, 1), jnp.int32)
    kc_p, vc_p = _compress(kv_rows.reshape(b, t, ROW_WIDTH), dummy_pt, w1c, w1_bf, pe8, w2_bf, paged=False)

    q_p, gate_p = _lin(x2, wqg, g=row(norm_mix[1]), b=bqg, splits=(n_q, N_KV_HEADS * 128), tm=512)

    tq = 128
    n_sub_p = t // CMP_STRIDE
    n_sel_p = t // SEL_BLOCK
    head_major = lambda a: a.transpose(0, 2, 1, 3).astype(BF16)
    tpos = np.arange(t)
    pc_t = jnp.asarray(_pos_cols(tpos, HEAD_DIM), BF16)
    onehot_t = jnp.asarray(np.eye(n_sel_p, 128, dtype=np.float32)[tpos // SEL_BLOCK], BF16)
    bcast = lambda a: jnp.broadcast_to(a[None, None], (b, N_KV_HEADS) + a.shape)
    ks_aug = jnp.concatenate([head_major(kv_rows_p[:, :, 1, 0]), bcast(pc_t), bcast(onehot_t)], axis=-1)
    vs = head_major(kv_rows_p[:, :, 1, 1])
    kw_aug = jnp.concatenate([head_major(win_p[:, :, 0]), bcast(pc_t)], axis=-1)
    vw = head_major(win_p[:, :, 1])
    pc_c = jnp.asarray(_pos_cols(CMP_STRIDE * np.arange(n_sub_p) + CMP_BLOCK - 1, HEAD_DIM), BF16)
    kc_aug = jnp.concatenate([kc_p.astype(BF16), bcast(pc_c)], axis=-1)
    cmat_t = jnp.asarray(_cmp_to_sel(n_sub_p, n_sub_p - 1, n_sel_p).T, BF16)
    slope_tab = jnp.zeros((N_KV_HEADS, Q_PER_KV, tq, 128), F32).at[:, :, :, HEAD_DIM:HEAD_DIM + 4].set(
        jnp.broadcast_to(slope_cols.reshape(N_KV_HEADS, Q_PER_KV, 1, 4), (N_KV_HEADS, Q_PER_KV, tq, 4)))
    o_p = _attn_prompt(q_p.reshape(b, t, n_q), gate_p.reshape(b, t, N_KV_HEADS * 128), slope_tab, kc_aug,
                       vc_p.astype(BF16), ks_aug, vs, kw_aug, vw, cmat_t, tq=tq)
    x3, = _lin(o_p.reshape(b * t, n_q), wo, res=x2, tm=512)
    y_prompt = _mlp(x3, row(norm_mlp[1]), wu[1], wd[1], row(norm_final), final_norm=True, tm=512).reshape(b, t, d)

    xs0 = x_sample.reshape(nb, d)
    xs1, u_s = _pool_sample(xs0, state_pool[:, 0], row(norm_mix[0]), pw, row(pool_scale[0]))
    pool_s = jnp.concatenate([state_pool[:, 0, 1:], u_s[:, None]], axis=1)[:, None]
    xs2 = _mlp(xs1, row(norm_mlp[0]), wu[0], wd[0], row(norm_final), final_norm=False, tm=nb)
    kv_s, = _lin(xs2, wkv, g=row(norm_kv), tm=nb)
    kv_rows_s = kv_s[:, :ROW_WIDTH].reshape(nb, 1, 2, 2, N_KV_HEADS, HEAD_DIM)

    pages = cache_kv_pages.reshape(n_phys, page_rows, ROW_WIDTH)
    kc_s, vc_s = _compress(pages, page_table, w1c, w1_bf, pe8, w2_bf, paged=True)
    n_sub_s = kc_s.shape[2]
    all_heads = lambda a: a.transpose(0, 2, 1, 3).reshape(nb, n_sub_s, KV_WIDTH).astype(BF16)

    q_s, gate_s = _lin(xs2, wqg, g=row(norm_mix[1]), b=bqg, splits=(n_q, N_KV_HEADS * 128), tm=nb)
    q4 = q_s.reshape(nb, N_KV_HEADS, Q_PER_KV, HEAD_DIM) * (HEAD_DIM ** -0.5)
    qbd = jnp.einsum('bgrd,gh->bgrhd', q4, jnp.eye(N_KV_HEADS, dtype=F32)).reshape(nb, N_HEADS, KV_WIDTH)
    g16 = gate_s.reshape(nb, N_KV_HEADS, 128)[:, :, :Q_PER_KV * N_BRANCH].reshape(nb, N_HEADS, N_BRANCH)
    g16 = jnp.pad(g16, ((0, 0), (0, 0), (0, 128 - N_BRANCH)))
    slope16 = jnp.broadcast_to(slopes[:, None], (N_HEADS, 128))
    past_len = page_table.shape[1] * page_rows
    n_blk_pad = 64
    expand = jnp.asarray(np.eye(n_blk_pad, dtype=np.float32)[:, np.arange(past_len) // SEL_BLOCK], BF16)
    cmat_s = jnp.asarray(_cmp_to_sel(n_sub_s, n_sub_s - 1, n_blk_pad), BF16)
    rmat = jnp.asarray(np.kron(np.eye(N_KV_HEADS), np.ones((Q_PER_KV, Q_PER_KV))), BF16)
    o_s, win_new_s = _attn_sample(
        qbd.astype(BF16), g16, slope16, all_heads(kc_s), all_heads(vc_s), pages, page_table,
        kv_s.reshape(nb, 1, 6 * KV_WIDTH), state_win.reshape(nb, state_win.shape[1], 2 * KV_WIDTH),
        expand, cmat_s, rmat)
    o5 = o_s.reshape(nb, N_KV_HEADS, Q_PER_KV, N_KV_HEADS, HEAD_DIM)
    o_heads = jnp.stack([o5[:, gh, :, gh] for gh in range(N_KV_HEADS)], axis=1).reshape(nb, n_q)
    xs3, = _lin(o_heads, wo, res=xs2, tm=nb)
    y_sample = _mlp(xs3, row(norm_mlp[1]), wu[1], wd[1], row(norm_final), final_norm=True, tm=nb).reshape(nb, 1, d)

    return (y_prompt, y_sample, pool_p, pool_s, kv_rows_p, kv_rows_s, win_new_p,
            win_new_s.reshape(state_win.shape))
```

```python
import functools

import numpy as np
import jax
import jax.numpy as jnp
from jax import lax
from jax.experimental import pallas as pl
from jax.experimental.pallas import tpu as pltpu

D_MODEL = 1024
POOL_WINDOWS = (2, 4, 8, 16)
POOL_GROUP_DIM = D_MODEL // len(POOL_WINDOWS)
POOL_BUF = max(POOL_WINDOWS) - 1
N_HEADS = 16
HEAD_DIM = 64
N_KV_HEADS = 4
Q_PER_KV = N_HEADS // N_KV_HEADS
N_BRANCH = 3
CMP_BLOCK = 32
CMP_STRIDE = 16
CMP_HIDDEN = 2 * HEAD_DIM
SEL_BLOCK = 64
N_SELECT = 16
WINDOW = 512
RMS_EPS = 1e-6
FORCED_PRIORITY = 1e6
KV_WIDTH = N_KV_HEADS * HEAD_DIM
ROW_WIDTH = 2 * 2 * KV_WIDTH

NEG = -1e30
F32 = jnp.float32
BF16 = jnp.bfloat16
MIB = 1024 * 1024


def _params(semantics, vmem_mib):
    return pltpu.CompilerParams(dimension_semantics=semantics, vmem_limit_bytes=vmem_mib * MIB)


def _rms(x, g):
    return x * lax.rsqrt(jnp.mean(x * x, axis=-1, keepdims=True) + RMS_EPS) * g


def _dot(a, b):
    return jnp.dot(a, b, preferred_element_type=F32)


def _dot_nt(a, b):
    return lax.dot_general(a, b, (((1,), (1,)), ((), ())), preferred_element_type=F32)


def _split_bf16(x):
    hi = x.astype(BF16)
    lo = (x - hi.astype(F32)).astype(BF16)
    return hi, lo


def _sigmoid(x):
    return 1.0 / (1.0 + jnp.exp(-x))


def _pool_prompt_kernel(x_ref, xp_ref, g_ref, w_ref, sc_ref, y_ref, ul_ref, u_scr, *, tt):
    i = pl.program_id(1)
    g = g_ref[...]
    x = x_ref[0]
    u = _rms(x, g)
    up = _rms(xp_ref[0], g) * (i > 0).astype(F32)
    u_scr[0:16, :] = up
    u_scr[16:16 + tt, :] = u
    t = i * tt + lax.broadcasted_iota(jnp.int32, (tt, 1), 0)
    for gi, win in enumerate(POOL_WINDOWS):
        cols = slice(gi * POOL_GROUP_DIM, (gi + 1) * POOL_GROUP_DIM)
        acc = u[:, cols]
        for k in range(1, win):
            acc = acc + u_scr[16 - k:16 - k + tt, cols]
        cnt = jnp.minimum(t + 1, win).astype(F32)
        diff = acc / cnt - u[:, cols]
        yg = _dot(diff.astype(BF16), w_ref[gi])
        y_ref[0, :, cols] = x[:, cols] + yg * sc_ref[:, cols]
    ul_ref[0] = u[tt - 16:, :]


def _pool_prompt(x, g, w_bf, scale, *, tt=256):
    b, t, d = x.shape
    n_t = t // tt
    kern = functools.partial(_pool_prompt_kernel, tt=tt)
    return pl.pallas_call(
        kern,
        grid=(b, n_t),
        in_specs=[
            pl.BlockSpec((1, tt, d), lambda bi, i: (bi, i, 0)),
            pl.BlockSpec((1, 16, d), lambda bi, i: (bi, jnp.maximum(i * (tt // 16) - 1, 0), 0)),
            pl.BlockSpec((1, d), lambda bi, i: (0, 0)),
            pl.BlockSpec((len(POOL_WINDOWS), POOL_GROUP_DIM, POOL_GROUP_DIM), lambda bi, i: (0, 0, 0)),
            pl.BlockSpec((1, d), lambda bi, i: (0, 0)),
        ],
        out_specs=[
            pl.BlockSpec((1, tt, d), lambda bi, i: (bi, i, 0)),
            pl.BlockSpec((1, 16, d), lambda bi, i: (bi, 0, 0)),
        ],
        out_shape=[jax.ShapeDtypeStruct((b, t, d), F32), jax.ShapeDtypeStruct((b, 16, d), F32)],
        scratch_shapes=[pltpu.VMEM((16 + tt, d), F32)],
        compiler_params=_params(("parallel", "arbitrary"), 32),
        name="pool_prompt",
    )(x, x, g, w_bf, scale)


def _pool_sample_kernel(x_ref, past_ref, g_ref, w_ref, sc_ref, y_ref, u_ref):
    x = x_ref[...]
    u = _rms(x, g_ref[...])
    for gi, win in enumerate(POOL_WINDOWS):
        cols = slice(gi * POOL_GROUP_DIM, (gi + 1) * POOL_GROUP_DIM)
        acc = u[:, cols]
        for k in range(1, win):
            acc = acc + past_ref[:, POOL_BUF - k, cols]
        diff = acc / float(win) - u[:, cols]
        yg = _dot(diff.astype(BF16), w_ref[gi])
        y_ref[:, cols] = x[:, cols] + yg * sc_ref[:, cols]
    u_ref[...] = u


def _pool_sample(x, past, g, w_bf, scale, *, bt=32):
    nb, d = x.shape
    return pl.pallas_call(
        _pool_sample_kernel,
        grid=(nb // bt,),
        in_specs=[
            pl.BlockSpec((bt, d), lambda i: (i, 0)),
            pl.BlockSpec((bt, POOL_BUF, d), lambda i: (i, 0, 0)),
            pl.BlockSpec((1, d), lambda i: (0, 0)),
            pl.BlockSpec((len(POOL_WINDOWS), POOL_GROUP_DIM, POOL_GROUP_DIM), lambda i: (0, 0, 0)),
            pl.BlockSpec((1, d), lambda i: (0, 0)),
        ],
        out_specs=[pl.BlockSpec((bt, d), lambda i: (i, 0)), pl.BlockSpec((bt, d), lambda i: (i, 0))],
        out_shape=[jax.ShapeDtypeStruct((nb, d), F32), jax.ShapeDtypeStruct((nb, d), F32)],
        compiler_params=_params(("parallel",), 32),
        name="pool_sample",
    )(x, past, g, w_bf, scale)


def _mlp_kernel(x_ref, g_ref, wu_ref, wd_ref, gf_ref, o_ref, xn_scr, acc_scr, *, final_norm):
    j = pl.program_id(1)

    @pl.when(j == 0)
    def _():
        xn_scr[...] = _rms(x_ref[...], g_ref[...]).astype(BF16)
        acc_scr[...] = jnp.zeros_like(acc_scr)

    h = jnp.maximum(_dot(xn_scr[...], wu_ref[...]), 0.0)
    acc_scr[...] += _dot((h * h).astype(BF16), wd_ref[...])

    @pl.when(j == pl.num_programs(1) - 1)
    def _():
        r = x_ref[...] + acc_scr[...]
        o_ref[...] = _rms(r, gf_ref[...]) if final_norm else r


def _mlp(x, g, wu_bf, wd_bf, gf, *, final_norm, tm, tf=1024):
    m, d = x.shape
    f = wu_bf.shape[1]
    kern = functools.partial(_mlp_kernel, final_norm=final_norm)
    return pl.pallas_call(
        kern,
        grid=(m // tm, f // tf),
        in_specs=[
            pl.BlockSpec((tm, d), lambda i, j: (i, 0)),
            pl.BlockSpec((1, d), lambda i, j: (0, 0)),
            pl.BlockSpec((d, tf), lambda i, j: (0, j)),
            pl.BlockSpec((tf, d), lambda i, j: (j, 0)),
            pl.BlockSpec((1, d), lambda i, j: (0, 0)),
        ],
        out_specs=pl.BlockSpec((tm, d), lambda i, j: (i, 0)),
        out_shape=jax.ShapeDtypeStruct((m, d), F32),
        scratch_shapes=[pltpu.VMEM((tm, d), BF16), pltpu.VMEM((tm, d), F32)],
        compiler_params=_params(("parallel", "arbitrary"), 48),
        name="mlp",
    )(x, g, wu_bf, wd_bf, gf)


def _lin_kernel(*refs, norm, bias, residual, splits):
    refs = list(refs)
    x_ref = refs.pop(0)
    g_ref = refs.pop(0) if norm else None
    w_ref = refs.pop(0)
    b_ref = refs.pop(0) if bias else None
    r_ref = refs.pop(0) if residual else None
    x = x_ref[...]
    if norm:
        x = _rms(x, g_ref[...])
    h = _dot(x.astype(BF16), w_ref[...])
    if bias:
        h = h + b_ref[...]
    if residual:
        h = h + r_ref[...]
    c0 = 0
    for o_ref, n in zip(refs, splits):
        o_ref[...] = h[:, c0:c0 + n]
        c0 += n


def _lin(x, w_bf, *, g=None, b=None, res=None, splits=None, tm):
    m, k = x.shape
    n = w_bf.shape[1]
    splits = splits or (n,)
    kern = functools.partial(_lin_kernel, norm=g is not None, bias=b is not None,
                             residual=res is not None, splits=splits)
    args = [x]
    in_specs = [pl.BlockSpec((tm, k), lambda i: (i, 0))]
    if g is not None:
        args.append(g)
        in_specs.append(pl.BlockSpec((1, k), lambda i: (0, 0)))
    args.append(w_bf)
    in_specs.append(pl.BlockSpec((k, n), lambda i: (0, 0)))
    if b is not None:
        args.append(b)
        in_specs.append(pl.BlockSpec((1, n), lambda i: (0, 0)))
    if res is not None:
        args.append(res)
        in_specs.append(pl.BlockSpec((tm, n), lambda i: (i, 0)))
    outs = pl.pallas_call(
        kern,
        grid=(m // tm,),
        in_specs=in_specs,
        out_specs=[pl.BlockSpec((tm, s), lambda i: (i, 0)) for s in splits],
        out_shape=[jax.ShapeDtypeStruct((m, s), F32) for s in splits],
        compiler_params=_params(("parallel",), 48),
        name="lin",
    )(*args)
    return outs


def _compress_kernel(pt_ref, *refs, n_page_refs, paged, rows_per_ref, n_sub):
    del pt_ref
    page_refs = refs[:n_page_refs]
    w1d_ref, w1_ref, pe_ref, w2_ref, kc_ref, vc_ref = refs[n_page_refs:]
    n_take = rows_per_ref // CMP_STRIDE

    def tap_rows(c, j):
        take = pl.ds(j, n_take, stride=CMP_STRIDE)
        if paged:
            return jnp.concatenate([r[0, c, take, :] for r in page_refs], axis=0)
        return page_refs[c][0, take, :]

    for kv, o_ref in enumerate((kc_ref, vc_ref)):
        pew = _dot(pe_ref[kv].astype(BF16), w1_ref[kv])[0:1]
        for c2 in range(2):
            acc = None
            for j in range(CMP_STRIDE):
                d = _dot(tap_rows(2 * kv + c2, j).astype(BF16), w1d_ref[kv, j])
                acc = d if acc is None else acc + d
            for e in range(2):
                a = acc[:, 2 * e * CMP_HIDDEN:(2 * e + 1) * CMP_HIDDEN]
                b2 = acc[:, (2 * e + 1) * CMP_HIDDEN:(2 * e + 2) * CMP_HIDDEN]
                hid = a + pltpu.roll(b2, n_sub - 1, 0) + pew
                act = hid * _sigmoid(hid)
                o_ref[0, 2 * c2 + e] = _dot(act.astype(BF16), w2_ref[kv])


def _compress(pages, page_table, w1d_bf, w1_bf, pe8, w2_bf, *, paged):
    if paged:
        nb, n_pages = page_table.shape
        rows_per_ref = pages.shape[2]
        n_sub = n_pages * rows_per_ref // CMP_STRIDE
        page_specs = [
            pl.BlockSpec((1, 4, rows_per_ref, 128), functools.partial(lambda p, b, pt: (pt[b, p], 0, 0, 0), p))
            for p in range(n_pages)]
    else:
        nb, rows_per_ref, _ = pages.shape
        n_sub = rows_per_ref // CMP_STRIDE
        page_specs = [pl.BlockSpec((1, rows_per_ref, 128), functools.partial(lambda c, b, pt: (b, 0, c), c))
                      for c in range(4)]
    page_args = [pages] * len(page_specs)
    kern = functools.partial(_compress_kernel, n_page_refs=len(page_args), paged=paged, rows_per_ref=rows_per_ref,
                             n_sub=n_sub)
    const = lambda *shape: pl.BlockSpec(shape, lambda b, pt: (0,) * len(shape))
    out_spec = pl.BlockSpec((1, N_KV_HEADS, n_sub, HEAD_DIM), lambda b, pt: (b, 0, 0, 0))
    return pl.pallas_call(
        kern,
        grid_spec=pltpu.PrefetchScalarGridSpec(
            num_scalar_prefetch=1,
            grid=(nb,),
            in_specs=page_specs + [
                const(2, CMP_STRIDE, 2 * HEAD_DIM, 4 * CMP_HIDDEN),
                const(2, CMP_BLOCK * HEAD_DIM, CMP_HIDDEN),
                const(2, 8, CMP_BLOCK * HEAD_DIM),
                const(2, CMP_HIDDEN, HEAD_DIM),
            ],
            out_specs=[out_spec, out_spec],
        ),
        out_shape=[jax.ShapeDtypeStruct((nb, N_KV_HEADS, n_sub, HEAD_DIM), F32)] * 2,
        compiler_params=_params(("parallel",), 48),
        name="compress",
    )(page_table, *page_args, w1d_bf, w1_bf, pe8, w2_bf)


def _attn_prompt_kernel(q_ref, gt_ref, st_ref, kc_ref, vc_ref, ks_ref, vs_ref, kw_ref, vw_ref, cm_ref, dm_ref,
                        o_ref, qa_scr, m_scr, ala_scr, alb_scr, sa_scr, sb_scr, pa_scr, pb_scr, acc_scr,
                        *, tq, tk, rc, n_sel):
    c0 = pl.program_id(2) * tq
    rows = Q_PER_KV * tq
    band = WINDOW + tq

    lane = lax.broadcasted_iota(jnp.int32, (tq, 128), 1)
    qb = q_ref[0] * (HEAD_DIM ** -0.5)
    for r in range(Q_PER_KV):
        pair = qb[:, (r // 2) * 128:(r // 2 + 1) * 128]
        if r % 2:
            pair = pltpu.roll(pair, HEAD_DIM, 1)
        qa_scr[r * tq:(r + 1) * tq, 0:128] = jnp.where(lane < HEAD_DIM, pair, st_ref[0, r]).astype(BF16)
    ql = qa_scr[:, 0:128]

    n_cmp_pad = kc_ref.shape[2]
    s = _dot_nt(ql, kc_ref[0, 0])
    trow = c0 + (lax.broadcasted_iota(jnp.int32, (rows, n_cmp_pad), 0) & (tq - 1))
    npos = CMP_STRIDE * lax.broadcasted_iota(jnp.int32, (rows, n_cmp_pad), 1) + (CMP_BLOCK - 1)
    mask = npos <= trow
    s = jnp.where(mask, s, NEG)
    m = jnp.max(s, axis=-1, keepdims=True)
    e = jnp.where(mask, jnp.exp(s - m), 0.0)
    p = e / jnp.maximum(jnp.sum(e, axis=-1, keepdims=True), 1e-30)
    o_c = _dot(p.astype(BF16), vc_ref[0, 0])

    psum = p[0:tq] + p[tq:2 * tq] + p[2 * tq:3 * tq] + p[3 * tq:4 * tq]
    hi, lo = _split_bf16(psum)
    imp = _dot_nt(cm_ref[...], hi) + _dot_nt(cm_ref[...], lo)
    blk = lax.broadcasted_iota(jnp.int32, (n_sel, tq), 0)
    tcol = c0 + lax.broadcasted_iota(jnp.int32, (n_sel, tq), 1)
    cur = tcol >> 6
    forced = (blk == 0) | (blk == cur) | (blk == cur - 1)
    valid = blk * SEL_BLOCK <= tcol
    pri = jnp.where(valid, jnp.where(forced, FORCED_PRIORITY, imp), -1.0)
    rank = jnp.zeros((n_sel, tq), jnp.int32)
    for s2 in range(n_sel):
        row = pri[s2:s2 + 1, :]
        beats = (row > pri) | ((row == pri) & (blk > s2))
        rank = rank + beats.astype(jnp.int32)
    bias = jnp.where(rank < min(N_SELECT, n_sel), 0.0, NEG).T
    right = jnp.concatenate([bias, jnp.zeros((tq, 128 - n_sel), F32)], axis=1).astype(BF16)
    for r in range(Q_PER_KV):
        qa_scr[r * tq:(r + 1) * tq, 128:256] = right

    m_scr[...] = jnp.full_like(m_scr, NEG)
    acc_scr[...] = jnp.zeros_like(acc_scr)

    def scores(j):
        return _dot_nt(qa_scr[...], ks_ref[0, 0, pl.ds(pl.multiple_of(j * tk, tk), tk), :])

    def soft_pv(s_scr, p_scr, al_scr, j):
        k0 = pl.multiple_of(j * tk, tk)
        for i in range(rows // rc):
            rs = slice(i * rc, (i + 1) * rc)
            qs = (i * rc) % tq
            sc = jnp.where(dm_ref[qs:qs + rc, :] <= c0 - k0, s_scr[rs, :], NEG)
            m_old = m_scr[rs]
            m_new = jnp.maximum(m_old, jnp.max(sc, axis=-1, keepdims=True))
            al_scr[rs] = jnp.exp(m_old - m_new)
            p_scr[rs] = jnp.exp(sc - m_new).astype(BF16)
            m_scr[rs] = m_new
        acc_scr[...] = al_scr[...] * acc_scr[...] + _dot(p_scr[...], vs_ref[0, 0, pl.ds(k0, tk), :])

    n_tiles = c0 // tk + 1
    last_tile = ks_ref.shape[2] // tk - 1
    sa_scr[...] = scores(0)

    def tile_pair(jj, carry):
        sb_scr[...] = scores(2 * jj + 1)
        soft_pv(sa_scr, pa_scr, ala_scr, 2 * jj)
        sa_scr[...] = scores(jnp.minimum(2 * jj + 2, last_tile))
        soft_pv(sb_scr, pb_scr, alb_scr, 2 * jj + 1)
        return carry

    lax.fori_loop(0, (n_tiles + 1) // 2, tile_pair, 0)
    acc = acc_scr[...]
    o_s = acc[:, :HEAD_DIM] / acc[:, HEAD_DIM:HEAD_DIM + 1]

    w0 = pl.multiple_of(jnp.maximum(c0 - WINDOW, 0), 128)
    sw = _dot_nt(ql, kw_ref[0, 0, pl.ds(w0, band), :])
    dist = ((c0 - w0) + (lax.broadcasted_iota(jnp.int32, (rows, band), 0) & (tq - 1))
            - lax.broadcasted_iota(jnp.int32, (rows, band), 1))
    sw = jnp.where((dist >= 0) & (dist < WINDOW), sw, NEG)
    pw = jnp.exp(sw - jnp.max(sw, axis=-1, keepdims=True))
    pw = pw / jnp.sum(pw, axis=-1, keepdims=True)
    o_w = _dot(pw.astype(BF16), vw_ref[0, 0, pl.ds(w0, band), :])

    gate = _sigmoid(gt_ref[0])
    outs = []
    for r in range(Q_PER_KV):
        rs = slice(r * tq, (r + 1) * tq)
        outs.append(gate[:, 3 * r:3 * r + 1] * o_c[rs] + gate[:, 3 * r + 1:3 * r + 2] * o_s[rs]
                    + gate[:, 3 * r + 2:3 * r + 3] * o_w[rs])
    o_ref[0] = jnp.concatenate(outs, axis=1)


def _attn_prompt(q, gate_pre, slope_tab, kc_aug, vc, ks_aug, vs, kw_aug, vw, cmat_t, *, tq=128, tk=512, rc=64):
    b, t, _ = q.shape
    n_sel = t // SEL_BLOCK
    n_cmp_pad = kc_aug.shape[2]
    kern = functools.partial(_attn_prompt_kernel, tq=tq, tk=tk, rc=rc, n_sel=n_sel)
    rows = Q_PER_KV * tq
    dmat = jnp.asarray(np.arange(tk)[None, :] - np.arange(tq)[:, None], jnp.int32)
    per_bg = lambda *shape: pl.BlockSpec((1, 1) + shape, lambda bi, gi, ci: (bi, gi, 0, 0))
    return pl.pallas_call(
        kern,
        grid=(b, N_KV_HEADS, t // tq),
        in_specs=[
            pl.BlockSpec((1, tq, KV_WIDTH), lambda bi, gi, ci: (bi, ci, gi)),
            pl.BlockSpec((1, tq, 128), lambda bi, gi, ci: (bi, ci, gi)),
            pl.BlockSpec((1, Q_PER_KV, tq, 128), lambda bi, gi, ci: (gi, 0, 0, 0)),
            per_bg(n_cmp_pad, 128),
            per_bg(n_cmp_pad, HEAD_DIM),
            per_bg(t, 256),
            per_bg(t, 128),
            per_bg(t, 128),
            per_bg(t, HEAD_DIM),
            pl.BlockSpec((n_sel, n_cmp_pad), lambda bi, gi, ci: (0, 0)),
            pl.BlockSpec((tq, tk), lambda bi, gi, ci: (0, 0)),
        ],
        out_specs=pl.BlockSpec((1, tq, KV_WIDTH), lambda bi, gi, ci: (bi, ci, gi)),
        out_shape=jax.ShapeDtypeStruct((b, t, N_HEADS * HEAD_DIM), F32),
        scratch_shapes=[
            pltpu.VMEM((rows, 256), BF16),
            pltpu.VMEM((rows, 1), F32),
            pltpu.VMEM((rows, 1), F32),
            pltpu.VMEM((rows, 1), F32),
            pltpu.VMEM((rows, tk), F32),
            pltpu.VMEM((rows, tk), F32),
            pltpu.VMEM((rows, tk), BF16),
            pltpu.VMEM((rows, tk), BF16),
            pltpu.VMEM((rows, 128), F32),
        ],
        compiler_params=_params(("parallel", "parallel", "arbitrary"), 48),
        name="attn_prompt",
    )(q, gate_pre, slope_tab, kc_aug, vc, ks_aug, vs, kw_aug, vw, cmat_t, dmat)


def _softmax_lanes(s):
    e = jnp.exp(s - jnp.max(s, axis=-1, keepdims=True))
    return e / jnp.sum(e, axis=-1, keepdims=True)


def _attn_sample_kernel(pt_ref, *refs, n_pages, past_len, n_sel):
    del pt_ref
    q_ref, gt_ref, sl_ref, kc_ref, vc_ref = refs[:5]
    page_refs = refs[5:5 + n_pages]
    new_ref, win_ref, ex_ref, cm_ref, rm_ref, o_ref, wn_ref = refs[5 + n_pages:]
    q = q_ref[0]
    sl = sl_ref[:, 0:1]
    new = new_ref[0]

    n_cmp_pad = kc_ref.shape[1]
    dist_c = (past_len - (CMP_BLOCK - 1)
              - CMP_STRIDE * lax.broadcasted_iota(jnp.int32, (N_HEADS, n_cmp_pad), 1)).astype(F32)
    s = _dot_nt(q, kc_ref[0]) - sl * dist_c
    mask = dist_c >= 0
    s = jnp.where(mask, s, NEG)
    e = jnp.where(mask, jnp.exp(s - jnp.max(s, axis=-1, keepdims=True)), 0.0)
    p_c = e / jnp.maximum(jnp.sum(e, axis=-1, keepdims=True), 1e-30)
    o_c = _dot(p_c.astype(BF16), vc_ref[0])

    hi, lo = _split_bf16(p_c)
    imp = _dot(hi, cm_ref[...]) + _dot(lo, cm_ref[...])
    hi, lo = _split_bf16(imp)
    imp = _dot(rm_ref[...], hi) + _dot(rm_ref[...], lo)
    n_blk = imp.shape[1]
    blk = lax.broadcasted_iota(jnp.int32, (N_HEADS, n_blk), 1)
    cur = past_len // SEL_BLOCK
    forced = (blk == 0) | (blk == cur) | (blk == cur - 1)
    valid = blk * SEL_BLOCK <= past_len
    pri = jnp.where(valid, jnp.where(forced, FORCED_PRIORITY, imp), -1.0)
    pri = jnp.where(blk < n_sel, pri, -2.0)
    rank = jnp.zeros((N_HEADS, n_blk), jnp.int32)
    for s2 in range(n_sel):
        col = pri[:, s2:s2 + 1]
        beats = (col > pri) | ((col == pri) & (blk > s2))
        rank = rank + beats.astype(jnp.int32)
    bias = jnp.where(rank < min(N_SELECT, n_sel), 0.0, NEG)

    k_sel = jnp.concatenate([jnp.concatenate([r[0, 0], r[0, 1]], axis=1) for r in page_refs],
                            axis=0).astype(BF16)
    v_sel = jnp.concatenate([jnp.concatenate([r[0, 2], r[0, 3]], axis=1) for r in page_refs],
                            axis=0).astype(BF16)
    dist_s = (past_len - lax.broadcasted_iota(jnp.int32, (N_HEADS, past_len), 1)).astype(F32)
    s = _dot_nt(q, k_sel) - sl * dist_s + _dot(bias.astype(BF16), ex_ref[...])
    qf = q.astype(F32)
    k_new = new[:, 2 * KV_WIDTH:3 * KV_WIDTH].astype(BF16).astype(F32)
    v_new = new[:, 3 * KV_WIDTH:4 * KV_WIDTH].astype(BF16).astype(F32)
    s_new = jnp.sum(qf * k_new, axis=-1, keepdims=True) + bias[:, past_len // SEL_BLOCK:past_len // SEL_BLOCK + 1]
    m = jnp.maximum(jnp.max(s, axis=-1, keepdims=True), s_new)
    e = jnp.exp(s - m)
    e_new = jnp.exp(s_new - m)
    l = jnp.sum(e, axis=-1, keepdims=True) + e_new
    o_s = (_dot(e.astype(BF16), v_sel) + e_new.astype(BF16).astype(F32) * v_new) / l

    wb = win_ref.shape[1]
    rolled = pltpu.roll(win_ref[0], wb - 1, 0)
    rowi = lax.broadcasted_iota(jnp.int32, rolled.shape, 0)
    wn = jnp.where(rowi == wb - 1, new[:, 4 * KV_WIDTH:6 * KV_WIDTH], rolled)
    wn_ref[0] = wn
    dist_w = (wb - 1 - lax.broadcasted_iota(jnp.int32, (N_HEADS, wb), 1)).astype(F32)
    p_w = _softmax_lanes(_dot_nt(q, wn[:, 0:KV_WIDTH].astype(BF16)) - sl * dist_w)
    o_w = _dot(p_w.astype(BF16), wn[:, KV_WIDTH:2 * KV_WIDTH].astype(BF16))

    gate = _sigmoid(gt_ref[0])
    o_ref[0] = gate[:, 0:1] * o_c + gate[:, 1:2] * o_s + gate[:, 2:3] * o_w


def _attn_sample(qbd, gate16, slope16, kc_all, vc_all, pages, page_table, new_rows, state_win, expand, cmat, rmat):
    nb, n_pages = page_table.shape
    page_rows = pages.shape[2]
    past_len = n_pages * page_rows
    n_sel = -(-(past_len + 1) // SEL_BLOCK)
    wb = state_win.shape[1]
    n_cmp_pad = kc_all.shape[1]
    kern = functools.partial(_attn_sample_kernel, n_pages=n_pages, past_len=past_len, n_sel=n_sel)
    per_b = lambda *shape: pl.BlockSpec((1,) + shape, lambda b, pt: (b,) + (0,) * len(shape))
    const = lambda *shape: pl.BlockSpec(shape, lambda b, pt: (0,) * len(shape))
    page_specs = [
        pl.BlockSpec((1, 4, page_rows, 128), functools.partial(lambda p, b, pt: (pt[b, p], 1, 0, 0), p))
        for p in range(n_pages)]
    return pl.pallas_call(
        kern,
        grid_spec=pltpu.PrefetchScalarGridSpec(
            num_scalar_prefetch=1,
            grid=(nb,),
            in_specs=[per_b(N_HEADS, KV_WIDTH), per_b(N_HEADS, 128), const(N_HEADS, 128),
                      per_b(n_cmp_pad, KV_WIDTH), per_b(n_cmp_pad, KV_WIDTH)]
            + page_specs
            + [per_b(1, 6 * KV_WIDTH), per_b(wb, 2 * KV_WIDTH), const(*expand.shape), const(*cmat.shape),
               const(*rmat.shape)],
            out_specs=[per_b(N_HEADS, KV_WIDTH), per_b(wb, 2 * KV_WIDTH)],
        ),
        out_shape=[jax.ShapeDtypeStruct((nb, N_HEADS, KV_WIDTH), F32),
                   jax.ShapeDtypeStruct((nb, wb, 2 * KV_WIDTH), F32)],
        compiler_params=_params(("parallel",), 48),
        name="attn_sample",
    )(page_table, qbd, gate16, slope16, kc_all, vc_all, *([pages] * n_pages), new_rows, state_win, expand, cmat,
      rmat)


def _alibi_slopes():
    return jnp.exp2(-8.0 * (jnp.arange(N_HEADS, dtype=F32) + 1.0) / N_HEADS)


def _cmp_to_sel(n_cmp_pad, n_cmp, n_sel_pad):
    m = np.zeros((n_cmp_pad, n_sel_pad), np.float32)
    for n in range(n_cmp):
        for k in range(CMP_BLOCK // CMP_STRIDE):
            m[n, (n + k) * CMP_STRIDE // SEL_BLOCK] += 1.0
    return m


def _pos_cols(pos, width):
    cols = np.zeros((pos.shape[0], width), np.float32)
    cols[:, 0] = cols[:, 1] = pos % 64
    cols[:, 2] = cols[:, 3] = pos // 64
    return cols


def kernel(x_prompt, x_sample, state_pool, cache_kv_pages, state_win, page_table, norm_mix, norm_mlp, w_up, w_down,
           pool_w, pool_scale, norm_kv, w_kv, cmp_pe, cmp_w1, cmp_w2, w_qg, b_gate, w_o, norm_final):
    b, t, d = x_prompt.shape
    nb = x_sample.shape[0]
    n_phys, page_rows = cache_kv_pages.shape[:2]
    n_q = N_HEADS * HEAD_DIM
    row = lambda v: v.reshape(1, -1)

    wu = w_up.astype(BF16)
    wd = w_down.astype(BF16)
    pw = pool_w[0].astype(BF16)
    wkv = w_kv.astype(BF16)
    wo = w_o[0].astype(BF16)
    hh = np.arange(N_HEADS)
    gcols = ((hh // Q_PER_KV) * 128 + (hh % Q_PER_KV) * N_BRANCH)[:, None] + np.arange(N_BRANCH)[None, :]
    gcols = gcols.reshape(-1)
    wg = jnp.zeros((d, N_KV_HEADS * 128), F32).at[:, gcols].set(w_qg[0][:, n_q:])
    wqg = jnp.concatenate([w_qg[0][:, :n_q], wg], axis=1).astype(BF16)
    bqg = jnp.zeros((1, n_q + N_KV_HEADS * 128), F32).at[0, n_q + gcols].set(b_gate[0])
    w1_bf = cmp_w1.astype(BF16)
    w1r = w1_bf.reshape(2, 2, CMP_STRIDE, HEAD_DIM, CMP_HIDDEN)
    w1c = jnp.concatenate([w1r[:, 0], w1r[:, 1]], axis=-1)
    w1d = jnp.concatenate([jnp.concatenate([w1c, jnp.zeros_like(w1c)], axis=-1),
                           jnp.concatenate([jnp.zeros_like(w1c), w1c], axis=-1)], axis=2)
    pe8 = jnp.broadcast_to(cmp_pe.reshape(2, 1, CMP_BLOCK * HEAD_DIM), (2, 8, CMP_BLOCK * HEAD_DIM))
    w2_bf = cmp_w2.astype(BF16)

    slopes = _alibi_slopes()
    s_hi = slopes.astype(BF16).astype(F32)
    s_lo = (slopes - s_hi).astype(BF16).astype(F32)
    slope_cols = jnp.stack([s_hi, s_lo, 64.0 * s_hi, 64.0 * s_lo], axis=-1)

    x1, u_last = _pool_prompt(x_prompt, row(norm_mix[0]), pw, row(pool_scale[0]))
    pool_p = u_last[:, None, 16 - POOL_BUF:, :]
    x2 = _mlp(x1.reshape(b * t, d), row(norm_mlp[0]), wu[0], wd[0], row(norm_final), final_norm=False, tm=512)
    kv_rows, kv_win = _lin(x2, wkv, g=row(norm_kv), splits=(ROW_WIDTH, 2 * KV_WIDTH), tm=512)
    kv_rows_p = kv_rows.reshape(b, t, 2, 2, N_KV_HEADS, HEAD_DIM)
    win_p = kv_win.reshape(b, t, 2, N_KV_HEADS, HEAD_DIM)
    win_new_p = win_p[:, -min(WINDOW, t):]

    dummy_pt = jnp.zeros((b, 1), jnp.int32)
    kc_p, vc_p = _compress(kv_rows.reshape(b, t, ROW_WIDTH), dummy_pt, w1d, w1_bf, pe8, w2_bf, paged=False)

    q_p, gate_p = _lin(x2, wqg, g=row(norm_mix[1]), b=bqg, splits=(n_q, N_KV_HEADS * 128), tm=512)

    tq = 128
    n_sub_p = t // CMP_STRIDE
    n_sel_p = t // SEL_BLOCK
    head_major = lambda a: a.transpose(0, 2, 1, 3).astype(BF16)
    tpos = np.arange(t)
    pc_t = jnp.asarray(_pos_cols(tpos, HEAD_DIM), BF16)
    onehot_t = jnp.asarray(np.eye(n_sel_p, 128, dtype=np.float32)[tpos // SEL_BLOCK], BF16)
    bcast = lambda a: jnp.broadcast_to(a[None, None], (b, N_KV_HEADS) + a.shape)
    ks_aug = jnp.concatenate([head_major(kv_rows_p[:, :, 1, 0]), bcast(pc_t), bcast(onehot_t)], axis=-1)
    ones_col = jnp.asarray(np.eye(1, HEAD_DIM, dtype=np.float32).repeat(t, axis=0), BF16)
    vs = jnp.concatenate([head_major(kv_rows_p[:, :, 1, 1]), bcast(ones_col)], axis=-1)
    kw_aug = jnp.concatenate([head_major(win_p[:, :, 0]), bcast(pc_t)], axis=-1)
    vw = head_major(win_p[:, :, 1])
    pc_c = jnp.asarray(_pos_cols(CMP_STRIDE * np.arange(n_sub_p) + CMP_BLOCK - 1, HEAD_DIM), BF16)
    kc_aug = jnp.concatenate([kc_p.astype(BF16), bcast(pc_c)], axis=-1)
    cmat_t = jnp.asarray(_cmp_to_sel(n_sub_p, n_sub_p - 1, n_sel_p).T, BF16)
    slope_tab = jnp.zeros((N_KV_HEADS, Q_PER_KV, tq, 128), F32).at[:, :, :, HEAD_DIM:HEAD_DIM + 4].set(
        jnp.broadcast_to(slope_cols.reshape(N_KV_HEADS, Q_PER_KV, 1, 4), (N_KV_HEADS, Q_PER_KV, tq, 4)))
    o_p = _attn_prompt(q_p.reshape(b, t, n_q), gate_p.reshape(b, t, N_KV_HEADS * 128), slope_tab, kc_aug,
                       vc_p.astype(BF16), ks_aug, vs, kw_aug, vw, cmat_t, tq=tq)
    x3, = _lin(o_p.reshape(b * t, n_q), wo, res=x2, tm=512)
    y_prompt = _mlp(x3, row(norm_mlp[1]), wu[1], wd[1], row(norm_final), final_norm=True, tm=512).reshape(b, t, d)

    xs0 = x_sample.reshape(nb, d)
    xs1, u_s = _pool_sample(xs0, state_pool[:, 0], row(norm_mix[0]), pw, row(pool_scale[0]))
    pool_s = jnp.concatenate([state_pool[:, 0, 1:], u_s[:, None]], axis=1)[:, None]
    xs2 = _mlp(xs1, row(norm_mlp[0]), wu[0], wd[0], row(norm_final), final_norm=False, tm=nb)
    kv_s, = _lin(xs2, wkv, g=row(norm_kv), tm=nb)
    kv_rows_s = kv_s[:, :ROW_WIDTH].reshape(nb, 1, 2, 2, N_KV_HEADS, HEAD_DIM)

    pages = cache_kv_pages.reshape(n_phys, page_rows, ROW_WIDTH // 128, 128).transpose(0, 2, 1, 3)
    kc_s, vc_s = _compress(pages, page_table, w1d, w1_bf, pe8, w2_bf, paged=True)
    n_sub_s = kc_s.shape[2]
    all_heads = lambda a: a.transpose(0, 2, 1, 3).reshape(nb, n_sub_s, KV_WIDTH).astype(BF16)

    q_s, gate_s = _lin(xs2, wqg, g=row(norm_mix[1]), b=bqg, splits=(n_q, N_KV_HEADS * 128), tm=nb)
    q4 = q_s.reshape(nb, N_KV_HEADS, Q_PER_KV, HEAD_DIM) * (HEAD_DIM ** -0.5)
    qbd = jnp.einsum('bgrd,gh->bgrhd', q4, jnp.eye(N_KV_HEADS, dtype=F32)).reshape(nb, N_HEADS, KV_WIDTH)
    g16 = gate_s.reshape(nb, N_KV_HEADS, 128)[:, :, :Q_PER_KV * N_BRANCH].reshape(nb, N_HEADS, N_BRANCH)
    g16 = jnp.pad(g16, ((0, 0), (0, 0), (0, 128 - N_BRANCH)))
    slope16 = jnp.broadcast_to(slopes[:, None], (N_HEADS, 128))
    past_len = page_table.shape[1] * page_rows
    n_blk_pad = 64
    expand = jnp.asarray(np.eye(n_blk_pad, dtype=np.float32)[:, np.arange(past_len) // SEL_BLOCK], BF16)
    cmat_s = jnp.asarray(_cmp_to_sel(n_sub_s, n_sub_s - 1, n_blk_pad), BF16)
    rmat = jnp.asarray(np.kron(np.eye(N_KV_HEADS), np.ones((Q_PER_KV, Q_PER_KV))), BF16)
    o_s, win_new_s = _attn_sample(
        qbd.astype(BF16), g16, slope16, all_heads(kc_s), all_heads(vc_s), pages, page_table,
        kv_s.reshape(nb, 1, 6 * KV_WIDTH), state_win.reshape(nb, state_win.shape[1], 2 * KV_WIDTH),
        expand, cmat_s, rmat)
    o5 = o_s.reshape(nb, N_KV_HEADS, Q_PER_KV, N_KV_HEADS, HEAD_DIM)
    o_heads = jnp.stack([o5[:, gh, :, gh] for gh in range(N_KV_HEADS)], axis=1).reshape(nb, n_q)
    xs3, = _lin(o_heads, wo, res=xs2, tm=nb)
    y_sample = _mlp(xs3, row(norm_mlp[1]), wu[1], wd[1], row(norm_final), final_norm=True, tm=nb).reshape(nb, 1, d)

    return (y_prompt, y_sample, pool_p, pool_s, kv_rows_p, kv_rows_s, win_new_p,
            win_new_s.reshape(state_win.shape))
```

```python
import functools

import numpy as np
import jax
import jax.numpy as jnp
from jax import lax
from jax.experimental import pallas as pl
from jax.experimental.pallas import tpu as pltpu

D_MODEL = 1024
POOL_WINDOWS = (2, 4, 8, 16)
POOL_GROUP_DIM = D_MODEL // len(POOL_WINDOWS)
POOL_BUF = max(POOL_WINDOWS) - 1
N_HEADS = 16
HEAD_DIM = 64
N_KV_HEADS = 4
Q_PER_KV = N_HEADS // N_KV_HEADS
N_BRANCH = 3
CMP_BLOCK = 32
CMP_STRIDE = 16
CMP_HIDDEN = 2 * HEAD_DIM
SEL_BLOCK = 64
N_SELECT = 16
WINDOW = 512
RMS_EPS = 1e-6
FORCED_PRIORITY = 1e6
KV_WIDTH = N_KV_HEADS * HEAD_DIM
ROW_WIDTH = 2 * 2 * KV_WIDTH

NEG = -1e30
F32 = jnp.float32
BF16 = jnp.bfloat16
MIB = 1024 * 1024


def _params(semantics, vmem_mib):
    return pltpu.CompilerParams(dimension_semantics=semantics, vmem_limit_bytes=vmem_mib * MIB)


def _rms(x, g):
    return x * lax.rsqrt(jnp.mean(x * x, axis=-1, keepdims=True) + RMS_EPS) * g


def _dot(a, b):
    return jnp.dot(a, b, preferred_element_type=F32)


def _dot_nt(a, b):
    return lax.dot_general(a, b, (((1,), (1,)), ((), ())), preferred_element_type=F32)


def _split_bf16(x):
    hi = x.astype(BF16)
    lo = (x - hi.astype(F32)).astype(BF16)
    return hi, lo


def _sigmoid(x):
    return 1.0 / (1.0 + jnp.exp(-x))


def _pool_prompt_kernel(x_ref, xp_ref, g_ref, w_ref, sc_ref, y_ref, ul_ref, u_scr, *, tt):
    i = pl.program_id(1)
    g = g_ref[...]
    x = x_ref[0]
    u = _rms(x, g)
    up = _rms(xp_ref[0], g) * (i > 0).astype(F32)
    u_scr[0:16, :] = up
    u_scr[16:16 + tt, :] = u
    t = i * tt + lax.broadcasted_iota(jnp.int32, (tt, 1), 0)
    for gi, win in enumerate(POOL_WINDOWS):
        cols = slice(gi * POOL_GROUP_DIM, (gi + 1) * POOL_GROUP_DIM)
        acc = u[:, cols]
        for k in range(1, win):
            acc = acc + u_scr[16 - k:16 - k + tt, cols]
        cnt = jnp.minimum(t + 1, win).astype(F32)
        diff = acc / cnt - u[:, cols]
        yg = _dot(diff.astype(BF16), w_ref[gi])
        y_ref[0, :, cols] = x[:, cols] + yg * sc_ref[:, cols]
    ul_ref[0] = u[tt - 16:, :]


def _pool_prompt(x, g, w_bf, scale, *, tt=256):
    b, t, d = x.shape
    n_t = t // tt
    kern = functools.partial(_pool_prompt_kernel, tt=tt)
    return pl.pallas_call(
        kern,
        grid=(b, n_t),
        in_specs=[
            pl.BlockSpec((1, tt, d), lambda bi, i: (bi, i, 0)),
            pl.BlockSpec((1, 16, d), lambda bi, i: (bi, jnp.maximum(i * (tt // 16) - 1, 0), 0)),
            pl.BlockSpec((1, d), lambda bi, i: (0, 0)),
            pl.BlockSpec((len(POOL_WINDOWS), POOL_GROUP_DIM, POOL_GROUP_DIM), lambda bi, i: (0, 0, 0)),
            pl.BlockSpec((1, d), lambda bi, i: (0, 0)),
        ],
        out_specs=[
            pl.BlockSpec((1, tt, d), lambda bi, i: (bi, i, 0)),
            pl.BlockSpec((1, 16, d), lambda bi, i: (bi, 0, 0)),
        ],
        out_shape=[jax.ShapeDtypeStruct((b, t, d), F32), jax.ShapeDtypeStruct((b, 16, d), F32)],
        scratch_shapes=[pltpu.VMEM((16 + tt, d), F32)],
        compiler_params=_params(("parallel", "arbitrary"), 32),
        name="pool_prompt",
    )(x, x, g, w_bf, scale)


def _pool_sample_kernel(x_ref, past_ref, g_ref, w_ref, sc_ref, y_ref, u_ref):
    x = x_ref[...]
    u = _rms(x, g_ref[...])
    for gi, win in enumerate(POOL_WINDOWS):
        cols = slice(gi * POOL_GROUP_DIM, (gi + 1) * POOL_GROUP_DIM)
        acc = u[:, cols]
        for k in range(1, win):
            acc = acc + past_ref[:, POOL_BUF - k, cols]
        diff = acc / float(win) - u[:, cols]
        yg = _dot(diff.astype(BF16), w_ref[gi])
        y_ref[:, cols] = x[:, cols] + yg * sc_ref[:, cols]
    u_ref[...] = u


def _pool_sample(x, past, g, w_bf, scale, *, bt=32):
    nb, d = x.shape
    return pl.pallas_call(
        _pool_sample_kernel,
        grid=(nb // bt,),
        in_specs=[
            pl.BlockSpec((bt, d), lambda i: (i, 0)),
            pl.BlockSpec((bt, POOL_BUF, d), lambda i: (i, 0, 0)),
            pl.BlockSpec((1, d), lambda i: (0, 0)),
            pl.BlockSpec((len(POOL_WINDOWS), POOL_GROUP_DIM, POOL_GROUP_DIM), lambda i: (0, 0, 0)),
            pl.BlockSpec((1, d), lambda i: (0, 0)),
        ],
        out_specs=[pl.BlockSpec((bt, d), lambda i: (i, 0)), pl.BlockSpec((bt, d), lambda i: (i, 0))],
        out_shape=[jax.ShapeDtypeStruct((nb, d), F32), jax.ShapeDtypeStruct((nb, d), F32)],
        compiler_params=_params(("parallel",), 32),
        name="pool_sample",
    )(x, past, g, w_bf, scale)


def _mlp_kernel(x_ref, g_ref, wu_ref, wd_ref, gf_ref, o_ref, xn_scr, acc_scr, *, final_norm):
    j = pl.program_id(1)

    @pl.when(j == 0)
    def _():
        xn_scr[...] = _rms(x_ref[...], g_ref[...]).astype(BF16)
        acc_scr[...] = jnp.zeros_like(acc_scr)

    h = jnp.maximum(_dot(xn_scr[...], wu_ref[...]), 0.0)
    acc_scr[...] += _dot((h * h).astype(BF16), wd_ref[...])

    @pl.when(j == pl.num_programs(1) - 1)
    def _():
        r = x_ref[...] + acc_scr[...]
        o_ref[...] = _rms(r, gf_ref[...]) if final_norm else r


def _mlp(x, g, wu_bf, wd_bf, gf, *, final_norm, tm, tf=1024):
    m, d = x.shape
    f = wu_bf.shape[1]
    kern = functools.partial(_mlp_kernel, final_norm=final_norm)
    return pl.pallas_call(
        kern,
        grid=(m // tm, f // tf),
        in_specs=[
            pl.BlockSpec((tm, d), lambda i, j: (i, 0)),
            pl.BlockSpec((1, d), lambda i, j: (0, 0)),
            pl.BlockSpec((d, tf), lambda i, j: (0, j)),
            pl.BlockSpec((tf, d), lambda i, j: (j, 0)),
            pl.BlockSpec((1, d), lambda i, j: (0, 0)),
        ],
        out_specs=pl.BlockSpec((tm, d), lambda i, j: (i, 0)),
        out_shape=jax.ShapeDtypeStruct((m, d), F32),
        scratch_shapes=[pltpu.VMEM((tm, d), BF16), pltpu.VMEM((tm, d), F32)],
        compiler_params=_params(("parallel", "arbitrary"), 48),
        name="mlp",
    )(x, g, wu_bf, wd_bf, gf)


def _lin_kernel(*refs, norm, bias, residual, splits):
    refs = list(refs)
    x_ref = refs.pop(0)
    g_ref = refs.pop(0) if norm else None
    w_ref = refs.pop(0)
    b_ref = refs.pop(0) if bias else None
    r_ref = refs.pop(0) if residual else None
    x = x_ref[...]
    if norm:
        x = _rms(x, g_ref[...])
    h = _dot(x.astype(BF16), w_ref[...])
    if bias:
        h = h + b_ref[...]
    if residual:
        h = h + r_ref[...]
    c0 = 0
    for o_ref, n in zip(refs, splits):
        o_ref[...] = h[:, c0:c0 + n]
        c0 += n


def _lin(x, w_bf, *, g=None, b=None, res=None, splits=None, tm):
    m, k = x.shape
    n = w_bf.shape[1]
    splits = splits or (n,)
    kern = functools.partial(_lin_kernel, norm=g is not None, bias=b is not None,
                             residual=res is not None, splits=splits)
    args = [x]
    in_specs = [pl.BlockSpec((tm, k), lambda i: (i, 0))]
    if g is not None:
        args.append(g)
        in_specs.append(pl.BlockSpec((1, k), lambda i: (0, 0)))
    args.append(w_bf)
    in_specs.append(pl.BlockSpec((k, n), lambda i: (0, 0)))
    if b is not None:
        args.append(b)
        in_specs.append(pl.BlockSpec((1, n), lambda i: (0, 0)))
    if res is not None:
        args.append(res)
        in_specs.append(pl.BlockSpec((tm, n), lambda i: (i, 0)))
    outs = pl.pallas_call(
        kern,
        grid=(m // tm,),
        in_specs=in_specs,
        out_specs=[pl.BlockSpec((tm, s), lambda i: (i, 0)) for s in splits],
        out_shape=[jax.ShapeDtypeStruct((m, s), F32) for s in splits],
        compiler_params=_params(("parallel",), 48),
        name="lin",
    )(*args)
    return outs


def _compress_kernel(pt_ref, *refs, n_page_refs, paged, rows_per_ref, n_sub):
    del pt_ref
    page_refs = refs[:n_page_refs]
    w1d_ref, w1_ref, pe_ref, w2_ref, kc_ref, vc_ref = refs[n_page_refs:]
    n_take = rows_per_ref // CMP_STRIDE

    def tap_rows(c, j):
        take = pl.ds(j, n_take, stride=CMP_STRIDE)
        if paged:
            return jnp.concatenate([r[0, c, take, :] for r in page_refs], axis=0)
        return page_refs[c][0, take, :]

    for kv, o_ref in enumerate((kc_ref, vc_ref)):
        pew = _dot(pe_ref[kv].astype(BF16), w1_ref[kv])[0:1]
        for c2 in range(2):
            acc = None
            for jp in range(CMP_STRIDE // 2):
                lhs = jnp.concatenate([tap_rows(2 * kv + c2, 2 * jp), tap_rows(2 * kv + c2, 2 * jp + 1)], axis=1)
                d = _dot(lhs.astype(BF16), w1d_ref[kv, jp])
                acc = d if acc is None else acc + d
            for e in range(2):
                a = acc[:, 2 * e * CMP_HIDDEN:(2 * e + 1) * CMP_HIDDEN]
                b2 = acc[:, (2 * e + 1) * CMP_HIDDEN:(2 * e + 2) * CMP_HIDDEN]
                hid = a + pltpu.roll(b2, n_sub - 1, 0) + pew
                act = hid * _sigmoid(hid)
                o_ref[0, 2 * c2 + e] = _dot(act.astype(BF16), w2_ref[kv])


def _compress(pages, page_table, w1d_bf, w1_bf, pe8, w2_bf, *, paged):
    if paged:
        nb, n_pages = page_table.shape
        rows_per_ref = pages.shape[2]
        n_sub = n_pages * rows_per_ref // CMP_STRIDE
        page_specs = [
            pl.BlockSpec((1, 4, rows_per_ref, 128), functools.partial(lambda p, b, pt: (pt[b, p], 0, 0, 0), p))
            for p in range(n_pages)]
    else:
        nb, rows_per_ref, _ = pages.shape
        n_sub = rows_per_ref // CMP_STRIDE
        page_specs = [pl.BlockSpec((1, rows_per_ref, 128), functools.partial(lambda c, b, pt: (b, 0, c), c))
                      for c in range(4)]
    page_args = [pages] * len(page_specs)
    kern = functools.partial(_compress_kernel, n_page_refs=len(page_args), paged=paged, rows_per_ref=rows_per_ref,
                             n_sub=n_sub)
    const = lambda *shape: pl.BlockSpec(shape, lambda b, pt: (0,) * len(shape))
    out_spec = pl.BlockSpec((1, N_KV_HEADS, n_sub, HEAD_DIM), lambda b, pt: (b, 0, 0, 0))
    return pl.pallas_call(
        kern,
        grid_spec=pltpu.PrefetchScalarGridSpec(
            num_scalar_prefetch=1,
            grid=(nb,),
            in_specs=page_specs + [
                const(2, CMP_STRIDE // 2, 4 * HEAD_DIM, 4 * CMP_HIDDEN),
                const(2, CMP_BLOCK * HEAD_DIM, CMP_HIDDEN),
                const(2, 8, CMP_BLOCK * HEAD_DIM),
                const(2, CMP_HIDDEN, HEAD_DIM),
            ],
            out_specs=[out_spec, out_spec],
        ),
        out_shape=[jax.ShapeDtypeStruct((nb, N_KV_HEADS, n_sub, HEAD_DIM), F32)] * 2,
        compiler_params=_params(("parallel",), 48),
        name="compress",
    )(page_table, *page_args, w1d_bf, w1_bf, pe8, w2_bf)


def _attn_prompt_kernel(q_ref, gt_ref, st_ref, kc_ref, vc_ref, ks_ref, vs_ref, kw_ref, vw_ref, cm_ref, dm_ref,
                        dc_ref, wb_ref, o_ref, qa_scr, pri_scr, m_scr, ala_scr, alb_scr, sa_scr, sb_scr, pa_scr,
                        pb_scr, acc_scr, *, tq, tk, rc, n_sel):
    c0 = pl.program_id(2) * tq
    rows = Q_PER_KV * tq
    band = WINDOW + tq

    lane = lax.broadcasted_iota(jnp.int32, (tq, 128), 1)
    qb = q_ref[0] * (HEAD_DIM ** -0.5)
    for r in range(Q_PER_KV):
        pair = qb[:, (r // 2) * 128:(r // 2 + 1) * 128]
        if r % 2:
            pair = pltpu.roll(pair, HEAD_DIM, 1)
        qa_scr[r * tq:(r + 1) * tq, 0:128] = jnp.where(lane < HEAD_DIM, pair, st_ref[0, r]).astype(BF16)
    ql = qa_scr[:, 0:128]

    n_cmp_pad = kc_ref.shape[2]
    s = _dot_nt(ql, kc_ref[0, 0])
    mask = jnp.concatenate([dc_ref[...]] * Q_PER_KV, axis=0) <= c0
    s = jnp.where(mask, s, NEG)
    m = jnp.max(s, axis=-1, keepdims=True)
    e = jnp.where(mask, jnp.exp(s - m), 0.0)
    p = e / jnp.maximum(jnp.sum(e, axis=-1, keepdims=True), 1e-30)
    o_c = _dot(p.astype(BF16), vc_ref[0, 0])

    psum = p[0:tq] + p[tq:2 * tq] + p[2 * tq:3 * tq] + p[3 * tq:4 * tq]
    hi, lo = _split_bf16(psum)
    imp = _dot_nt(cm_ref[...], hi) + _dot_nt(cm_ref[...], lo)
    blk = lax.broadcasted_iota(jnp.int32, (n_sel, tq), 0)
    tcol = c0 + lax.broadcasted_iota(jnp.int32, (n_sel, tq), 1)
    cur = tcol >> 6
    forced = (blk == 0) | (blk == cur) | (blk == cur - 1)
    valid = blk * SEL_BLOCK <= tcol
    pri = jnp.where(valid, jnp.where(forced, FORCED_PRIORITY, imp), -1.0)
    pri_scr[...] = pri
    n_valid = c0 // SEL_BLOCK + tq // SEL_BLOCK
    n_walk = jnp.where(n_valid > N_SELECT, (n_valid + 7) // 8, 0)

    def walk(g8, rank):
        grp = pri_scr[pl.ds(pl.multiple_of(g8 * 8, 8), 8), :]
        for i in range(8):
            row = grp[i:i + 1, :]
            beats = (row > pri) | ((row == pri) & (blk > g8 * 8 + i))
            rank = rank + beats.astype(jnp.int32)
        return rank

    rank = lax.fori_loop(0, n_walk, walk, jnp.zeros((n_sel, tq), jnp.int32))
    bias = jnp.where(rank < min(N_SELECT, n_sel), 0.0, NEG).T
    right = jnp.concatenate([bias, jnp.zeros((tq, 128 - n_sel), F32)], axis=1).astype(BF16)
    for r in range(Q_PER_KV):
        qa_scr[r * tq:(r + 1) * tq, 128:256] = right

    m_scr[...] = jnp.full_like(m_scr, NEG)
    acc_scr[...] = jnp.zeros_like(acc_scr)

    def scores(j):
        return _dot_nt(qa_scr[...], ks_ref[0, 0, pl.ds(pl.multiple_of(j * tk, tk), tk), :])

    def soft_pv(s_scr, p_scr, al_scr, j, masked):
        k0 = pl.multiple_of(j * tk, tk)
        for i in range(rows // rc):
            rs = slice(i * rc, (i + 1) * rc)
            qs = (i * rc) % tq
            sc = s_scr[rs, :]
            if masked:
                sc = jnp.where(dm_ref[qs:qs + rc, :] <= c0 - k0, sc, NEG)
            m_old = m_scr[rs]
            m_new = jnp.maximum(m_old, jnp.max(sc, axis=-1, keepdims=True))
            al_scr[rs] = jnp.exp(m_old - m_new)
            p_scr[rs] = jnp.exp(sc - m_new).astype(BF16)
            m_scr[rs] = m_new
        acc_scr[...] = al_scr[...] * acc_scr[...] + _dot(p_scr[...], vs_ref[0, 0, pl.ds(k0, tk), :])

    n_tiles = c0 // tk + 1
    n_loop = (n_tiles - 1) // 2
    sa_scr[...] = scores(0)

    def tile_pair(jj, carry):
        sb_scr[...] = scores(2 * jj + 1)
        soft_pv(sa_scr, pa_scr, ala_scr, 2 * jj, False)
        sa_scr[...] = scores(2 * jj + 2)
        soft_pv(sb_scr, pb_scr, alb_scr, 2 * jj + 1, False)
        return carry

    lax.fori_loop(0, n_loop, tile_pair, 0)
    sb_scr[...] = scores(2 * n_loop + 1)
    soft_pv(sa_scr, pa_scr, ala_scr, 2 * n_loop, True)
    soft_pv(sb_scr, pb_scr, alb_scr, 2 * n_loop + 1, True)
    acc = acc_scr[...]
    o_s = acc[:, :HEAD_DIM] / acc[:, HEAD_DIM:HEAD_DIM + 1]

    w0 = pl.multiple_of(jnp.maximum(c0 - WINDOW, 0), 128)
    sw = _dot_nt(ql, kw_ref[0, 0, pl.ds(w0, band), :]) + jnp.concatenate([wb_ref[0]] * Q_PER_KV, axis=0)
    pw = jnp.exp(sw - jnp.max(sw, axis=-1, keepdims=True))
    aw = _dot(pw.astype(BF16), vw_ref[0, 0, pl.ds(w0, band), :])
    o_w = aw[:, :HEAD_DIM] / aw[:, HEAD_DIM:HEAD_DIM + 1]

    gate = _sigmoid(gt_ref[0])
    outs = []
    for r in range(Q_PER_KV):
        rs = slice(r * tq, (r + 1) * tq)
        outs.append(gate[:, 3 * r:3 * r + 1] * o_c[rs] + gate[:, 3 * r + 1:3 * r + 2] * o_s[rs]
                    + gate[:, 3 * r + 2:3 * r + 3] * o_w[rs])
    o_ref[0] = jnp.concatenate(outs, axis=1)


def _attn_prompt(q, gate_pre, slope_tab, kc_aug, vc, ks_aug, vs, kw_aug, vw, cmat_t, *, tq=128, tk=512, rc=64):
    b, t, _ = q.shape
    n_sel = t // SEL_BLOCK
    n_cmp_pad = kc_aug.shape[2]
    kern = functools.partial(_attn_prompt_kernel, tq=tq, tk=tk, rc=rc, n_sel=n_sel)
    rows = Q_PER_KV * tq
    band = WINDOW + tq
    qi = np.arange(tq)[:, None]
    dmat = jnp.asarray(np.arange(tk)[None, :] - qi, jnp.int32)
    dcmp = jnp.asarray(CMP_STRIDE * np.arange(n_cmp_pad)[None, :] + (CMP_BLOCK - 1) - qi, jnp.int32)
    dist = (np.minimum(np.arange(WINDOW // tq + 1) * tq, WINDOW)[:, None, None] + qi[None]
            - np.arange(band)[None, None, :])
    wbias = jnp.asarray(np.where((dist >= 0) & (dist < WINDOW), 0.0, NEG), F32)
    per_bg = lambda *shape: pl.BlockSpec((1, 1) + shape, lambda bi, gi, ci: (bi, gi, 0, 0))
    return pl.pallas_call(
        kern,
        grid=(b, N_KV_HEADS, t // tq),
        in_specs=[
            pl.BlockSpec((1, tq, KV_WIDTH), lambda bi, gi, ci: (bi, ci, gi)),
            pl.BlockSpec((1, tq, 128), lambda bi, gi, ci: (bi, ci, gi)),
            pl.BlockSpec((1, Q_PER_KV, tq, 128), lambda bi, gi, ci: (gi, 0, 0, 0)),
            per_bg(n_cmp_pad, 128),
            per_bg(n_cmp_pad, HEAD_DIM),
            per_bg(t, 256),
            per_bg(t, 128),
            per_bg(t, 128),
            per_bg(t, 128),
            pl.BlockSpec((n_sel, n_cmp_pad), lambda bi, gi, ci: (0, 0)),
            pl.BlockSpec((tq, tk), lambda bi, gi, ci: (0, 0)),
            pl.BlockSpec((tq, n_cmp_pad), lambda bi, gi, ci: (0, 0)),
            pl.BlockSpec((1, tq, band), lambda bi, gi, ci: (jnp.minimum(ci, WINDOW // tq), 0, 0)),
        ],
        out_specs=pl.BlockSpec((1, tq, KV_WIDTH), lambda bi, gi, ci: (bi, ci, gi)),
        out_shape=jax.ShapeDtypeStruct((b, t, N_HEADS * HEAD_DIM), F32),
        scratch_shapes=[
            pltpu.VMEM((rows, 256), BF16),
            pltpu.VMEM((n_sel, tq), F32),
            pltpu.VMEM((rows, 1), F32),
            pltpu.VMEM((rows, 1), F32),
            pltpu.VMEM((rows, 1), F32),
            pltpu.VMEM((rows, tk), F32),
            pltpu.VMEM((rows, tk), F32),
            pltpu.VMEM((rows, tk), BF16),
            pltpu.VMEM((rows, tk), BF16),
            pltpu.VMEM((rows, 128), F32),
        ],
        compiler_params=_params(("parallel", "parallel", "arbitrary"), 48),
        name="attn_prompt",
    )(q, gate_pre, slope_tab, kc_aug, vc, ks_aug, vs, kw_aug, vw, cmat_t, dmat, dcmp, wbias)


def _softmax_lanes(s):
    e = jnp.exp(s - jnp.max(s, axis=-1, keepdims=True))
    return e / jnp.sum(e, axis=-1, keepdims=True)


def _attn_sample_kernel(pt_ref, *refs, n_pages, past_len, n_sel):
    del pt_ref
    q_ref, gt_ref, sl_ref, kc_ref, vc_ref = refs[:5]
    page_refs = refs[5:5 + n_pages]
    new_ref, win_ref, ex_ref, cm_ref, rm_ref, o_ref, wn_ref = refs[5 + n_pages:]
    q = q_ref[0]
    sl = sl_ref[:, 0:1]
    new = new_ref[0]

    n_cmp_pad = kc_ref.shape[1]
    dist_c = (past_len - (CMP_BLOCK - 1)
              - CMP_STRIDE * lax.broadcasted_iota(jnp.int32, (N_HEADS, n_cmp_pad), 1)).astype(F32)
    s = _dot_nt(q, kc_ref[0]) - sl * dist_c
    mask = dist_c >= 0
    s = jnp.where(mask, s, NEG)
    e = jnp.where(mask, jnp.exp(s - jnp.max(s, axis=-1, keepdims=True)), 0.0)
    p_c = e / jnp.maximum(jnp.sum(e, axis=-1, keepdims=True), 1e-30)
    o_c = _dot(p_c.astype(BF16), vc_ref[0])

    hi, lo = _split_bf16(p_c)
    imp = _dot(hi, cm_ref[...]) + _dot(lo, cm_ref[...])
    hi, lo = _split_bf16(imp)
    imp = _dot(rm_ref[...], hi) + _dot(rm_ref[...], lo)
    n_blk = imp.shape[1]
    blk = lax.broadcasted_iota(jnp.int32, (N_HEADS, n_blk), 1)
    cur = past_len // SEL_BLOCK
    forced = (blk == 0) | (blk == cur) | (blk == cur - 1)
    valid = blk * SEL_BLOCK <= past_len
    pri = jnp.where(valid, jnp.where(forced, FORCED_PRIORITY, imp), -1.0)
    pri = jnp.where(blk < n_sel, pri, -2.0)
    rank = jnp.zeros((N_HEADS, n_blk), jnp.int32)
    for s2 in range(n_sel):
        col = pri[:, s2:s2 + 1]
        beats = (col > pri) | ((col == pri) & (blk > s2))
        rank = rank + beats.astype(jnp.int32)
    bias = jnp.where(rank < min(N_SELECT, n_sel), 0.0, NEG)

    k_sel = jnp.concatenate([jnp.concatenate([r[0, 0], r[0, 1]], axis=1) for r in page_refs],
                            axis=0).astype(BF16)
    v_sel = jnp.concatenate([jnp.concatenate([r[0, 2], r[0, 3]], axis=1) for r in page_refs],
                            axis=0).astype(BF16)
    dist_s = (past_len - lax.broadcasted_iota(jnp.int32, (N_HEADS, past_len), 1)).astype(F32)
    s = _dot_nt(q, k_sel) - sl * dist_s + _dot(bias.astype(BF16), ex_ref[...])
    qf = q.astype(F32)
    k_new = new[:, 2 * KV_WIDTH:3 * KV_WIDTH].astype(BF16).astype(F32)
    v_new = new[:, 3 * KV_WIDTH:4 * KV_WIDTH].astype(BF16).astype(F32)
    s_new = jnp.sum(qf * k_new, axis=-1, keepdims=True) + bias[:, past_len // SEL_BLOCK:past_len // SEL_BLOCK + 1]
    m = jnp.maximum(jnp.max(s, axis=-1, keepdims=True), s_new)
    e = jnp.exp(s - m)
    e_new = jnp.exp(s_new - m)
    l = jnp.sum(e, axis=-1, keepdims=True) + e_new
    o_s = (_dot(e.astype(BF16), v_sel) + e_new.astype(BF16).astype(F32) * v_new) / l

    wb = win_ref.shape[1]
    rolled = pltpu.roll(win_ref[0], wb - 1, 0)
    rowi = lax.broadcasted_iota(jnp.int32, rolled.shape, 0)
    wn = jnp.where(rowi == wb - 1, new[:, 4 * KV_WIDTH:6 * KV_WIDTH], rolled)
    wn_ref[0] = wn
    dist_w = (wb - 1 - lax.broadcasted_iota(jnp.int32, (N_HEADS, wb), 1)).astype(F32)
    p_w = _softmax_lanes(_dot_nt(q, wn[:, 0:KV_WIDTH].astype(BF16)) - sl * dist_w)
    o_w = _dot(p_w.astype(BF16), wn[:, KV_WIDTH:2 * KV_WIDTH].astype(BF16))

    gate = _sigmoid(gt_ref[0])
    o_ref[0] = gate[:, 0:1] * o_c + gate[:, 1:2] * o_s + gate[:, 2:3] * o_w


def _attn_sample(qbd, gate16, slope16, kc_all, vc_all, pages, page_table, new_rows, state_win, expand, cmat, rmat):
    nb, n_pages = page_table.shape
    page_rows = pages.shape[2]
    past_len = n_pages * page_rows
    n_sel = -(-(past_len + 1) // SEL_BLOCK)
    wb = state_win.shape[1]
    n_cmp_pad = kc_all.shape[1]
    kern = functools.partial(_attn_sample_kernel, n_pages=n_pages, past_len=past_len, n_sel=n_sel)
    per_b = lambda *shape: pl.BlockSpec((1,) + shape, lambda b, pt: (b,) + (0,) * len(shape))
    const = lambda *shape: pl.BlockSpec(shape, lambda b, pt: (0,) * len(shape))
    page_specs = [
        pl.BlockSpec((1, 4, page_rows, 128), functools.partial(lambda p, b, pt: (pt[b, p], 1, 0, 0), p))
        for p in range(n_pages)]
    return pl.pallas_call(
        kern,
        grid_spec=pltpu.PrefetchScalarGridSpec(
            num_scalar_prefetch=1,
            grid=(nb,),
            in_specs=[per_b(N_HEADS, KV_WIDTH), per_b(N_HEADS, 128), const(N_HEADS, 128),
                      per_b(n_cmp_pad, KV_WIDTH), per_b(n_cmp_pad, KV_WIDTH)]
            + page_specs
            + [per_b(1, 6 * KV_WIDTH), per_b(wb, 2 * KV_WIDTH), const(*expand.shape), const(*cmat.shape),
               const(*rmat.shape)],
            out_specs=[per_b(N_HEADS, KV_WIDTH), per_b(wb, 2 * KV_WIDTH)],
        ),
        out_shape=[jax.ShapeDtypeStruct((nb, N_HEADS, KV_WIDTH), F32),
                   jax.ShapeDtypeStruct((nb, wb, 2 * KV_WIDTH), F32)],
        compiler_params=_params(("parallel",), 48),
        name="attn_sample",
    )(page_table, qbd, gate16, slope16, kc_all, vc_all, *([pages] * n_pages), new_rows, state_win, expand, cmat,
      rmat)


def _alibi_slopes():
    return jnp.exp2(-8.0 * (jnp.arange(N_HEADS, dtype=F32) + 1.0) / N_HEADS)


def _cmp_to_sel(n_cmp_pad, n_cmp, n_sel_pad):
    m = np.zeros((n_cmp_pad, n_sel_pad), np.float32)
    for n in range(n_cmp):
        for k in range(CMP_BLOCK // CMP_STRIDE):
            m[n, (n + k) * CMP_STRIDE // SEL_BLOCK] += 1.0
    return m


def _pos_cols(pos, width):
    cols = np.zeros((pos.shape[0], width), np.float32)
    cols[:, 0] = cols[:, 1] = pos % 64
    cols[:, 2] = cols[:, 3] = pos // 64
    return cols


def kernel(x_prompt, x_sample, state_pool, cache_kv_pages, state_win, page_table, norm_mix, norm_mlp, w_up, w_down,
           pool_w, pool_scale, norm_kv, w_kv, cmp_pe, cmp_w1, cmp_w2, w_qg, b_gate, w_o, norm_final):
    b, t, d = x_prompt.shape
    nb = x_sample.shape[0]
    n_phys, page_rows = cache_kv_pages.shape[:2]
    n_q = N_HEADS * HEAD_DIM
    row = lambda v: v.reshape(1, -1)

    wu = w_up.astype(BF16)
    wd = w_down.astype(BF16)
    pw = pool_w[0].astype(BF16)
    wkv = w_kv.astype(BF16)
    wo = w_o[0].astype(BF16)
    hh = np.arange(N_HEADS)
    gcols = ((hh // Q_PER_KV) * 128 + (hh % Q_PER_KV) * N_BRANCH)[:, None] + np.arange(N_BRANCH)[None, :]
    gcols = gcols.reshape(-1)
    wg = jnp.zeros((d, N_KV_HEADS * 128), F32).at[:, gcols].set(w_qg[0][:, n_q:])
    wqg = jnp.concatenate([w_qg[0][:, :n_q], wg], axis=1).astype(BF16)
    bqg = jnp.zeros((1, n_q + N_KV_HEADS * 128), F32).at[0, n_q + gcols].set(b_gate[0])
    w1_bf = cmp_w1.astype(BF16)
    w1r = w1_bf.reshape(2, 2, CMP_STRIDE, HEAD_DIM, CMP_HIDDEN)
    w1c = jnp.concatenate([w1r[:, 0], w1r[:, 1]], axis=-1)
    w1d = jnp.concatenate([jnp.concatenate([w1c, jnp.zeros_like(w1c)], axis=-1),
                           jnp.concatenate([jnp.zeros_like(w1c), w1c], axis=-1)], axis=2)
    w1d = w1d.reshape(2, CMP_STRIDE // 2, 4 * HEAD_DIM, 4 * CMP_HIDDEN)
    pe8 = jnp.broadcast_to(cmp_pe.reshape(2, 1, CMP_BLOCK * HEAD_DIM), (2, 8, CMP_BLOCK * HEAD_DIM))
    w2_bf = cmp_w2.astype(BF16)

    slopes = _alibi_slopes()
    s_hi = slopes.astype(BF16).astype(F32)
    s_lo = (slopes - s_hi).astype(BF16).astype(F32)
    slope_cols = jnp.stack([s_hi, s_lo, 64.0 * s_hi, 64.0 * s_lo], axis=-1)

    x1, u_last = _pool_prompt(x_prompt, row(norm_mix[0]), pw, row(pool_scale[0]))
    pool_p = u_last[:, None, 16 - POOL_BUF:, :]
    x2 = _mlp(x1.reshape(b * t, d), row(norm_mlp[0]), wu[0], wd[0], row(norm_final), final_norm=False, tm=1024)
    kv_rows, kv_win = _lin(x2, wkv, g=row(norm_kv), splits=(ROW_WIDTH, 2 * KV_WIDTH), tm=512)
    kv_rows_p = kv_rows.reshape(b, t, 2, 2, N_KV_HEADS, HEAD_DIM)
    win_p = kv_win.reshape(b, t, 2, N_KV_HEADS, HEAD_DIM)
    win_new_p = win_p[:, -min(WINDOW, t):]

    dummy_pt = jnp.zeros((b, 1), jnp.int32)
    kc_p, vc_p = _compress(kv_rows.reshape(b, t, ROW_WIDTH), dummy_pt, w1d, w1_bf, pe8, w2_bf, paged=False)

    q_p, gate_p = _lin(x2, wqg, g=row(norm_mix[1]), b=bqg, splits=(n_q, N_KV_HEADS * 128), tm=512)

    tq = 128
    n_sub_p = t // CMP_STRIDE
    n_sel_p = t // SEL_BLOCK
    head_major = lambda a: a.transpose(0, 2, 1, 3).astype(BF16)
    tpos = np.arange(t)
    pc_t = jnp.asarray(_pos_cols(tpos, HEAD_DIM), BF16)
    onehot_t = jnp.asarray(np.eye(n_sel_p, 128, dtype=np.float32)[tpos // SEL_BLOCK], BF16)
    bcast = lambda a: jnp.broadcast_to(a[None, None], (b, N_KV_HEADS) + a.shape)
    ks_aug = jnp.concatenate([head_major(kv_rows_p[:, :, 1, 0]), bcast(pc_t), bcast(onehot_t)], axis=-1)
    ones_col = jnp.asarray(np.eye(1, HEAD_DIM, dtype=np.float32).repeat(t, axis=0), BF16)
    vs = jnp.concatenate([head_major(kv_rows_p[:, :, 1, 1]), bcast(ones_col)], axis=-1)
    kw_aug = jnp.concatenate([head_major(win_p[:, :, 0]), bcast(pc_t)], axis=-1)
    vw = jnp.concatenate([head_major(win_p[:, :, 1]), bcast(ones_col)], axis=-1)
    pc_c = jnp.asarray(_pos_cols(CMP_STRIDE * np.arange(n_sub_p) + CMP_BLOCK - 1, HEAD_DIM), BF16)
    kc_aug = jnp.concatenate([kc_p.astype(BF16), bcast(pc_c)], axis=-1)
    cmat_t = jnp.asarray(_cmp_to_sel(n_sub_p, n_sub_p - 1, n_sel_p).T, BF16)
    slope_tab = jnp.zeros((N_KV_HEADS, Q_PER_KV, tq, 128), F32).at[:, :, :, HEAD_DIM:HEAD_DIM + 4].set(
        jnp.broadcast_to(slope_cols.reshape(N_KV_HEADS, Q_PER_KV, 1, 4), (N_KV_HEADS, Q_PER_KV, tq, 4)))
    o_p = _attn_prompt(q_p.reshape(b, t, n_q), gate_p.reshape(b, t, N_KV_HEADS * 128), slope_tab, kc_aug,
                       vc_p.astype(BF16), ks_aug, vs, kw_aug, vw, cmat_t, tq=tq)
    x3, = _lin(o_p.reshape(b * t, n_q), wo, res=x2, tm=512)
    y_prompt = _mlp(x3, row(norm_mlp[1]), wu[1], wd[1], row(norm_final), final_norm=True, tm=1024).reshape(b, t, d)

    xs0 = x_sample.reshape(nb, d)
    xs1, u_s = _pool_sample(xs0, state_pool[:, 0], row(norm_mix[0]), pw, row(pool_scale[0]))
    pool_s = jnp.concatenate([state_pool[:, 0, 1:], u_s[:, None]], axis=1)[:, None]
    xs2 = _mlp(xs1, row(norm_mlp[0]), wu[0], wd[0], row(norm_final), final_norm=False, tm=nb)
    kv_s, = _lin(xs2, wkv, g=row(norm_kv), tm=nb)
    kv_rows_s = kv_s[:, :ROW_WIDTH].reshape(nb, 1, 2, 2, N_KV_HEADS, HEAD_DIM)

    pages = cache_kv_pages.reshape(n_phys, page_rows, ROW_WIDTH // 128, 128).transpose(0, 2, 1, 3)
    kc_s, vc_s = _compress(pages, page_table, w1d, w1_bf, pe8, w2_bf, paged=True)
    n_sub_s = kc_s.shape[2]
    all_heads = lambda a: a.transpose(0, 2, 1, 3).reshape(nb, n_sub_s, KV_WIDTH).astype(BF16)

    q_s, gate_s = _lin(xs2, wqg, g=row(norm_mix[1]), b=bqg, splits=(n_q, N_KV_HEADS * 128), tm=nb)
    q4 = q_s.reshape(nb, N_KV_HEADS, Q_PER_KV, HEAD_DIM) * (HEAD_DIM ** -0.5)
    qbd = jnp.einsum('bgrd,gh->bgrhd', q4, jnp.eye(N_KV_HEADS, dtype=F32)).reshape(nb, N_HEADS, KV_WIDTH)
    g16 = gate_s.reshape(nb, N_KV_HEADS, 128)[:, :, :Q_PER_KV * N_BRANCH].reshape(nb, N_HEADS, N_BRANCH)
    g16 = jnp.pad(g16, ((0, 0), (0, 0), (0, 128 - N_BRANCH)))
    slope16 = jnp.broadcast_to(slopes[:, None], (N_HEADS, 128))
    past_len = page_table.shape[1] * page_rows
    n_blk_pad = 64
    expand = jnp.asarray(np.eye(n_blk_pad, dtype=np.float32)[:, np.arange(past_len) // SEL_BLOCK], BF16)
    cmat_s = jnp.asarray(_cmp_to_sel(n_sub_s, n_sub_s - 1, n_blk_pad), BF16)
    rmat = jnp.asarray(np.kron(np.eye(N_KV_HEADS), np.ones((Q_PER_KV, Q_PER_KV))), BF16)
    o_s, win_new_s = _attn_sample(
        qbd.astype(BF16), g16, slope16, all_heads(kc_s), all_heads(vc_s), pages, page_table,
        kv_s.reshape(nb, 1, 6 * KV_WIDTH), state_win.reshape(nb, state_win.shape[1], 2 * KV_WIDTH),
        expand, cmat_s, rmat)
    o5 = o_s.reshape(nb, N_KV_HEADS, Q_PER_KV, N_KV_HEADS, HEAD_DIM)
    o_heads = jnp.stack([o5[:, gh, :, gh] for gh in range(N_KV_HEADS)], axis=1).reshape(nb, n_q)
    xs3, = _lin(o_heads, wo, res=xs2, tm=nb)
    y_sample = _mlp(xs3, row(norm_mlp[1]), wu[1], wd[1], row(norm_final), final_norm=True, tm=nb).reshape(nb, 1, d)

    return (y_prompt, y_sample, pool_p, pool_s, kv_rows_p, kv_rows_s, win_new_p,
            win_new_s.reshape(state_win.shape))
```

```python
import functools

import numpy as np
import jax
import jax.numpy as jnp
from jax import lax
from jax.experimental import pallas as pl
from jax.experimental.pallas import tpu as pltpu

D_MODEL = 1024
POOL_WINDOWS = (2, 4, 8, 16)
POOL_GROUP_DIM = D_MODEL // len(POOL_WINDOWS)
POOL_BUF = max(POOL_WINDOWS) - 1
N_HEADS = 16
HEAD_DIM = 64
N_KV_HEADS = 4
Q_PER_KV = N_HEADS // N_KV_HEADS
N_BRANCH = 3
CMP_BLOCK = 32
CMP_STRIDE = 16
CMP_HIDDEN = 2 * HEAD_DIM
SEL_BLOCK = 64
N_SELECT = 16
WINDOW = 512
RMS_EPS = 1e-6
FORCED_PRIORITY = 1e6
KV_WIDTH = N_KV_HEADS * HEAD_DIM
ROW_WIDTH = 2 * 2 * KV_WIDTH

NEG = -1e30
F32 = jnp.float32
BF16 = jnp.bfloat16
MIB = 1024 * 1024


def _params(semantics, vmem_mib):
    return pltpu.CompilerParams(dimension_semantics=semantics, vmem_limit_bytes=vmem_mib * MIB)


def _rms(x, g):
    return x * lax.rsqrt(jnp.mean(x * x, axis=-1, keepdims=True) + RMS_EPS) * g


def _dot(a, b):
    return jnp.dot(a, b, preferred_element_type=F32)


def _dot_nt(a, b):
    return lax.dot_general(a, b, (((1,), (1,)), ((), ())), preferred_element_type=F32)


def _split_bf16(x):
    hi = x.astype(BF16)
    lo = (x - hi.astype(F32)).astype(BF16)
    return hi, lo


def _sigmoid(x):
    return 1.0 / (1.0 + jnp.exp(-x))


def _pool_prompt_kernel(x_ref, xp_ref, g_ref, w_ref, sc_ref, y_ref, ul_ref, u_scr, *, tt):
    i = pl.program_id(1)
    g = g_ref[...]
    x = x_ref[0]
    u = _rms(x, g)
    up = _rms(xp_ref[0], g) * (i > 0).astype(F32)
    u_scr[0:16, :] = up
    u_scr[16:16 + tt, :] = u
    t = i * tt + lax.broadcasted_iota(jnp.int32, (tt, 1), 0)
    for gi, win in enumerate(POOL_WINDOWS):
        cols = slice(gi * POOL_GROUP_DIM, (gi + 1) * POOL_GROUP_DIM)
        acc = u[:, cols]
        for k in range(1, win):
            acc = acc + u_scr[16 - k:16 - k + tt, cols]
        cnt = jnp.minimum(t + 1, win).astype(F32)
        diff = acc / cnt - u[:, cols]
        yg = _dot(diff.astype(BF16), w_ref[gi])
        y_ref[0, :, cols] = x[:, cols] + yg * sc_ref[:, cols]
    ul_ref[0] = u[tt - 16:, :]


def _pool_prompt(x, g, w_bf, scale, *, tt=256):
    b, t, d = x.shape
    n_t = t // tt
    kern = functools.partial(_pool_prompt_kernel, tt=tt)
    return pl.pallas_call(
        kern,
        grid=(b, n_t),
        in_specs=[
            pl.BlockSpec((1, tt, d), lambda bi, i: (bi, i, 0)),
            pl.BlockSpec((1, 16, d), lambda bi, i: (bi, jnp.maximum(i * (tt // 16) - 1, 0), 0)),
            pl.BlockSpec((1, d), lambda bi, i: (0, 0)),
            pl.BlockSpec((len(POOL_WINDOWS), POOL_GROUP_DIM, POOL_GROUP_DIM), lambda bi, i: (0, 0, 0)),
            pl.BlockSpec((1, d), lambda bi, i: (0, 0)),
        ],
        out_specs=[
            pl.BlockSpec((1, tt, d), lambda bi, i: (bi, i, 0)),
            pl.BlockSpec((1, 16, d), lambda bi, i: (bi, 0, 0)),
        ],
        out_shape=[jax.ShapeDtypeStruct((b, t, d), F32), jax.ShapeDtypeStruct((b, 16, d), F32)],
        scratch_shapes=[pltpu.VMEM((16 + tt, d), F32)],
        compiler_params=_params(("parallel", "arbitrary"), 32),
        name="pool_prompt",
    )(x, x, g, w_bf, scale)


def _pool_sample_kernel(x_ref, past_ref, g_ref, w_ref, sc_ref, y_ref, u_ref):
    x = x_ref[...]
    u = _rms(x, g_ref[...])
    for gi, win in enumerate(POOL_WINDOWS):
        cols = slice(gi * POOL_GROUP_DIM, (gi + 1) * POOL_GROUP_DIM)
        acc = u[:, cols]
        for k in range(1, win):
            acc = acc + past_ref[:, POOL_BUF - k, cols]
        diff = acc / float(win) - u[:, cols]
        yg = _dot(diff.astype(BF16), w_ref[gi])
        y_ref[:, cols] = x[:, cols] + yg * sc_ref[:, cols]
    u_ref[...] = u


def _pool_sample(x, past, g, w_bf, scale, *, bt=32):
    nb, d = x.shape
    return pl.pallas_call(
        _pool_sample_kernel,
        grid=(nb // bt,),
        in_specs=[
            pl.BlockSpec((bt, d), lambda i: (i, 0)),
            pl.BlockSpec((bt, POOL_BUF, d), lambda i: (i, 0, 0)),
            pl.BlockSpec((1, d), lambda i: (0, 0)),
            pl.BlockSpec((len(POOL_WINDOWS), POOL_GROUP_DIM, POOL_GROUP_DIM), lambda i: (0, 0, 0)),
            pl.BlockSpec((1, d), lambda i: (0, 0)),
        ],
        out_specs=[pl.BlockSpec((bt, d), lambda i: (i, 0)), pl.BlockSpec((bt, d), lambda i: (i, 0))],
        out_shape=[jax.ShapeDtypeStruct((nb, d), F32), jax.ShapeDtypeStruct((nb, d), F32)],
        compiler_params=_params(("parallel",), 32),
        name="pool_sample",
    )(x, past, g, w_bf, scale)


def _mlp_kernel(x_ref, g_ref, wu_ref, wd_ref, gf_ref, o_ref, xn_scr, acc_scr, *, final_norm):
    j = pl.program_id(1)

    @pl.when(j == 0)
    def _():
        xn_scr[...] = _rms(x_ref[...], g_ref[...]).astype(BF16)
        acc_scr[...] = jnp.zeros_like(acc_scr)

    h = jnp.maximum(_dot(xn_scr[...], wu_ref[...]), 0.0)
    acc_scr[...] += _dot((h * h).astype(BF16), wd_ref[...])

    @pl.when(j == pl.num_programs(1) - 1)
    def _():
        r = x_ref[...] + acc_scr[...]
        o_ref[...] = _rms(r, gf_ref[...]) if final_norm else r


def _mlp(x, g, wu_bf, wd_bf, gf, *, final_norm, tm, tf=1024):
    m, d = x.shape
    f = wu_bf.shape[1]
    kern = functools.partial(_mlp_kernel, final_norm=final_norm)
    return pl.pallas_call(
        kern,
        grid=(m // tm, f // tf),
        in_specs=[
            pl.BlockSpec((tm, d), lambda i, j: (i, 0)),
            pl.BlockSpec((1, d), lambda i, j: (0, 0)),
            pl.BlockSpec((d, tf), lambda i, j: (0, j)),
            pl.BlockSpec((tf, d), lambda i, j: (j, 0)),
            pl.BlockSpec((1, d), lambda i, j: (0, 0)),
        ],
        out_specs=pl.BlockSpec((tm, d), lambda i, j: (i, 0)),
        out_shape=jax.ShapeDtypeStruct((m, d), F32),
        scratch_shapes=[pltpu.VMEM((tm, d), BF16), pltpu.VMEM((tm, d), F32)],
        compiler_params=_params(("parallel", "arbitrary"), 48),
        name="mlp",
    )(x, g, wu_bf, wd_bf, gf)


def _proj_kernel(x_ref, gkv_ref, gq_ref, wkv_ref, wqg_ref, bqg_ref, pc_ref, oh_ref,
                 rows_ref, win_ref, q_ref, gate_ref, ks_ref, vs_ref, kw_ref, vw_ref):
    x = x_ref[...]
    xh = x * lax.rsqrt(jnp.mean(x * x, axis=-1, keepdims=True) + RMS_EPS)
    hkv = _dot((xh * gkv_ref[...]).astype(BF16), wkv_ref[...])
    hq = _dot((xh * gq_ref[...]).astype(BF16), wqg_ref[...]) + bqg_ref[...]
    n_q = N_HEADS * HEAD_DIM
    rows_ref[...] = hkv[:, :ROW_WIDTH]
    win_ref[...] = hkv[:, ROW_WIDTH:]
    q_ref[...] = hq[:, :n_q]
    gate_ref[...] = hq[:, n_q:]

    lane = lax.broadcasted_iota(jnp.int32, (x.shape[0], 128), 1)
    pc = pc_ref[...]
    ones = (lane == HEAD_DIM).astype(F32)

    def head(col0, gh, filler):
        pair = hkv[:, col0 + (gh // 2) * 128:col0 + (gh // 2 + 1) * 128]
        if gh % 2:
            pair = pltpu.roll(pair, HEAD_DIM, 1)
        return jnp.where(lane < HEAD_DIM, pair, filler).astype(BF16)

    for gh in range(N_KV_HEADS):
        ks_ref[0, gh, :, 0:128] = head(2 * KV_WIDTH, gh, pc)
        ks_ref[0, gh, :, 128:256] = oh_ref[...]
        vs_ref[0, gh] = head(3 * KV_WIDTH, gh, ones)
        kw_ref[0, gh] = head(4 * KV_WIDTH, gh, pc)
        vw_ref[0, gh] = head(5 * KV_WIDTH, gh, ones)


def _proj(x, g_kv, g_q, wkv_bf, wqg_bf, bqg, pos_cols, onehot, *, b, t, tm=512):
    m, d = x.shape
    nt = t // tm
    n_kv = wkv_bf.shape[1]
    n_qg = wqg_bf.shape[1]
    n_q = N_HEADS * HEAD_DIM
    const = lambda *shape: pl.BlockSpec(shape, lambda i: (0,) * len(shape))
    rows_of = lambda n: pl.BlockSpec((tm, n), lambda i: (i, 0))
    heads_of = lambda n: pl.BlockSpec((1, N_KV_HEADS, tm, n), lambda i: (i // nt, 0, i % nt, 0))
    heads_shape = lambda n: jax.ShapeDtypeStruct((b, N_KV_HEADS, t, n), BF16)
    return pl.pallas_call(
        _proj_kernel,
        grid=(m // tm,),
        in_specs=[rows_of(d), const(1, d), const(1, d), const(d, n_kv), const(d, n_qg), const(1, n_qg),
                  pl.BlockSpec((tm, 128), lambda i: (i % nt, 0)), pl.BlockSpec((tm, 128), lambda i: (i % nt, 0))],
        out_specs=[rows_of(ROW_WIDTH), rows_of(n_kv - ROW_WIDTH), rows_of(n_q), rows_of(n_qg - n_q),
                   heads_of(256), heads_of(128), heads_of(128), heads_of(128)],
        out_shape=[jax.ShapeDtypeStruct((m, ROW_WIDTH), F32), jax.ShapeDtypeStruct((m, n_kv - ROW_WIDTH), F32),
                   jax.ShapeDtypeStruct((m, n_q), F32), jax.ShapeDtypeStruct((m, n_qg - n_q), F32),
                   heads_shape(256), heads_shape(128), heads_shape(128), heads_shape(128)],
        compiler_params=_params(("parallel",), 56),
        name="proj",
    )(x, g_kv, g_q, wkv_bf, wqg_bf, bqg, pos_cols, onehot)


def _lin_kernel(*refs, norm, bias, residual, splits):
    refs = list(refs)
    x_ref = refs.pop(0)
    g_ref = refs.pop(0) if norm else None
    w_ref = refs.pop(0)
    b_ref = refs.pop(0) if bias else None
    r_ref = refs.pop(0) if residual else None
    x = x_ref[...]
    if norm:
        x = _rms(x, g_ref[...])
    h = _dot(x.astype(BF16), w_ref[...])
    if bias:
        h = h + b_ref[...]
    if residual:
        h = h + r_ref[...]
    c0 = 0
    for o_ref, n in zip(refs, splits):
        o_ref[...] = h[:, c0:c0 + n]
        c0 += n


def _lin(x, w_bf, *, g=None, b=None, res=None, splits=None, tm):
    m, k = x.shape
    n = w_bf.shape[1]
    splits = splits or (n,)
    kern = functools.partial(_lin_kernel, norm=g is not None, bias=b is not None,
                             residual=res is not None, splits=splits)
    args = [x]
    in_specs = [pl.BlockSpec((tm, k), lambda i: (i, 0))]
    if g is not None:
        args.append(g)
        in_specs.append(pl.BlockSpec((1, k), lambda i: (0, 0)))
    args.append(w_bf)
    in_specs.append(pl.BlockSpec((k, n), lambda i: (0, 0)))
    if b is not None:
        args.append(b)
        in_specs.append(pl.BlockSpec((1, n), lambda i: (0, 0)))
    if res is not None:
        args.append(res)
        in_specs.append(pl.BlockSpec((tm, n), lambda i: (i, 0)))
    outs = pl.pallas_call(
        kern,
        grid=(m // tm,),
        in_specs=in_specs,
        out_specs=[pl.BlockSpec((tm, s), lambda i: (i, 0)) for s in splits],
        out_shape=[jax.ShapeDtypeStruct((m, s), F32) for s in splits],
        compiler_params=_params(("parallel",), 48),
        name="lin",
    )(*args)
    return outs


def _compress_kernel(pt_ref, *refs, n_page_refs, paged, rows_per_ref, n_sub):
    del pt_ref
    page_refs = refs[:n_page_refs]
    w1d_ref, w1_ref, pe_ref, w2_ref, kc_ref, vc_ref = refs[n_page_refs:]
    n_take = rows_per_ref // CMP_STRIDE

    def tap_rows(c, j):
        take = pl.ds(j, n_take, stride=CMP_STRIDE)
        if paged:
            return jnp.concatenate([r[0, c, take, :] for r in page_refs], axis=0)
        return page_refs[c][0, take, :]

    for kv, o_ref in enumerate((kc_ref, vc_ref)):
        pew = _dot(pe_ref[kv].astype(BF16), w1_ref[kv])[0:1]
        for c2 in range(2):
            acc = None
            for jp in range(CMP_STRIDE // 2):
                lhs = jnp.concatenate([tap_rows(2 * kv + c2, 2 * jp), tap_rows(2 * kv + c2, 2 * jp + 1)], axis=1)
                d = _dot(lhs.astype(BF16), w1d_ref[kv, jp])
                acc = d if acc is None else acc + d
            for e in range(2):
                a = acc[:, 2 * e * CMP_HIDDEN:(2 * e + 1) * CMP_HIDDEN]
                b2 = acc[:, (2 * e + 1) * CMP_HIDDEN:(2 * e + 2) * CMP_HIDDEN]
                hid = a + pltpu.roll(b2, n_sub - 1, 0) + pew
                act = hid * _sigmoid(hid)
                o_ref[0, 2 * c2 + e] = _dot(act.astype(BF16), w2_ref[kv])


def _compress(pages, page_table, w1d_bf, w1_bf, pe8, w2_bf, *, paged):
    if paged:
        nb, n_pages = page_table.shape
        rows_per_ref = pages.shape[2]
        n_sub = n_pages * rows_per_ref // CMP_STRIDE
        page_specs = [
            pl.BlockSpec((1, 4, rows_per_ref, 128), functools.partial(lambda p, b, pt: (pt[b, p], 0, 0, 0), p))
            for p in range(n_pages)]
    else:
        nb, rows_per_ref, _ = pages.shape
        n_sub = rows_per_ref // CMP_STRIDE
        page_specs = [pl.BlockSpec((1, rows_per_ref, 128), functools.partial(lambda c, b, pt: (b, 0, c), c))
                      for c in range(4)]
    page_args = [pages] * len(page_specs)
    kern = functools.partial(_compress_kernel, n_page_refs=len(page_args), paged=paged, rows_per_ref=rows_per_ref,
                             n_sub=n_sub)
    const = lambda *shape: pl.BlockSpec(shape, lambda b, pt: (0,) * len(shape))
    out_spec = pl.BlockSpec((1, N_KV_HEADS, n_sub, HEAD_DIM), lambda b, pt: (b, 0, 0, 0))
    return pl.pallas_call(
        kern,
        grid_spec=pltpu.PrefetchScalarGridSpec(
            num_scalar_prefetch=1,
            grid=(nb,),
            in_specs=page_specs + [
                const(2, CMP_STRIDE // 2, 4 * HEAD_DIM, 4 * CMP_HIDDEN),
                const(2, CMP_BLOCK * HEAD_DIM, CMP_HIDDEN),
                const(2, 8, CMP_BLOCK * HEAD_DIM),
                const(2, CMP_HIDDEN, HEAD_DIM),
            ],
            out_specs=[out_spec, out_spec],
        ),
        out_shape=[jax.ShapeDtypeStruct((nb, N_KV_HEADS, n_sub, HEAD_DIM), F32)] * 2,
        compiler_params=_params(("parallel",), 48),
        name="compress",
    )(page_table, *page_args, w1d_bf, w1_bf, pe8, w2_bf)


def _attn_prompt_kernel(q_ref, gt_ref, st_ref, kc_ref, vc_ref, ks_ref, vs_ref, kw_ref, vw_ref, cm_ref, dm_ref,
                        dc_ref, wb_ref, o_ref, qa_scr, pri_scr, m_scr, ala_scr, alb_scr, sa_scr, sb_scr, pa_scr,
                        pb_scr, acc_scr, *, tq, tk, rc, n_sel):
    c0 = pl.program_id(2) * tq
    rows = Q_PER_KV * tq
    band = WINDOW + tq

    lane = lax.broadcasted_iota(jnp.int32, (tq, 128), 1)
    qb = q_ref[0] * (HEAD_DIM ** -0.5)
    for r in range(Q_PER_KV):
        pair = qb[:, (r // 2) * 128:(r // 2 + 1) * 128]
        if r % 2:
            pair = pltpu.roll(pair, HEAD_DIM, 1)
        qa_scr[r * tq:(r + 1) * tq, 0:128] = jnp.where(lane < HEAD_DIM, pair, st_ref[0, r]).astype(BF16)
    ql = qa_scr[:, 0:128]

    n_cmp_pad = kc_ref.shape[2]
    s = _dot_nt(ql, kc_ref[0, 0])
    mask = jnp.concatenate([dc_ref[...]] * Q_PER_KV, axis=0) <= c0
    s = jnp.where(mask, s, NEG)
    m = jnp.max(s, axis=-1, keepdims=True)
    e = jnp.where(mask, jnp.exp(s - m), 0.0)
    p = e / jnp.maximum(jnp.sum(e, axis=-1, keepdims=True), 1e-30)
    o_c = _dot(p.astype(BF16), vc_ref[0, 0])

    psum = p[0:tq] + p[tq:2 * tq] + p[2 * tq:3 * tq] + p[3 * tq:4 * tq]
    hi, lo = _split_bf16(psum)
    imp = _dot_nt(cm_ref[...], hi) + _dot_nt(cm_ref[...], lo)
    blk = lax.broadcasted_iota(jnp.int32, (n_sel, tq), 0)
    tcol = c0 + lax.broadcasted_iota(jnp.int32, (n_sel, tq), 1)
    cur = tcol >> 6
    forced = (blk == 0) | (blk == cur) | (blk == cur - 1)
    valid = blk * SEL_BLOCK <= tcol
    pri = jnp.where(valid, jnp.where(forced, FORCED_PRIORITY, imp), -1.0)
    pri_scr[...] = pri
    n_valid = c0 // SEL_BLOCK + tq // SEL_BLOCK
    n_walk = jnp.where(n_valid > N_SELECT, (n_valid + 7) // 8, 0)

    def walk(g8, rank):
        grp = pri_scr[pl.ds(pl.multiple_of(g8 * 8, 8), 8), :]
        for i in range(8):
            row = grp[i:i + 1, :]
            beats = (row > pri) | ((row == pri) & (blk > g8 * 8 + i))
            rank = rank + beats.astype(jnp.int32)
        return rank

    rank = lax.fori_loop(0, n_walk, walk, jnp.zeros((n_sel, tq), jnp.int32))
    bias = jnp.where(rank < min(N_SELECT, n_sel), 0.0, NEG).T
    right = jnp.concatenate([bias, jnp.zeros((tq, 128 - n_sel), F32)], axis=1).astype(BF16)
    for r in range(Q_PER_KV):
        qa_scr[r * tq:(r + 1) * tq, 128:256] = right

    m_scr[...] = jnp.full_like(m_scr, NEG)
    acc_scr[...] = jnp.zeros_like(acc_scr)

    def scores(j):
        return _dot_nt(qa_scr[...], ks_ref[0, 0, pl.ds(pl.multiple_of(j * tk, tk), tk), :])

    def soft_pv(s_scr, p_scr, al_scr, j, masked):
        k0 = pl.multiple_of(j * tk, tk)
        for i in range(rows // rc):
            rs = slice(i * rc, (i + 1) * rc)
            qs = (i * rc) % tq
            sc = s_scr[rs, :]
            if masked:
                sc = jnp.where(dm_ref[qs:qs + rc, :] <= c0 - k0, sc, NEG)
            m_old = m_scr[rs]
            m_new = jnp.maximum(m_old, jnp.max(sc, axis=-1, keepdims=True))
            al_scr[rs] = jnp.exp(m_old - m_new)
            p_scr[rs] = jnp.exp(sc - m_new).astype(BF16)
            m_scr[rs] = m_new
        acc_scr[...] = al_scr[...] * acc_scr[...] + _dot(p_scr[...], vs_ref[0, 0, pl.ds(k0, tk), :])

    n_tiles = c0 // tk + 1
    n_loop = (n_tiles - 1) // 2
    sa_scr[...] = scores(0)

    def tile_pair(jj, carry):
        sb_scr[...] = scores(2 * jj + 1)
        soft_pv(sa_scr, pa_scr, ala_scr, 2 * jj, False)
        sa_scr[...] = scores(2 * jj + 2)
        soft_pv(sb_scr, pb_scr, alb_scr, 2 * jj + 1, False)
        return carry

    lax.fori_loop(0, n_loop, tile_pair, 0)
    sb_scr[...] = scores(2 * n_loop + 1)
    soft_pv(sa_scr, pa_scr, ala_scr, 2 * n_loop, True)
    soft_pv(sb_scr, pb_scr, alb_scr, 2 * n_loop + 1, True)
    acc = acc_scr[...]
    o_s = acc[:, :HEAD_DIM] / acc[:, HEAD_DIM:HEAD_DIM + 1]

    w0 = pl.multiple_of(jnp.maximum(c0 - WINDOW, 0), 128)
    sw = _dot_nt(ql, kw_ref[0, 0, pl.ds(w0, band), :]) + jnp.concatenate([wb_ref[0]] * Q_PER_KV, axis=0)
    pw = jnp.exp(sw - jnp.max(sw, axis=-1, keepdims=True))
    aw = _dot(pw.astype(BF16), vw_ref[0, 0, pl.ds(w0, band), :])
    o_w = aw[:, :HEAD_DIM] / aw[:, HEAD_DIM:HEAD_DIM + 1]

    gate = _sigmoid(gt_ref[0])
    outs = []
    for r in range(Q_PER_KV):
        rs = slice(r * tq, (r + 1) * tq)
        outs.append(gate[:, 3 * r:3 * r + 1] * o_c[rs] + gate[:, 3 * r + 1:3 * r + 2] * o_s[rs]
                    + gate[:, 3 * r + 2:3 * r + 3] * o_w[rs])
    o_ref[0] = jnp.concatenate(outs, axis=1)


def _attn_prompt(q, gate_pre, slope_tab, kc_aug, vc, ks_aug, vs, kw_aug, vw, cmat_t, *, tq=128, tk=512, rc=64):
    b, t, _ = q.shape
    n_sel = t // SEL_BLOCK
    n_cmp_pad = kc_aug.shape[2]
    kern = functools.partial(_attn_prompt_kernel, tq=tq, tk=tk, rc=rc, n_sel=n_sel)
    rows = Q_PER_KV * tq
    band = WINDOW + tq
    qi = np.arange(tq)[:, None]
    dmat = jnp.asarray(np.arange(tk)[None, :] - qi, jnp.int32)
    dcmp = jnp.asarray(CMP_STRIDE * np.arange(n_cmp_pad)[None, :] + (CMP_BLOCK - 1) - qi, jnp.int32)
    dist = (np.minimum(np.arange(WINDOW // tq + 1) * tq, WINDOW)[:, None, None] + qi[None]
            - np.arange(band)[None, None, :])
    wbias = jnp.asarray(np.where((dist >= 0) & (dist < WINDOW), 0.0, NEG), F32)
    per_bg = lambda *shape: pl.BlockSpec((1, 1) + shape, lambda bi, gi, ci: (bi, gi, 0, 0))
    return pl.pallas_call(
        kern,
        grid=(b, N_KV_HEADS, t // tq),
        in_specs=[
            pl.BlockSpec((1, tq, KV_WIDTH), lambda bi, gi, ci: (bi, ci, gi)),
            pl.BlockSpec((1, tq, 128), lambda bi, gi, ci: (bi, ci, gi)),
            pl.BlockSpec((1, Q_PER_KV, tq, 128), lambda bi, gi, ci: (gi, 0, 0, 0)),
            per_bg(n_cmp_pad, 128),
            per_bg(n_cmp_pad, HEAD_DIM),
            per_bg(t, 256),
            per_bg(t, 128),
            per_bg(t, 128),
            per_bg(t, 128),
            pl.BlockSpec((n_sel, n_cmp_pad), lambda bi, gi, ci: (0, 0)),
            pl.BlockSpec((tq, tk), lambda bi, gi, ci: (0, 0)),
            pl.BlockSpec((tq, n_cmp_pad), lambda bi, gi, ci: (0, 0)),
            pl.BlockSpec((1, tq, band), lambda bi, gi, ci: (jnp.minimum(ci, WINDOW // tq), 0, 0)),
        ],
        out_specs=pl.BlockSpec((1, tq, KV_WIDTH), lambda bi, gi, ci: (bi, ci, gi)),
        out_shape=jax.ShapeDtypeStruct((b, t, N_HEADS * HEAD_DIM), F32),
        scratch_shapes=[
            pltpu.VMEM((rows, 256), BF16),
            pltpu.VMEM((n_sel, tq), F32),
            pltpu.VMEM((rows, 1), F32),
            pltpu.VMEM((rows, 1), F32),
            pltpu.VMEM((rows, 1), F32),
            pltpu.VMEM((rows, tk), F32),
            pltpu.VMEM((rows, tk), F32),
            pltpu.VMEM((rows, tk), BF16),
            pltpu.VMEM((rows, tk), BF16),
            pltpu.VMEM((rows, 128), F32),
        ],
        compiler_params=_params(("parallel", "parallel", "arbitrary"), 48),
        name="attn_prompt",
    )(q, gate_pre, slope_tab, kc_aug, vc, ks_aug, vs, kw_aug, vw, cmat_t, dmat, dcmp, wbias)


def _softmax_lanes(s):
    e = jnp.exp(s - jnp.max(s, axis=-1, keepdims=True))
    return e / jnp.sum(e, axis=-1, keepdims=True)


def _attn_sample_kernel(pt_ref, *refs, n_pages, past_len, n_sel):
    del pt_ref
    q_ref, gt_ref, sl_ref, kc_ref, vc_ref = refs[:5]
    page_refs = refs[5:5 + n_pages]
    new_ref, win_ref, ex_ref, cm_ref, rm_ref, o_ref, wn_ref = refs[5 + n_pages:]
    q = q_ref[0]
    sl = sl_ref[:, 0:1]
    new = new_ref[0]

    n_cmp_pad = kc_ref.shape[1]
    dist_c = (past_len - (CMP_BLOCK - 1)
              - CMP_STRIDE * lax.broadcasted_iota(jnp.int32, (N_HEADS, n_cmp_pad), 1)).astype(F32)
    s = _dot_nt(q, kc_ref[0]) - sl * dist_c
    mask = dist_c >= 0
    s = jnp.where(mask, s, NEG)
    e = jnp.where(mask, jnp.exp(s - jnp.max(s, axis=-1, keepdims=True)), 0.0)
    p_c = e / jnp.maximum(jnp.sum(e, axis=-1, keepdims=True), 1e-30)
    o_c = _dot(p_c.astype(BF16), vc_ref[0])

    hi, lo = _split_bf16(p_c)
    imp = _dot(hi, cm_ref[...]) + _dot(lo, cm_ref[...])
    hi, lo = _split_bf16(imp)
    imp = _dot(rm_ref[...], hi) + _dot(rm_ref[...], lo)
    n_blk = imp.shape[1]
    blk = lax.broadcasted_iota(jnp.int32, (N_HEADS, n_blk), 1)
    cur = past_len // SEL_BLOCK
    forced = (blk == 0) | (blk == cur) | (blk == cur - 1)
    valid = blk * SEL_BLOCK <= past_len
    pri = jnp.where(valid, jnp.where(forced, FORCED_PRIORITY, imp), -1.0)
    pri = jnp.where(blk < n_sel, pri, -2.0)
    rank = jnp.zeros((N_HEADS, n_blk), jnp.int32)
    for s2 in range(n_sel):
        col = pri[:, s2:s2 + 1]
        beats = (col > pri) | ((col == pri) & (blk > s2))
        rank = rank + beats.astype(jnp.int32)
    bias = jnp.where(rank < min(N_SELECT, n_sel), 0.0, NEG)

    k_sel = jnp.concatenate([jnp.concatenate([r[0, 0], r[0, 1]], axis=1) for r in page_refs],
                            axis=0).astype(BF16)
    v_sel = jnp.concatenate([jnp.concatenate([r[0, 2], r[0, 3]], axis=1) for r in page_refs],
                            axis=0).astype(BF16)
    dist_s = (past_len - lax.broadcasted_iota(jnp.int32, (N_HEADS, past_len), 1)).astype(F32)
    s = _dot_nt(q, k_sel) - sl * dist_s + _dot(bias.astype(BF16), ex_ref[...])
    qf = q.astype(F32)
    k_new = new[:, 2 * KV_WIDTH:3 * KV_WIDTH].astype(BF16).astype(F32)
    v_new = new[:, 3 * KV_WIDTH:4 * KV_WIDTH].astype(BF16).astype(F32)
    s_new = jnp.sum(qf * k_new, axis=-1, keepdims=True) + bias[:, past_len // SEL_BLOCK:past_len // SEL_BLOCK + 1]
    m = jnp.maximum(jnp.max(s, axis=-1, keepdims=True), s_new)
    e = jnp.exp(s - m)
    e_new = jnp.exp(s_new - m)
    l = jnp.sum(e, axis=-1, keepdims=True) + e_new
    o_s = (_dot(e.astype(BF16), v_sel) + e_new.astype(BF16).astype(F32) * v_new) / l

    wb = win_ref.shape[1]
    rolled = pltpu.roll(win_ref[0], wb - 1, 0)
    rowi = lax.broadcasted_iota(jnp.int32, rolled.shape, 0)
    wn = jnp.where(rowi == wb - 1, new[:, 4 * KV_WIDTH:6 * KV_WIDTH], rolled)
    wn_ref[0] = wn
    dist_w = (wb - 1 - lax.broadcasted_iota(jnp.int32, (N_HEADS, wb), 1)).astype(F32)
    p_w = _softmax_lanes(_dot_nt(q, wn[:, 0:KV_WIDTH].astype(BF16)) - sl * dist_w)
    o_w = _dot(p_w.astype(BF16), wn[:, KV_WIDTH:2 * KV_WIDTH].astype(BF16))

    gate = _sigmoid(gt_ref[0])
    o_ref[0] = gate[:, 0:1] * o_c + gate[:, 1:2] * o_s + gate[:, 2:3] * o_w


def _attn_sample(qbd, gate16, slope16, kc_all, vc_all, pages, page_table, new_rows, state_win, expand, cmat, rmat):
    nb, n_pages = page_table.shape
    page_rows = pages.shape[2]
    past_len = n_pages * page_rows
    n_sel = -(-(past_len + 1) // SEL_BLOCK)
    wb = state_win.shape[1]
    n_cmp_pad = kc_all.shape[1]
    kern = functools.partial(_attn_sample_kernel, n_pages=n_pages, past_len=past_len, n_sel=n_sel)
    per_b = lambda *shape: pl.BlockSpec((1,) + shape, lambda b, pt: (b,) + (0,) * len(shape))
    const = lambda *shape: pl.BlockSpec(shape, lambda b, pt: (0,) * len(shape))
    page_specs = [
        pl.BlockSpec((1, 4, page_rows, 128), functools.partial(lambda p, b, pt: (pt[b, p], 1, 0, 0), p))
        for p in range(n_pages)]
    return pl.pallas_call(
        kern,
        grid_spec=pltpu.PrefetchScalarGridSpec(
            num_scalar_prefetch=1,
            grid=(nb,),
            in_specs=[per_b(N_HEADS, KV_WIDTH), per_b(N_HEADS, 128), const(N_HEADS, 128),
                      per_b(n_cmp_pad, KV_WIDTH), per_b(n_cmp_pad, KV_WIDTH)]
            + page_specs
            + [per_b(1, 6 * KV_WIDTH), per_b(wb, 2 * KV_WIDTH), const(*expand.shape), const(*cmat.shape),
               const(*rmat.shape)],
            out_specs=[per_b(N_HEADS, KV_WIDTH), per_b(wb, 2 * KV_WIDTH)],
        ),
        out_shape=[jax.ShapeDtypeStruct((nb, N_HEADS, KV_WIDTH), F32),
                   jax.ShapeDtypeStruct((nb, wb, 2 * KV_WIDTH), F32)],
        compiler_params=_params(("parallel",), 48),
        name="attn_sample",
    )(page_table, qbd, gate16, slope16, kc_all, vc_all, *([pages] * n_pages), new_rows, state_win, expand, cmat,
      rmat)


def _alibi_slopes():
    return jnp.exp2(-8.0 * (jnp.arange(N_HEADS, dtype=F32) + 1.0) / N_HEADS)


def _cmp_to_sel(n_cmp_pad, n_cmp, n_sel_pad):
    m = np.zeros((n_cmp_pad, n_sel_pad), np.float32)
    for n in range(n_cmp):
        for k in range(CMP_BLOCK // CMP_STRIDE):
            m[n, (n + k) * CMP_STRIDE // SEL_BLOCK] += 1.0
    return m


def _pos_cols(pos, width):
    cols = np.zeros((pos.shape[0], width), np.float32)
    cols[:, 0] = cols[:, 1] = pos % 64
    cols[:, 2] = cols[:, 3] = pos // 64
    return cols


def kernel(x_prompt, x_sample, state_pool, cache_kv_pages, state_win, page_table, norm_mix, norm_mlp, w_up, w_down,
           pool_w, pool_scale, norm_kv, w_kv, cmp_pe, cmp_w1, cmp_w2, w_qg, b_gate, w_o, norm_final):
    b, t, d = x_prompt.shape
    nb = x_sample.shape[0]
    n_phys, page_rows = cache_kv_pages.shape[:2]
    n_q = N_HEADS * HEAD_DIM
    row = lambda v: v.reshape(1, -1)

    wu = w_up.astype(BF16)
    wd = w_down.astype(BF16)
    pw = pool_w[0].astype(BF16)
    wkv = w_kv.astype(BF16)
    wo = w_o[0].astype(BF16)
    hh = np.arange(N_HEADS)
    gcols = ((hh // Q_PER_KV) * 128 + (hh % Q_PER_KV) * N_BRANCH)[:, None] + np.arange(N_BRANCH)[None, :]
    gcols = gcols.reshape(-1)
    wg = jnp.zeros((d, N_KV_HEADS * 128), F32).at[:, gcols].set(w_qg[0][:, n_q:])
    wqg = jnp.concatenate([w_qg[0][:, :n_q], wg], axis=1).astype(BF16)
    bqg = jnp.zeros((1, n_q + N_KV_HEADS * 128), F32).at[0, n_q + gcols].set(b_gate[0])
    w1_bf = cmp_w1.astype(BF16)
    w1r = w1_bf.reshape(2, 2, CMP_STRIDE, HEAD_DIM, CMP_HIDDEN)
    w1c = jnp.concatenate([w1r[:, 0], w1r[:, 1]], axis=-1)
    w1d = jnp.concatenate([jnp.concatenate([w1c, jnp.zeros_like(w1c)], axis=-1),
                           jnp.concatenate([jnp.zeros_like(w1c), w1c], axis=-1)], axis=2)
    w1d = w1d.reshape(2, CMP_STRIDE // 2, 4 * HEAD_DIM, 4 * CMP_HIDDEN)
    pe8 = jnp.broadcast_to(cmp_pe.reshape(2, 1, CMP_BLOCK * HEAD_DIM), (2, 8, CMP_BLOCK * HEAD_DIM))
    w2_bf = cmp_w2.astype(BF16)

    slopes = _alibi_slopes()
    s_hi = slopes.astype(BF16).astype(F32)
    s_lo = (slopes - s_hi).astype(BF16).astype(F32)
    slope_cols = jnp.stack([s_hi, s_lo, 64.0 * s_hi, 64.0 * s_lo], axis=-1)

    x1, u_last = _pool_prompt(x_prompt, row(norm_mix[0]), pw, row(pool_scale[0]))
    pool_p = u_last[:, None, 16 - POOL_BUF:, :]
    x2 = _mlp(x1.reshape(b * t, d), row(norm_mlp[0]), wu[0], wd[0], row(norm_final), final_norm=False, tm=1024)
    tq = 128
    n_sub_p = t // CMP_STRIDE
    n_sel_p = t // SEL_BLOCK
    tpos = np.arange(t)
    pc_t = jnp.asarray(np.concatenate([np.zeros((t, HEAD_DIM), np.float32), _pos_cols(tpos, HEAD_DIM)], axis=1))
    onehot_t = jnp.asarray(np.eye(n_sel_p, 128, dtype=np.float32)[tpos // SEL_BLOCK], BF16)
    kv_rows, kv_win, q_p, gate_p, ks_aug, vs, kw_aug, vw = _proj(
        x2, row(norm_kv), row(norm_mix[1]), wkv, wqg, bqg, pc_t, onehot_t, b=b, t=t)
    kv_rows_p = kv_rows.reshape(b, t, 2, 2, N_KV_HEADS, HEAD_DIM)
    win_new_p = kv_win.reshape(b, t, 2, N_KV_HEADS, HEAD_DIM)[:, -min(WINDOW, t):]

    dummy_pt = jnp.zeros((b, 1), jnp.int32)
    kc_p, vc_p = _compress(kv_rows.reshape(b, t, ROW_WIDTH), dummy_pt, w1d, w1_bf, pe8, w2_bf, paged=False)

    bcast = lambda a: jnp.broadcast_to(a[None, None], (b, N_KV_HEADS) + a.shape)
    pc_c = jnp.asarray(_pos_cols(CMP_STRIDE * np.arange(n_sub_p) + CMP_BLOCK - 1, HEAD_DIM), BF16)
    kc_aug = jnp.concatenate([kc_p.astype(BF16), bcast(pc_c)], axis=-1)
    cmat_t = jnp.asarray(_cmp_to_sel(n_sub_p, n_sub_p - 1, n_sel_p).T, BF16)
    slope_tab = jnp.zeros((N_KV_HEADS, Q_PER_KV, tq, 128), F32).at[:, :, :, HEAD_DIM:HEAD_DIM + 4].set(
        jnp.broadcast_to(slope_cols.reshape(N_KV_HEADS, Q_PER_KV, 1, 4), (N_KV_HEADS, Q_PER_KV, tq, 4)))
    o_p = _attn_prompt(q_p.reshape(b, t, n_q), gate_p.reshape(b, t, N_KV_HEADS * 128), slope_tab, kc_aug,
                       vc_p.astype(BF16), ks_aug, vs, kw_aug, vw, cmat_t, tq=tq)
    x3, = _lin(o_p.reshape(b * t, n_q), wo, res=x2, tm=512)
    y_prompt = _mlp(x3, row(norm_mlp[1]), wu[1], wd[1], row(norm_final), final_norm=True, tm=1024).reshape(b, t, d)

    xs0 = x_sample.reshape(nb, d)
    xs1, u_s = _pool_sample(xs0, state_pool[:, 0], row(norm_mix[0]), pw, row(pool_scale[0]))
    pool_s = jnp.concatenate([state_pool[:, 0, 1:], u_s[:, None]], axis=1)[:, None]
    xs2 = _mlp(xs1, row(norm_mlp[0]), wu[0], wd[0], row(norm_final), final_norm=False, tm=nb)
    kv_s, = _lin(xs2, wkv, g=row(norm_kv), tm=nb)
    kv_rows_s = kv_s[:, :ROW_WIDTH].reshape(nb, 1, 2, 2, N_KV_HEADS, HEAD_DIM)

    pages = cache_kv_pages.reshape(n_phys, page_rows, ROW_WIDTH // 128, 128).transpose(0, 2, 1, 3)
    kc_s, vc_s = _compress(pages, page_table, w1d, w1_bf, pe8, w2_bf, paged=True)
    n_sub_s = kc_s.shape[2]
    all_heads = lambda a: a.transpose(0, 2, 1, 3).reshape(nb, n_sub_s, KV_WIDTH).astype(BF16)

    q_s, gate_s = _lin(xs2, wqg, g=row(norm_mix[1]), b=bqg, splits=(n_q, N_KV_HEADS * 128), tm=nb)
    q4 = q_s.reshape(nb, N_KV_HEADS, Q_PER_KV, HEAD_DIM) * (HEAD_DIM ** -0.5)
    qbd = jnp.einsum('bgrd,gh->bgrhd', q4, jnp.eye(N_KV_HEADS, dtype=F32)).reshape(nb, N_HEADS, KV_WIDTH)
    g16 = gate_s.reshape(nb, N_KV_HEADS, 128)[:, :, :Q_PER_KV * N_BRANCH].reshape(nb, N_HEADS, N_BRANCH)
    g16 = jnp.pad(g16, ((0, 0), (0, 0), (0, 128 - N_BRANCH)))
    slope16 = jnp.broadcast_to(slopes[:, None], (N_HEADS, 128))
    past_len = page_table.shape[1] * page_rows
    n_blk_pad = 64
    expand = jnp.asarray(np.eye(n_blk_pad, dtype=np.float32)[:, np.arange(past_len) // SEL_BLOCK], BF16)
    cmat_s = jnp.asarray(_cmp_to_sel(n_sub_s, n_sub_s - 1, n_blk_pad), BF16)
    rmat = jnp.asarray(np.kron(np.eye(N_KV_HEADS), np.ones((Q_PER_KV, Q_PER_KV))), BF16)
    o_s, win_new_s = _attn_sample(
        qbd.astype(BF16), g16, slope16, all_heads(kc_s), all_heads(vc_s), pages, page_table,
        kv_s.reshape(nb, 1, 6 * KV_WIDTH), state_win.reshape(nb, state_win.shape[1], 2 * KV_WIDTH),
        expand, cmat_s, rmat)
    o5 = o_s.reshape(nb, N_KV_HEADS, Q_PER_KV, N_KV_HEADS, HEAD_DIM)
    o_heads = jnp.stack([o5[:, gh, :, gh] for gh in range(N_KV_HEADS)], axis=1).reshape(nb, n_q)
    xs3, = _lin(o_heads, wo, res=xs2, tm=nb)
    y_sample = _mlp(xs3, row(norm_mlp[1]), wu[1], wd[1], row(norm_final), final_norm=True, tm=nb).reshape(nb, 1, d)

    return (y_prompt, y_sample, pool_p, pool_s, kv_rows_p, kv_rows_s, win_new_p,
            win_new_s.reshape(state_win.shape))
```

```python
import functools

import numpy as np
import jax
import jax.numpy as jnp
from jax import lax
from jax.experimental import pallas as pl
from jax.experimental.pallas import tpu as pltpu

D_MODEL = 1024
POOL_WINDOWS = (2, 4, 8, 16)
POOL_GROUP_DIM = D_MODEL // len(POOL_WINDOWS)
POOL_BUF = max(POOL_WINDOWS) - 1
N_HEADS = 16
HEAD_DIM = 64
N_KV_HEADS = 4
Q_PER_KV = N_HEADS // N_KV_HEADS
N_BRANCH = 3
CMP_BLOCK = 32
CMP_STRIDE = 16
CMP_HIDDEN = 2 * HEAD_DIM
SEL_BLOCK = 64
N_SELECT = 16
WINDOW = 512
RMS_EPS = 1e-6
FORCED_PRIORITY = 1e6
KV_WIDTH = N_KV_HEADS * HEAD_DIM
ROW_WIDTH = 2 * 2 * KV_WIDTH

NEG = -1e30
F32 = jnp.float32
BF16 = jnp.bfloat16
MIB = 1024 * 1024


def _params(semantics, vmem_mib):
    return pltpu.CompilerParams(dimension_semantics=semantics, vmem_limit_bytes=vmem_mib * MIB)


def _rms(x, g):
    return x * lax.rsqrt(jnp.mean(x * x, axis=-1, keepdims=True) + RMS_EPS) * g


def _dot(a, b):
    return jnp.dot(a, b, preferred_element_type=F32)


def _dot_nt(a, b):
    return lax.dot_general(a, b, (((1,), (1,)), ((), ())), preferred_element_type=F32)


def _split_bf16(x):
    hi = x.astype(BF16)
    lo = (x - hi.astype(F32)).astype(BF16)
    return hi, lo


def _sigmoid(x):
    return 1.0 / (1.0 + jnp.exp(-x))


def _pool_prompt_kernel(x_ref, xp_ref, g_ref, w_ref, sc_ref, y_ref, ul_ref, u_scr, *, tt):
    i = pl.program_id(1)
    g = g_ref[...]
    x = x_ref[0]
    u = _rms(x, g)
    up = _rms(xp_ref[0], g) * (i > 0).astype(F32)
    u_scr[0:16, :] = up
    u_scr[16:16 + tt, :] = u
    t = i * tt + lax.broadcasted_iota(jnp.int32, (tt, 1), 0)
    for gi, win in enumerate(POOL_WINDOWS):
        cols = slice(gi * POOL_GROUP_DIM, (gi + 1) * POOL_GROUP_DIM)
        acc = u[:, cols]
        for k in range(1, win):
            acc = acc + u_scr[16 - k:16 - k + tt, cols]
        cnt = jnp.minimum(t + 1, win).astype(F32)
        diff = acc / cnt - u[:, cols]
        yg = _dot(diff.astype(BF16), w_ref[gi])
        y_ref[0, :, cols] = x[:, cols] + yg * sc_ref[:, cols]
    ul_ref[0] = u[tt - 16:, :]


def _pool_prompt(x, g, w_bf, scale, *, tt=256):
    b, t, d = x.shape
    n_t = t // tt
    kern = functools.partial(_pool_prompt_kernel, tt=tt)
    return pl.pallas_call(
        kern,
        grid=(b, n_t),
        in_specs=[
            pl.BlockSpec((1, tt, d), lambda bi, i: (bi, i, 0)),
            pl.BlockSpec((1, 16, d), lambda bi, i: (bi, jnp.maximum(i * (tt // 16) - 1, 0), 0)),
            pl.BlockSpec((1, d), lambda bi, i: (0, 0)),
            pl.BlockSpec((len(POOL_WINDOWS), POOL_GROUP_DIM, POOL_GROUP_DIM), lambda bi, i: (0, 0, 0)),
            pl.BlockSpec((1, d), lambda bi, i: (0, 0)),
        ],
        out_specs=[
            pl.BlockSpec((1, tt, d), lambda bi, i: (bi, i, 0)),
            pl.BlockSpec((1, 16, d), lambda bi, i: (bi, 0, 0)),
        ],
        out_shape=[jax.ShapeDtypeStruct((b, t, d), F32), jax.ShapeDtypeStruct((b, 16, d), F32)],
        scratch_shapes=[pltpu.VMEM((16 + tt, d), F32)],
        compiler_params=_params(("parallel", "arbitrary"), 32),
        name="pool_prompt",
    )(x, x, g, w_bf, scale)


def _pool_sample_kernel(x_ref, past_ref, g_ref, w_ref, sc_ref, y_ref, u_ref):
    x = x_ref[...]
    u = _rms(x, g_ref[...])
    for gi, win in enumerate(POOL_WINDOWS):
        cols = slice(gi * POOL_GROUP_DIM, (gi + 1) * POOL_GROUP_DIM)
        acc = u[:, cols]
        for k in range(1, win):
            acc = acc + past_ref[:, POOL_BUF - k, cols]
        diff = acc / float(win) - u[:, cols]
        yg = _dot(diff.astype(BF16), w_ref[gi])
        y_ref[:, cols] = x[:, cols] + yg * sc_ref[:, cols]
    u_ref[...] = u


def _pool_sample(x, past, g, w_bf, scale, *, bt=32):
    nb, d = x.shape
    return pl.pallas_call(
        _pool_sample_kernel,
        grid=(nb // bt,),
        in_specs=[
            pl.BlockSpec((bt, d), lambda i: (i, 0)),
            pl.BlockSpec((bt, POOL_BUF, d), lambda i: (i, 0, 0)),
            pl.BlockSpec((1, d), lambda i: (0, 0)),
            pl.BlockSpec((len(POOL_WINDOWS), POOL_GROUP_DIM, POOL_GROUP_DIM), lambda i: (0, 0, 0)),
            pl.BlockSpec((1, d), lambda i: (0, 0)),
        ],
        out_specs=[pl.BlockSpec((bt, d), lambda i: (i, 0)), pl.BlockSpec((bt, d), lambda i: (i, 0))],
        out_shape=[jax.ShapeDtypeStruct((nb, d), F32), jax.ShapeDtypeStruct((nb, d), F32)],
        compiler_params=_params(("parallel",), 32),
        name="pool_sample",
    )(x, past, g, w_bf, scale)


def _mlp_kernel(x_ref, g_ref, wu_ref, wd_ref, gf_ref, o_ref, xn_scr, acc_scr, *, final_norm):
    j = pl.program_id(1)

    @pl.when(j == 0)
    def _():
        xn_scr[...] = _rms(x_ref[...], g_ref[...]).astype(BF16)
        acc_scr[...] = jnp.zeros_like(acc_scr)

    h = jnp.maximum(_dot(xn_scr[...], wu_ref[...]), 0.0)
    acc_scr[...] += _dot((h * h).astype(BF16), wd_ref[...])

    @pl.when(j == pl.num_programs(1) - 1)
    def _():
        r = x_ref[...] + acc_scr[...]
        o_ref[...] = _rms(r, gf_ref[...]) if final_norm else r


def _mlp(x, g, wu_bf, wd_bf, gf, *, final_norm, tm, tf=1024):
    m, d = x.shape
    f = wu_bf.shape[1]
    kern = functools.partial(_mlp_kernel, final_norm=final_norm)
    return pl.pallas_call(
        kern,
        grid=(m // tm, f // tf),
        in_specs=[
            pl.BlockSpec((tm, d), lambda i, j: (i, 0)),
            pl.BlockSpec((1, d), lambda i, j: (0, 0)),
            pl.BlockSpec((d, tf), lambda i, j: (0, j)),
            pl.BlockSpec((tf, d), lambda i, j: (j, 0)),
            pl.BlockSpec((1, d), lambda i, j: (0, 0)),
        ],
        out_specs=pl.BlockSpec((tm, d), lambda i, j: (i, 0)),
        out_shape=jax.ShapeDtypeStruct((m, d), F32),
        scratch_shapes=[pltpu.VMEM((tm, d), BF16), pltpu.VMEM((tm, d), F32)],
        compiler_params=_params(("parallel", "arbitrary"), 48),
        name="mlp",
    )(x, g, wu_bf, wd_bf, gf)


def _proj_kernel(x_ref, gkv_ref, gq_ref, wkv_ref, wqg_ref, bqg_ref, pc_ref, oh_ref,
                 rows_ref, win_ref, q_ref, gate_ref, ks_ref, vs_ref, kw_ref, vw_ref):
    x = x_ref[...]
    xh = x * lax.rsqrt(jnp.mean(x * x, axis=-1, keepdims=True) + RMS_EPS)
    hkv = _dot((xh * gkv_ref[...]).astype(BF16), wkv_ref[...])
    hq = _dot((xh * gq_ref[...]).astype(BF16), wqg_ref[...]) + bqg_ref[...]
    n_q = N_HEADS * HEAD_DIM
    rows_ref[...] = hkv[:, :ROW_WIDTH]
    win_ref[...] = hkv[:, ROW_WIDTH:]
    q_ref[...] = hq[:, :n_q]
    gate_ref[...] = hq[:, n_q:]

    lane = lax.broadcasted_iota(jnp.int32, (x.shape[0], 128), 1)
    pc = pc_ref[...]
    ones = (lane == HEAD_DIM).astype(F32)

    def head(col0, gh, filler):
        pair = hkv[:, col0 + (gh // 2) * 128:col0 + (gh // 2 + 1) * 128]
        if gh % 2:
            pair = pltpu.roll(pair, HEAD_DIM, 1)
        return jnp.where(lane < HEAD_DIM, pair, filler).astype(BF16)

    for gh in range(N_KV_HEADS):
        ks_ref[0, gh, :, 0:128] = head(2 * KV_WIDTH, gh, pc)
        ks_ref[0, gh, :, 128:256] = oh_ref[...]
        vs_ref[0, gh] = head(3 * KV_WIDTH, gh, ones)
        kw_ref[0, gh] = head(4 * KV_WIDTH, gh, pc)
        vw_ref[0, gh] = head(5 * KV_WIDTH, gh, ones)


def _proj(x, g_kv, g_q, wkv_bf, wqg_bf, bqg, pos_cols, onehot, *, b, t, tm=512):
    m, d = x.shape
    nt = t // tm
    n_kv = wkv_bf.shape[1]
    n_qg = wqg_bf.shape[1]
    n_q = N_HEADS * HEAD_DIM
    const = lambda *shape: pl.BlockSpec(shape, lambda i: (0,) * len(shape))
    rows_of = lambda n: pl.BlockSpec((tm, n), lambda i: (i, 0))
    heads_of = lambda n: pl.BlockSpec((1, N_KV_HEADS, tm, n), lambda i: (i // nt, 0, i % nt, 0))
    heads_shape = lambda n: jax.ShapeDtypeStruct((b, N_KV_HEADS, t, n), BF16)
    return pl.pallas_call(
        _proj_kernel,
        grid=(m // tm,),
        in_specs=[rows_of(d), const(1, d), const(1, d), const(d, n_kv), const(d, n_qg), const(1, n_qg),
                  pl.BlockSpec((tm, 128), lambda i: (i % nt, 0)), pl.BlockSpec((tm, 128), lambda i: (i % nt, 0))],
        out_specs=[rows_of(ROW_WIDTH), rows_of(n_kv - ROW_WIDTH), rows_of(n_q), rows_of(n_qg - n_q),
                   heads_of(256), heads_of(128), heads_of(128), heads_of(128)],
        out_shape=[jax.ShapeDtypeStruct((m, ROW_WIDTH), F32), jax.ShapeDtypeStruct((m, n_kv - ROW_WIDTH), F32),
                   jax.ShapeDtypeStruct((m, n_q), F32), jax.ShapeDtypeStruct((m, n_qg - n_q), F32),
                   heads_shape(256), heads_shape(128), heads_shape(128), heads_shape(128)],
        compiler_params=_params(("parallel",), 56),
        name="proj",
    )(x, g_kv, g_q, wkv_bf, wqg_bf, bqg, pos_cols, onehot)


def _lin_kernel(*refs, norm, bias, residual, splits):
    refs = list(refs)
    x_ref = refs.pop(0)
    g_ref = refs.pop(0) if norm else None
    w_ref = refs.pop(0)
    b_ref = refs.pop(0) if bias else None
    r_ref = refs.pop(0) if residual else None
    x = x_ref[...]
    if norm:
        x = _rms(x, g_ref[...])
    h = _dot(x.astype(BF16), w_ref[...])
    if bias:
        h = h + b_ref[...]
    if residual:
        h = h + r_ref[...]
    c0 = 0
    for o_ref, n in zip(refs, splits):
        o_ref[...] = h[:, c0:c0 + n]
        c0 += n


def _lin(x, w_bf, *, g=None, b=None, res=None, splits=None, tm):
    m, k = x.shape
    n = w_bf.shape[1]
    splits = splits or (n,)
    kern = functools.partial(_lin_kernel, norm=g is not None, bias=b is not None,
                             residual=res is not None, splits=splits)
    args = [x]
    in_specs = [pl.BlockSpec((tm, k), lambda i: (i, 0))]
    if g is not None:
        args.append(g)
        in_specs.append(pl.BlockSpec((1, k), lambda i: (0, 0)))
    args.append(w_bf)
    in_specs.append(pl.BlockSpec((k, n), lambda i: (0, 0)))
    if b is not None:
        args.append(b)
        in_specs.append(pl.BlockSpec((1, n), lambda i: (0, 0)))
    if res is not None:
        args.append(res)
        in_specs.append(pl.BlockSpec((tm, n), lambda i: (i, 0)))
    outs = pl.pallas_call(
        kern,
        grid=(m // tm,),
        in_specs=in_specs,
        out_specs=[pl.BlockSpec((tm, s), lambda i: (i, 0)) for s in splits],
        out_shape=[jax.ShapeDtypeStruct((m, s), F32) for s in splits],
        compiler_params=_params(("parallel",), 48),
        name="lin",
    )(*args)
    return outs


def _compress_kernel(pt_ref, *refs, n_page_refs, paged, rows_per_ref, n_sub):
    del pt_ref
    page_refs = refs[:n_page_refs]
    w1d_ref, w1_ref, pe_ref, w2_ref, kc_ref, vc_ref = refs[n_page_refs:]
    n_take = rows_per_ref // CMP_STRIDE

    def tap_rows(c, j):
        take = pl.ds(j, n_take, stride=CMP_STRIDE)
        if paged:
            return jnp.concatenate([r[0, c, take, :] for r in page_refs], axis=0)
        return page_refs[c][0, take, :]

    for kv, o_ref in enumerate((kc_ref, vc_ref)):
        pew = _dot(pe_ref[kv].astype(BF16), w1_ref[kv])[0:1]
        for c2 in range(2):
            acc = None
            for jp in range(CMP_STRIDE // 2):
                lhs = jnp.concatenate([tap_rows(2 * kv + c2, 2 * jp), tap_rows(2 * kv + c2, 2 * jp + 1)], axis=1)
                d = _dot(lhs.astype(BF16), w1d_ref[kv, jp])
                acc = d if acc is None else acc + d
            for e in range(2):
                a = acc[:, 2 * e * CMP_HIDDEN:(2 * e + 1) * CMP_HIDDEN]
                b2 = acc[:, (2 * e + 1) * CMP_HIDDEN:(2 * e + 2) * CMP_HIDDEN]
                hid = a + pltpu.roll(b2, n_sub - 1, 0) + pew
                act = hid * _sigmoid(hid)
                o_ref[0, 2 * c2 + e] = _dot(act.astype(BF16), w2_ref[kv])


def _compress(pages, page_table, w1d_bf, w1_bf, pe8, w2_bf, *, paged):
    if paged:
        nb, n_pages = page_table.shape
        rows_per_ref = pages.shape[2]
        n_sub = n_pages * rows_per_ref // CMP_STRIDE
        page_specs = [
            pl.BlockSpec((1, 4, rows_per_ref, 128), functools.partial(lambda p, b, pt: (pt[b, p], 0, 0, 0), p))
            for p in range(n_pages)]
    else:
        nb, rows_per_ref, _ = pages.shape
        n_sub = rows_per_ref // CMP_STRIDE
        page_specs = [pl.BlockSpec((1, rows_per_ref, 128), functools.partial(lambda c, b, pt: (b, 0, c), c))
                      for c in range(4)]
    page_args = [pages] * len(page_specs)
    kern = functools.partial(_compress_kernel, n_page_refs=len(page_args), paged=paged, rows_per_ref=rows_per_ref,
                             n_sub=n_sub)
    const = lambda *shape: pl.BlockSpec(shape, lambda b, pt: (0,) * len(shape))
    out_spec = pl.BlockSpec((1, N_KV_HEADS, n_sub, HEAD_DIM), lambda b, pt: (b, 0, 0, 0))
    return pl.pallas_call(
        kern,
        grid_spec=pltpu.PrefetchScalarGridSpec(
            num_scalar_prefetch=1,
            grid=(nb,),
            in_specs=page_specs + [
                const(2, CMP_STRIDE // 2, 4 * HEAD_DIM, 4 * CMP_HIDDEN),
                const(2, CMP_BLOCK * HEAD_DIM, CMP_HIDDEN),
                const(2, 8, CMP_BLOCK * HEAD_DIM),
                const(2, CMP_HIDDEN, HEAD_DIM),
            ],
            out_specs=[out_spec, out_spec],
        ),
        out_shape=[jax.ShapeDtypeStruct((nb, N_KV_HEADS, n_sub, HEAD_DIM), F32)] * 2,
        compiler_params=_params(("parallel",), 48),
        name="compress",
    )(page_table, *page_args, w1d_bf, w1_bf, pe8, w2_bf)


def _attn_prompt_kernel(q_ref, gt_ref, st_ref, kc_ref, vc_ref, ks_ref, vs_ref, kw_ref, vw_ref, cm_ref, dm_ref,
                        dc_ref, wb_ref, o_ref, qa_scr, pri_scr, m_scr, ala_scr, alb_scr, sa_scr, sb_scr, pa_scr,
                        pb_scr, acc_scr, *, tq, tk, rc, n_sel):
    c0 = pl.program_id(2) * tq
    rows = Q_PER_KV * tq
    band = WINDOW + tq

    lane = lax.broadcasted_iota(jnp.int32, (tq, 128), 1)
    qb = q_ref[0] * (HEAD_DIM ** -0.5)
    for r in range(Q_PER_KV):
        pair = qb[:, (r // 2) * 128:(r // 2 + 1) * 128]
        if r % 2:
            pair = pltpu.roll(pair, HEAD_DIM, 1)
        qa_scr[r * tq:(r + 1) * tq, 0:128] = jnp.where(lane < HEAD_DIM, pair, st_ref[0, r]).astype(BF16)
    ql = qa_scr[:, 0:128]

    n_cmp_pad = kc_ref.shape[2]
    s = _dot_nt(ql, kc_ref[0, 0])
    mask = jnp.concatenate([dc_ref[...]] * Q_PER_KV, axis=0) <= c0
    s = jnp.where(mask, s, NEG)
    m = jnp.max(s, axis=-1, keepdims=True)
    e = jnp.where(mask, jnp.exp(s - m), 0.0)
    p = e / jnp.maximum(jnp.sum(e, axis=-1, keepdims=True), 1e-30)
    o_c = _dot(p.astype(BF16), vc_ref[0, 0])

    psum = p[0:tq] + p[tq:2 * tq] + p[2 * tq:3 * tq] + p[3 * tq:4 * tq]
    hi, lo = _split_bf16(psum)
    imp = _dot_nt(cm_ref[...], hi) + _dot_nt(cm_ref[...], lo)
    blk = lax.broadcasted_iota(jnp.int32, (n_sel, tq), 0)
    tcol = c0 + lax.broadcasted_iota(jnp.int32, (n_sel, tq), 1)
    cur = tcol >> 6
    forced = (blk == 0) | (blk == cur) | (blk == cur - 1)
    valid = blk * SEL_BLOCK <= tcol
    pri = jnp.where(valid, jnp.where(forced, FORCED_PRIORITY, imp), -1.0)
    pri_scr[...] = pri
    n_valid = c0 // SEL_BLOCK + tq // SEL_BLOCK
    n_walk = jnp.where(n_valid > N_SELECT, (n_valid + 7) // 8, 0)

    def walk(g8, rank):
        grp = pri_scr[pl.ds(pl.multiple_of(g8 * 8, 8), 8), :]
        for i in range(8):
            row = grp[i:i + 1, :]
            beats = (row > pri) | ((row == pri) & (blk > g8 * 8 + i))
            rank = rank + beats.astype(jnp.int32)
        return rank

    rank = lax.fori_loop(0, n_walk, walk, jnp.zeros((n_sel, tq), jnp.int32))
    bias = jnp.where(rank < min(N_SELECT, n_sel), 0.0, NEG).T
    right = jnp.concatenate([bias, jnp.zeros((tq, 128 - n_sel), F32)], axis=1).astype(BF16)
    for r in range(Q_PER_KV):
        qa_scr[r * tq:(r + 1) * tq, 128:256] = right

    m_scr[...] = jnp.full_like(m_scr, NEG)
    acc_scr[...] = jnp.zeros_like(acc_scr)

    def scores(j):
        return _dot_nt(qa_scr[...], ks_ref[0, 0, pl.ds(pl.multiple_of(j * tk, tk), tk), :])

    def soft_pv(s_scr, p_scr, al_scr, j, masked):
        k0 = pl.multiple_of(j * tk, tk)
        for i in range(rows // rc):
            rs = slice(i * rc, (i + 1) * rc)
            qs = (i * rc) % tq
            sc = s_scr[rs, :]
            if masked:
                sc = jnp.where(dm_ref[qs:qs + rc, :] <= c0 - k0, sc, NEG)
            m_old = m_scr[rs]
            m_new = jnp.maximum(m_old, jnp.max(sc, axis=-1, keepdims=True))
            al_scr[rs] = jnp.exp(m_old - m_new)
            p_scr[rs] = jnp.exp(sc - m_new).astype(BF16)
            m_scr[rs] = m_new
        acc_scr[...] = al_scr[...] * acc_scr[...] + _dot(p_scr[...], vs_ref[0, 0, pl.ds(k0, tk), :])

    n_tiles = c0 // tk + 1
    n_loop = (n_tiles - 1) // 2
    sa_scr[...] = scores(0)

    def tile_pair(jj, carry):
        sb_scr[...] = scores(2 * jj + 1)
        soft_pv(sa_scr, pa_scr, ala_scr, 2 * jj, False)
        sa_scr[...] = scores(2 * jj + 2)
        soft_pv(sb_scr, pb_scr, alb_scr, 2 * jj + 1, False)
        return carry

    lax.fori_loop(0, n_loop, tile_pair, 0)
    sb_scr[...] = scores(2 * n_loop + 1)
    soft_pv(sa_scr, pa_scr, ala_scr, 2 * n_loop, True)
    soft_pv(sb_scr, pb_scr, alb_scr, 2 * n_loop + 1, True)
    acc = acc_scr[...]
    o_s = acc[:, :HEAD_DIM] / acc[:, HEAD_DIM:HEAD_DIM + 1]

    w0 = pl.multiple_of(jnp.maximum(c0 - WINDOW, 0), 128)
    sw = _dot_nt(ql, kw_ref[0, 0, pl.ds(w0, band), :]) + jnp.concatenate([wb_ref[0]] * Q_PER_KV, axis=0)
    pw = jnp.exp(sw - jnp.max(sw, axis=-1, keepdims=True))
    aw = _dot(pw.astype(BF16), vw_ref[0, 0, pl.ds(w0, band), :])
    o_w = aw[:, :HEAD_DIM] / aw[:, HEAD_DIM:HEAD_DIM + 1]

    gate = _sigmoid(gt_ref[0])
    outs = []
    for r in range(Q_PER_KV):
        rs = slice(r * tq, (r + 1) * tq)
        outs.append(gate[:, 3 * r:3 * r + 1] * o_c[rs] + gate[:, 3 * r + 1:3 * r + 2] * o_s[rs]
                    + gate[:, 3 * r + 2:3 * r + 3] * o_w[rs])
    o_ref[0] = jnp.concatenate(outs, axis=1)


def _attn_prompt(q, gate_pre, slope_tab, kc_aug, vc, ks_aug, vs, kw_aug, vw, cmat_t, *, tq=128, tk=512, rc=64):
    b, t, _ = q.shape
    n_sel = t // SEL_BLOCK
    n_cmp_pad = kc_aug.shape[2]
    kern = functools.partial(_attn_prompt_kernel, tq=tq, tk=tk, rc=rc, n_sel=n_sel)
    rows = Q_PER_KV * tq
    band = WINDOW + tq
    qi = np.arange(tq)[:, None]
    dmat = jnp.asarray(np.arange(tk)[None, :] - qi, jnp.int32)
    dcmp = jnp.asarray(CMP_STRIDE * np.arange(n_cmp_pad)[None, :] + (CMP_BLOCK - 1) - qi, jnp.int32)
    dist = (np.minimum(np.arange(WINDOW // tq + 1) * tq, WINDOW)[:, None, None] + qi[None]
            - np.arange(band)[None, None, :])
    wbias = jnp.asarray(np.where((dist >= 0) & (dist < WINDOW), 0.0, NEG), F32)
    per_bg = lambda *shape: pl.BlockSpec((1, 1) + shape, lambda bi, gi, ci: (bi, gi, 0, 0))
    return pl.pallas_call(
        kern,
        grid=(b, N_KV_HEADS, t // tq),
        in_specs=[
            pl.BlockSpec((1, tq, KV_WIDTH), lambda bi, gi, ci: (bi, ci, gi)),
            pl.BlockSpec((1, tq, 128), lambda bi, gi, ci: (bi, ci, gi)),
            pl.BlockSpec((1, Q_PER_KV, tq, 128), lambda bi, gi, ci: (gi, 0, 0, 0)),
            per_bg(n_cmp_pad, 128),
            per_bg(n_cmp_pad, HEAD_DIM),
            per_bg(t, 256),
            per_bg(t, 128),
            per_bg(t, 128),
            per_bg(t, 128),
            pl.BlockSpec((n_sel, n_cmp_pad), lambda bi, gi, ci: (0, 0)),
            pl.BlockSpec((tq, tk), lambda bi, gi, ci: (0, 0)),
            pl.BlockSpec((tq, n_cmp_pad), lambda bi, gi, ci: (0, 0)),
            pl.BlockSpec((1, tq, band), lambda bi, gi, ci: (jnp.minimum(ci, WINDOW // tq), 0, 0)),
        ],
        out_specs=pl.BlockSpec((1, tq, KV_WIDTH), lambda bi, gi, ci: (bi, ci, gi)),
        out_shape=jax.ShapeDtypeStruct((b, t, N_HEADS * HEAD_DIM), F32),
        scratch_shapes=[
            pltpu.VMEM((rows, 256), BF16),
            pltpu.VMEM((n_sel, tq), F32),
            pltpu.VMEM((rows, 1), F32),
            pltpu.VMEM((rows, 1), F32),
            pltpu.VMEM((rows, 1), F32),
            pltpu.VMEM((rows, tk), F32),
            pltpu.VMEM((rows, tk), F32),
            pltpu.VMEM((rows, tk), BF16),
            pltpu.VMEM((rows, tk), BF16),
            pltpu.VMEM((rows, 128), F32),
        ],
        compiler_params=_params(("parallel", "parallel", "arbitrary"), 48),
        name="attn_prompt",
    )(q, gate_pre, slope_tab, kc_aug, vc, ks_aug, vs, kw_aug, vw, cmat_t, dmat, dcmp, wbias)


def _softmax_lanes(s):
    e = jnp.exp(s - jnp.max(s, axis=-1, keepdims=True))
    return e / jnp.sum(e, axis=-1, keepdims=True)


def _attn_sample_kernel(pt_ref, *refs, n_pages, past_len, n_sel):
    del pt_ref
    q_ref, gt_ref, sl_ref, kc_ref, vc_ref = refs[:5]
    page_refs = refs[5:5 + n_pages]
    new_ref, win_ref, ex_ref, cm_ref, rm_ref, o_ref, wn_ref = refs[5 + n_pages:]
    q = q_ref[0]
    sl = sl_ref[:, 0:1]
    new = new_ref[0]

    n_cmp_pad = kc_ref.shape[1]
    dist_c = (past_len - (CMP_BLOCK - 1)
              - CMP_STRIDE * lax.broadcasted_iota(jnp.int32, (N_HEADS, n_cmp_pad), 1)).astype(F32)
    s = _dot_nt(q, kc_ref[0]) - sl * dist_c
    mask = dist_c >= 0
    s = jnp.where(mask, s, NEG)
    e = jnp.where(mask, jnp.exp(s - jnp.max(s, axis=-1, keepdims=True)), 0.0)
    p_c = e / jnp.maximum(jnp.sum(e, axis=-1, keepdims=True), 1e-30)
    o_c = _dot(p_c.astype(BF16), vc_ref[0])

    hi, lo = _split_bf16(p_c)
    imp = _dot(hi, cm_ref[...]) + _dot(lo, cm_ref[...])
    hi, lo = _split_bf16(imp)
    imp = _dot(rm_ref[...], hi) + _dot(rm_ref[...], lo)
    n_blk = imp.shape[1]
    blk = lax.broadcasted_iota(jnp.int32, (N_HEADS, n_blk), 1)
    cur = past_len // SEL_BLOCK
    forced = (blk == 0) | (blk == cur) | (blk == cur - 1)
    valid = blk * SEL_BLOCK <= past_len
    pri = jnp.where(valid, jnp.where(forced, FORCED_PRIORITY, imp), -1.0)
    pri = jnp.where(blk < n_sel, pri, -2.0)
    rank = jnp.zeros((N_HEADS, n_blk), jnp.int32)
    for s2 in range(n_sel):
        col = pri[:, s2:s2 + 1]
        beats = (col > pri) | ((col == pri) & (blk > s2))
        rank = rank + beats.astype(jnp.int32)
    bias = jnp.where(rank < min(N_SELECT, n_sel), 0.0, NEG)

    k_sel = jnp.concatenate([jnp.concatenate([r[0, 0], r[0, 1]], axis=1) for r in page_refs],
                            axis=0).astype(BF16)
    v_sel = jnp.concatenate([jnp.concatenate([r[0, 2], r[0, 3]], axis=1) for r in page_refs],
                            axis=0).astype(BF16)
    dist_s = (past_len - lax.broadcasted_iota(jnp.int32, (N_HEADS, past_len), 1)).astype(F32)
    s = _dot_nt(q, k_sel) - sl * dist_s + _dot(bias.astype(BF16), ex_ref[...])
    qf = q.astype(F32)
    k_new = new[:, 2 * KV_WIDTH:3 * KV_WIDTH].astype(BF16).astype(F32)
    v_new = new[:, 3 * KV_WIDTH:4 * KV_WIDTH].astype(BF16).astype(F32)
    s_new = jnp.sum(qf * k_new, axis=-1, keepdims=True) + bias[:, past_len // SEL_BLOCK:past_len // SEL_BLOCK + 1]
    m = jnp.maximum(jnp.max(s, axis=-1, keepdims=True), s_new)
    e = jnp.exp(s - m)
    e_new = jnp.exp(s_new - m)
    l = jnp.sum(e, axis=-1, keepdims=True) + e_new
    o_s = (_dot(e.astype(BF16), v_sel) + e_new.astype(BF16).astype(F32) * v_new) / l

    wb = win_ref.shape[1]
    rolled = pltpu.roll(win_ref[0], wb - 1, 0)
    rowi = lax.broadcasted_iota(jnp.int32, rolled.shape, 0)
    wn = jnp.where(rowi == wb - 1, new[:, 4 * KV_WIDTH:6 * KV_WIDTH], rolled)
    wn_ref[0] = wn
    dist_w = (wb - 1 - lax.broadcasted_iota(jnp.int32, (N_HEADS, wb), 1)).astype(F32)
    p_w = _softmax_lanes(_dot_nt(q, wn[:, 0:KV_WIDTH].astype(BF16)) - sl * dist_w)
    o_w = _dot(p_w.astype(BF16), wn[:, KV_WIDTH:2 * KV_WIDTH].astype(BF16))

    gate = _sigmoid(gt_ref[0])
    o_ref[0] = gate[:, 0:1] * o_c + gate[:, 1:2] * o_s + gate[:, 2:3] * o_w


def _attn_sample(qbd, gate16, slope16, kc_all, vc_all, pages, page_table, new_rows, state_win, expand, cmat, rmat):
    nb, n_pages = page_table.shape
    page_rows = pages.shape[2]
    past_len = n_pages * page_rows
    n_sel = -(-(past_len + 1) // SEL_BLOCK)
    wb = state_win.shape[1]
    n_cmp_pad = kc_all.shape[1]
    kern = functools.partial(_attn_sample_kernel, n_pages=n_pages, past_len=past_len, n_sel=n_sel)
    per_b = lambda *shape: pl.BlockSpec((1,) + shape, lambda b, pt: (b,) + (0,) * len(shape))
    const = lambda *shape: pl.BlockSpec(shape, lambda b, pt: (0,) * len(shape))
    page_specs = [
        pl.BlockSpec((1, 4, page_rows, 128), functools.partial(lambda p, b, pt: (pt[b, p], 1, 0, 0), p))
        for p in range(n_pages)]
    return pl.pallas_call(
        kern,
        grid_spec=pltpu.PrefetchScalarGridSpec(
            num_scalar_prefetch=1,
            grid=(nb,),
            in_specs=[per_b(N_HEADS, KV_WIDTH), per_b(N_HEADS, 128), const(N_HEADS, 128),
                      per_b(n_cmp_pad, KV_WIDTH), per_b(n_cmp_pad, KV_WIDTH)]
            + page_specs
            + [per_b(1, 6 * KV_WIDTH), per_b(wb, 2 * KV_WIDTH), const(*expand.shape), const(*cmat.shape),
               const(*rmat.shape)],
            out_specs=[per_b(N_HEADS, KV_WIDTH), per_b(wb, 2 * KV_WIDTH)],
        ),
        out_shape=[jax.ShapeDtypeStruct((nb, N_HEADS, KV_WIDTH), F32),
                   jax.ShapeDtypeStruct((nb, wb, 2 * KV_WIDTH), F32)],
        compiler_params=_params(("parallel",), 48),
        name="attn_sample",
    )(page_table, qbd, gate16, slope16, kc_all, vc_all, *([pages] * n_pages), new_rows, state_win, expand, cmat,
      rmat)


def _alibi_slopes():
    return jnp.exp2(-8.0 * (jnp.arange(N_HEADS, dtype=F32) + 1.0) / N_HEADS)


def _cmp_to_sel(n_cmp_pad, n_cmp, n_sel_pad):
    m = np.zeros((n_cmp_pad, n_sel_pad), np.float32)
    for n in range(n_cmp):
        for k in range(CMP_BLOCK // CMP_STRIDE):
            m[n, (n + k) * CMP_STRIDE // SEL_BLOCK] += 1.0
    return m


def _pos_cols(pos, width):
    cols = np.zeros((pos.shape[0], width), np.float32)
    cols[:, 0] = cols[:, 1] = pos % 64
    cols[:, 2] = cols[:, 3] = pos // 64
    return cols


def kernel(x_prompt, x_sample, state_pool, cache_kv_pages, state_win, page_table, norm_mix, norm_mlp, w_up, w_down,
           pool_w, pool_scale, norm_kv, w_kv, cmp_pe, cmp_w1, cmp_w2, w_qg, b_gate, w_o, norm_final):
    b, t, d = x_prompt.shape
    nb = x_sample.shape[0]
    n_phys, page_rows = cache_kv_pages.shape[:2]
    n_q = N_HEADS * HEAD_DIM
    row = lambda v: v.reshape(1, -1)

    wu = w_up.astype(BF16)
    wd = w_down.astype(BF16)
    pw = pool_w[0].astype(BF16)
    wkv = w_kv.astype(BF16)
    wo = w_o[0].astype(BF16)
    hh = np.arange(N_HEADS)
    gcols = ((hh // Q_PER_KV) * 128 + (hh % Q_PER_KV) * N_BRANCH)[:, None] + np.arange(N_BRANCH)[None, :]
    gcols = gcols.reshape(-1)
    wg = jnp.zeros((d, N_KV_HEADS * 128), F32).at[:, gcols].set(w_qg[0][:, n_q:])
    wqg = jnp.concatenate([w_qg[0][:, :n_q], wg], axis=1).astype(BF16)
    bqg = jnp.zeros((1, n_q + N_KV_HEADS * 128), F32).at[0, n_q + gcols].set(b_gate[0])
    w1_bf = cmp_w1.astype(BF16)
    w1r = w1_bf.reshape(2, 2, CMP_STRIDE, HEAD_DIM, CMP_HIDDEN)
    w1c = jnp.concatenate([w1r[:, 0], w1r[:, 1]], axis=-1)
    w1d = jnp.concatenate([jnp.concatenate([w1c, jnp.zeros_like(w1c)], axis=-1),
                           jnp.concatenate([jnp.zeros_like(w1c), w1c], axis=-1)], axis=2)
    w1d = w1d.reshape(2, CMP_STRIDE // 2, 4 * HEAD_DIM, 4 * CMP_HIDDEN)
    pe8 = jnp.broadcast_to(cmp_pe.reshape(2, 1, CMP_BLOCK * HEAD_DIM), (2, 8, CMP_BLOCK * HEAD_DIM))
    w2_bf = cmp_w2.astype(BF16)

    slopes = _alibi_slopes()
    s_hi = slopes.astype(BF16).astype(F32)
    s_lo = (slopes - s_hi).astype(BF16).astype(F32)
    slope_cols = jnp.stack([s_hi, s_lo, 64.0 * s_hi, 64.0 * s_lo], axis=-1)

    x1, u_last = _pool_prompt(x_prompt, row(norm_mix[0]), pw, row(pool_scale[0]))
    pool_p = u_last[:, None, 16 - POOL_BUF:, :]
    x2 = _mlp(x1.reshape(b * t, d), row(norm_mlp[0]), wu[0], wd[0], row(norm_final), final_norm=False, tm=1024)
    cache_kv_pages, x2 = lax.optimization_barrier((cache_kv_pages, x2))
    tq = 256
    n_sub_p = t // CMP_STRIDE
    n_sel_p = t // SEL_BLOCK
    tpos = np.arange(t)
    pc_t = jnp.asarray(np.concatenate([np.zeros((t, HEAD_DIM), np.float32), _pos_cols(tpos, HEAD_DIM)], axis=1))
    onehot_t = jnp.asarray(np.eye(n_sel_p, 128, dtype=np.float32)[tpos // SEL_BLOCK], BF16)
    kv_rows, kv_win, q_p, gate_p, ks_aug, vs, kw_aug, vw = _proj(
        x2, row(norm_kv), row(norm_mix[1]), wkv, wqg, bqg, pc_t, onehot_t, b=b, t=t)
    kv_rows_p = kv_rows.reshape(b, t, 2, 2, N_KV_HEADS, HEAD_DIM)
    win_new_p = kv_win.reshape(b, t, 2, N_KV_HEADS, HEAD_DIM)[:, -min(WINDOW, t):]

    dummy_pt = jnp.zeros((b, 1), jnp.int32)
    kc_p, vc_p = _compress(kv_rows.reshape(b, t, ROW_WIDTH), dummy_pt, w1d, w1_bf, pe8, w2_bf, paged=False)

    bcast = lambda a: jnp.broadcast_to(a[None, None], (b, N_KV_HEADS) + a.shape)
    pc_c = jnp.asarray(_pos_cols(CMP_STRIDE * np.arange(n_sub_p) + CMP_BLOCK - 1, HEAD_DIM), BF16)
    kc_aug = jnp.concatenate([kc_p.astype(BF16), bcast(pc_c)], axis=-1)
    cmat_t = jnp.asarray(_cmp_to_sel(n_sub_p, n_sub_p - 1, n_sel_p).T, BF16)
    slope_tab = jnp.zeros((N_KV_HEADS, Q_PER_KV, tq, 128), F32).at[:, :, :, HEAD_DIM:HEAD_DIM + 4].set(
        jnp.broadcast_to(slope_cols.reshape(N_KV_HEADS, Q_PER_KV, 1, 4), (N_KV_HEADS, Q_PER_KV, tq, 4)))
    o_p = _attn_prompt(q_p.reshape(b, t, n_q), gate_p.reshape(b, t, N_KV_HEADS * 128), slope_tab, kc_aug,
                       vc_p.astype(BF16), ks_aug, vs, kw_aug, vw, cmat_t, tq=tq)
    x_sample, state_pool, page_table, o_p = lax.optimization_barrier((x_sample, state_pool, page_table, o_p))
    x3, = _lin(o_p.reshape(b * t, n_q), wo, res=x2, tm=512)
    y_prompt = _mlp(x3, row(norm_mlp[1]), wu[1], wd[1], row(norm_final), final_norm=True, tm=1024).reshape(b, t, d)

    xs0 = x_sample.reshape(nb, d)
    xs1, u_s = _pool_sample(xs0, state_pool[:, 0], row(norm_mix[0]), pw, row(pool_scale[0]))
    pool_s = jnp.concatenate([state_pool[:, 0, 1:], u_s[:, None]], axis=1)[:, None]
    xs2 = _mlp(xs1, row(norm_mlp[0]), wu[0], wd[0], row(norm_final), final_norm=False, tm=nb)
    kv_s, = _lin(xs2, wkv, g=row(norm_kv), tm=nb)
    kv_rows_s = kv_s[:, :ROW_WIDTH].reshape(nb, 1, 2, 2, N_KV_HEADS, HEAD_DIM)

    pages = cache_kv_pages.reshape(n_phys, page_rows, ROW_WIDTH // 128, 128).transpose(0, 2, 1, 3)
    kc_s, vc_s = _compress(pages, page_table, w1d, w1_bf, pe8, w2_bf, paged=True)
    n_sub_s = kc_s.shape[2]
    all_heads = lambda a: a.transpose(0, 2, 1, 3).reshape(nb, n_sub_s, KV_WIDTH).astype(BF16)

    q_s, gate_s = _lin(xs2, wqg, g=row(norm_mix[1]), b=bqg, splits=(n_q, N_KV_HEADS * 128), tm=nb)
    q4 = q_s.reshape(nb, N_KV_HEADS, Q_PER_KV, HEAD_DIM) * (HEAD_DIM ** -0.5)
    qbd = jnp.einsum('bgrd,gh->bgrhd', q4, jnp.eye(N_KV_HEADS, dtype=F32)).reshape(nb, N_HEADS, KV_WIDTH)
    g16 = gate_s.reshape(nb, N_KV_HEADS, 128)[:, :, :Q_PER_KV * N_BRANCH].reshape(nb, N_HEADS, N_BRANCH)
    g16 = jnp.pad(g16, ((0, 0), (0, 0), (0, 128 - N_BRANCH)))
    slope16 = jnp.broadcast_to(slopes[:, None], (N_HEADS, 128))
    past_len = page_table.shape[1] * page_rows
    n_blk_pad = 64
    expand = jnp.asarray(np.eye(n_blk_pad, dtype=np.float32)[:, np.arange(past_len) // SEL_BLOCK], BF16)
    cmat_s = jnp.asarray(_cmp_to_sel(n_sub_s, n_sub_s - 1, n_blk_pad), BF16)
    rmat = jnp.asarray(np.kron(np.eye(N_KV_HEADS), np.ones((Q_PER_KV, Q_PER_KV))), BF16)
    o_s, win_new_s = _attn_sample(
        qbd.astype(BF16), g16, slope16, all_heads(kc_s), all_heads(vc_s), pages, page_table,
        kv_s.reshape(nb, 1, 6 * KV_WIDTH), state_win.reshape(nb, state_win.shape[1], 2 * KV_WIDTH),
        expand, cmat_s, rmat)
    o5 = o_s.reshape(nb, N_KV_HEADS, Q_PER_KV, N_KV_HEADS, HEAD_DIM)
    o_heads = jnp.stack([o5[:, gh, :, gh] for gh in range(N_KV_HEADS)], axis=1).reshape(nb, n_q)
    xs3, = _lin(o_heads, wo, res=xs2, tm=nb)
    y_sample = _mlp(xs3, row(norm_mlp[1]), wu[1], wd[1], row(norm_final), final_norm=True, tm=nb).reshape(nb, 1, d)

    return (y_prompt, y_sample, pool_p, pool_s, kv_rows_p, kv_rows_s, win_new_p,
            win_new_s.reshape(state_win.shape))
```

```python
import functools

import numpy as np
import jax
import jax.numpy as jnp
from jax import lax
from jax.experimental import pallas as pl
from jax.experimental.pallas import tpu as pltpu

D_MODEL = 1024
POOL_WINDOWS = (2, 4, 8, 16)
POOL_GROUP_DIM = D_MODEL // len(POOL_WINDOWS)
POOL_BUF = max(POOL_WINDOWS) - 1
N_HEADS = 16
HEAD_DIM = 64
N_KV_HEADS = 4
Q_PER_KV = N_HEADS // N_KV_HEADS
N_BRANCH = 3
CMP_BLOCK = 32
CMP_STRIDE = 16
CMP_HIDDEN = 2 * HEAD_DIM
SEL_BLOCK = 64
N_SELECT = 16
WINDOW = 512
RMS_EPS = 1e-6
FORCED_PRIORITY = 1e6
KV_WIDTH = N_KV_HEADS * HEAD_DIM
ROW_WIDTH = 2 * 2 * KV_WIDTH

NEG = -1e30
F32 = jnp.float32
BF16 = jnp.bfloat16
MIB = 1024 * 1024


def _params(semantics, vmem_mib):
    return pltpu.CompilerParams(dimension_semantics=semantics, vmem_limit_bytes=vmem_mib * MIB)


def _rms(x, g):
    return x * lax.rsqrt(jnp.mean(x * x, axis=-1, keepdims=True) + RMS_EPS) * g


def _dot(a, b):
    return jnp.dot(a, b, preferred_element_type=F32)


def _dot_nt(a, b):
    return lax.dot_general(a, b, (((1,), (1,)), ((), ())), preferred_element_type=F32)


def _split_bf16(x):
    hi = x.astype(BF16)
    lo = (x - hi.astype(F32)).astype(BF16)
    return hi, lo


def _sigmoid(x):
    return 1.0 / (1.0 + jnp.exp(-x))


def _pool_prompt_kernel(x_ref, xp_ref, g_ref, w_ref, sc_ref, y_ref, ul_ref, u_scr, *, tt):
    i = pl.program_id(1)
    g = g_ref[...]
    x = x_ref[0]
    u = _rms(x, g)
    up = _rms(xp_ref[0], g) * (i > 0).astype(F32)
    u_scr[0:16, :] = up
    u_scr[16:16 + tt, :] = u
    t = i * tt + lax.broadcasted_iota(jnp.int32, (tt, 1), 0)
    for gi, win in enumerate(POOL_WINDOWS):
        cols = slice(gi * POOL_GROUP_DIM, (gi + 1) * POOL_GROUP_DIM)
        acc = u[:, cols]
        for k in range(1, win):
            acc = acc + u_scr[16 - k:16 - k + tt, cols]
        cnt = jnp.minimum(t + 1, win).astype(F32)
        diff = acc / cnt - u[:, cols]
        yg = _dot(diff.astype(BF16), w_ref[gi])
        y_ref[0, :, cols] = x[:, cols] + yg * sc_ref[:, cols]
    ul_ref[0] = u[tt - 16:, :]


def _pool_prompt(x, g, w_bf, scale, *, tt=256):
    b, t, d = x.shape
    n_t = t // tt
    kern = functools.partial(_pool_prompt_kernel, tt=tt)
    return pl.pallas_call(
        kern,
        grid=(b, n_t),
        in_specs=[
            pl.BlockSpec((1, tt, d), lambda bi, i: (bi, i, 0)),
            pl.BlockSpec((1, 16, d), lambda bi, i: (bi, jnp.maximum(i * (tt // 16) - 1, 0), 0)),
            pl.BlockSpec((1, d), lambda bi, i: (0, 0)),
            pl.BlockSpec((len(POOL_WINDOWS), POOL_GROUP_DIM, POOL_GROUP_DIM), lambda bi, i: (0, 0, 0)),
            pl.BlockSpec((1, d), lambda bi, i: (0, 0)),
        ],
        out_specs=[
            pl.BlockSpec((1, tt, d), lambda bi, i: (bi, i, 0)),
            pl.BlockSpec((1, 16, d), lambda bi, i: (bi, 0, 0)),
        ],
        out_shape=[jax.ShapeDtypeStruct((b, t, d), F32), jax.ShapeDtypeStruct((b, 16, d), F32)],
        scratch_shapes=[pltpu.VMEM((16 + tt, d), F32)],
        compiler_params=_params(("parallel", "arbitrary"), 32),
        name="pool_prompt",
    )(x, x, g, w_bf, scale)


def _pool_sample_kernel(x_ref, past_ref, g_ref, w_ref, sc_ref, y_ref, u_ref):
    x = x_ref[...]
    u = _rms(x, g_ref[...])
    for gi, win in enumerate(POOL_WINDOWS):
        cols = slice(gi * POOL_GROUP_DIM, (gi + 1) * POOL_GROUP_DIM)
        acc = u[:, cols]
        for k in range(1, win):
            acc = acc + past_ref[:, POOL_BUF - k, cols]
        diff = acc / float(win) - u[:, cols]
        yg = _dot(diff.astype(BF16), w_ref[gi])
        y_ref[:, cols] = x[:, cols] + yg * sc_ref[:, cols]
    u_ref[...] = u


def _pool_sample(x, past, g, w_bf, scale, *, bt=32):
    nb, d = x.shape
    return pl.pallas_call(
        _pool_sample_kernel,
        grid=(nb // bt,),
        in_specs=[
            pl.BlockSpec((bt, d), lambda i: (i, 0)),
            pl.BlockSpec((bt, POOL_BUF, d), lambda i: (i, 0, 0)),
            pl.BlockSpec((1, d), lambda i: (0, 0)),
            pl.BlockSpec((len(POOL_WINDOWS), POOL_GROUP_DIM, POOL_GROUP_DIM), lambda i: (0, 0, 0)),
            pl.BlockSpec((1, d), lambda i: (0, 0)),
        ],
        out_specs=[pl.BlockSpec((bt, d), lambda i: (i, 0)), pl.BlockSpec((bt, d), lambda i: (i, 0))],
        out_shape=[jax.ShapeDtypeStruct((nb, d), F32), jax.ShapeDtypeStruct((nb, d), F32)],
        compiler_params=_params(("parallel",), 32),
        name="pool_sample",
    )(x, past, g, w_bf, scale)


def _mlp_kernel(x_ref, g_ref, wu_ref, wd_ref, gf_ref, o_ref, xn_scr, acc_scr, *, final_norm):
    j = pl.program_id(1)

    @pl.when(j == 0)
    def _():
        xn_scr[...] = _rms(x_ref[...], g_ref[...]).astype(BF16)
        acc_scr[...] = jnp.zeros_like(acc_scr)

    h = jnp.maximum(_dot(xn_scr[...], wu_ref[...]), 0.0)
    acc_scr[...] += _dot((h * h).astype(BF16), wd_ref[...])

    @pl.when(j == pl.num_programs(1) - 1)
    def _():
        r = x_ref[...] + acc_scr[...]
        o_ref[...] = _rms(r, gf_ref[...]) if final_norm else r


def _mlp(x, g, wu_bf, wd_bf, gf, *, final_norm, tm, tf=1024):
    m, d = x.shape
    f = wu_bf.shape[1]
    kern = functools.partial(_mlp_kernel, final_norm=final_norm)
    return pl.pallas_call(
        kern,
        grid=(m // tm, f // tf),
        in_specs=[
            pl.BlockSpec((tm, d), lambda i, j: (i, 0)),
            pl.BlockSpec((1, d), lambda i, j: (0, 0)),
            pl.BlockSpec((d, tf), lambda i, j: (0, j)),
            pl.BlockSpec((tf, d), lambda i, j: (j, 0)),
            pl.BlockSpec((1, d), lambda i, j: (0, 0)),
        ],
        out_specs=pl.BlockSpec((tm, d), lambda i, j: (i, 0)),
        out_shape=jax.ShapeDtypeStruct((m, d), F32),
        scratch_shapes=[pltpu.VMEM((tm, d), BF16), pltpu.VMEM((tm, d), F32)],
        compiler_params=_params(("parallel", "arbitrary"), 48),
        name="mlp",
    )(x, g, wu_bf, wd_bf, gf)


def _proj_kernel(x_ref, gkv_ref, gq_ref, wkv_ref, wqg_ref, bqg_ref, pc_ref, oh_ref,
                 rows_ref, win_ref, q_ref, gate_ref, ks_ref, vs_ref, kw_ref, vw_ref):
    x = x_ref[...]
    xh = x * lax.rsqrt(jnp.mean(x * x, axis=-1, keepdims=True) + RMS_EPS)
    hkv = _dot((xh * gkv_ref[...]).astype(BF16), wkv_ref[...])
    hq = _dot((xh * gq_ref[...]).astype(BF16), wqg_ref[...]) + bqg_ref[...]
    n_q = N_HEADS * HEAD_DIM
    rows_ref[...] = hkv[:, :ROW_WIDTH]
    win_ref[...] = hkv[:, ROW_WIDTH:]
    q_ref[...] = hq[:, :n_q]
    gate_ref[...] = hq[:, n_q:]

    lane = lax.broadcasted_iota(jnp.int32, (x.shape[0], 128), 1)
    pc = pc_ref[...]
    ones = (lane == HEAD_DIM).astype(F32)

    def head(col0, gh, filler):
        pair = hkv[:, col0 + (gh // 2) * 128:col0 + (gh // 2 + 1) * 128]
        if gh % 2:
            pair = pltpu.roll(pair, HEAD_DIM, 1)
        return jnp.where(lane < HEAD_DIM, pair, filler).astype(BF16)

    for gh in range(N_KV_HEADS):
        ks_ref[0, gh, :, 0:128] = head(2 * KV_WIDTH, gh, pc)
        ks_ref[0, gh, :, 128:256] = oh_ref[...]
        vs_ref[0, gh] = head(3 * KV_WIDTH, gh, ones)
        kw_ref[0, gh] = head(4 * KV_WIDTH, gh, pc)
        vw_ref[0, gh] = head(5 * KV_WIDTH, gh, ones)


def _proj(x, g_kv, g_q, wkv_bf, wqg_bf, bqg, pos_cols, onehot, *, b, t, tm=512):
    m, d = x.shape
    nt = t // tm
    n_kv = wkv_bf.shape[1]
    n_qg = wqg_bf.shape[1]
    n_q = N_HEADS * HEAD_DIM
    const = lambda *shape: pl.BlockSpec(shape, lambda i: (0,) * len(shape))
    rows_of = lambda n: pl.BlockSpec((tm, n), lambda i: (i, 0))
    heads_of = lambda n: pl.BlockSpec((1, N_KV_HEADS, tm, n), lambda i: (i // nt, 0, i % nt, 0))
    heads_shape = lambda n: jax.ShapeDtypeStruct((b, N_KV_HEADS, t, n), BF16)
    return pl.pallas_call(
        _proj_kernel,
        grid=(m // tm,),
        in_specs=[rows_of(d), const(1, d), const(1, d), const(d, n_kv), const(d, n_qg), const(1, n_qg),
                  pl.BlockSpec((tm, 128), lambda i: (i % nt, 0)), pl.BlockSpec((tm, 128), lambda i: (i % nt, 0))],
        out_specs=[rows_of(ROW_WIDTH), rows_of(n_kv - ROW_WIDTH), rows_of(n_q), rows_of(n_qg - n_q),
                   heads_of(256), heads_of(128), heads_of(128), heads_of(128)],
        out_shape=[jax.ShapeDtypeStruct((m, ROW_WIDTH), F32), jax.ShapeDtypeStruct((m, n_kv - ROW_WIDTH), F32),
                   jax.ShapeDtypeStruct((m, n_q), F32), jax.ShapeDtypeStruct((m, n_qg - n_q), F32),
                   heads_shape(256), heads_shape(128), heads_shape(128), heads_shape(128)],
        compiler_params=_params(("parallel",), 56),
        name="proj",
    )(x, g_kv, g_q, wkv_bf, wqg_bf, bqg, pos_cols, onehot)


def _lin_kernel(*refs, norm, bias, residual, splits):
    refs = list(refs)
    x_ref = refs.pop(0)
    g_ref = refs.pop(0) if norm else None
    w_ref = refs.pop(0)
    b_ref = refs.pop(0) if bias else None
    r_ref = refs.pop(0) if residual else None
    x = x_ref[...]
    if norm:
        x = _rms(x, g_ref[...])
    h = _dot(x.astype(BF16), w_ref[...])
    if bias:
        h = h + b_ref[...]
    if residual:
        h = h + r_ref[...]
    c0 = 0
    for o_ref, n in zip(refs, splits):
        o_ref[...] = h[:, c0:c0 + n]
        c0 += n


def _lin(x, w_bf, *, g=None, b=None, res=None, splits=None, tm):
    m, k = x.shape
    n = w_bf.shape[1]
    splits = splits or (n,)
    kern = functools.partial(_lin_kernel, norm=g is not None, bias=b is not None,
                             residual=res is not None, splits=splits)
    args = [x]
    in_specs = [pl.BlockSpec((tm, k), lambda i: (i, 0))]
    if g is not None:
        args.append(g)
        in_specs.append(pl.BlockSpec((1, k), lambda i: (0, 0)))
    args.append(w_bf)
    in_specs.append(pl.BlockSpec((k, n), lambda i: (0, 0)))
    if b is not None:
        args.append(b)
        in_specs.append(pl.BlockSpec((1, n), lambda i: (0, 0)))
    if res is not None:
        args.append(res)
        in_specs.append(pl.BlockSpec((tm, n), lambda i: (i, 0)))
    outs = pl.pallas_call(
        kern,
        grid=(m // tm,),
        in_specs=in_specs,
        out_specs=[pl.BlockSpec((tm, s), lambda i: (i, 0)) for s in splits],
        out_shape=[jax.ShapeDtypeStruct((m, s), F32) for s in splits],
        compiler_params=_params(("parallel",), 48),
        name="lin",
    )(*args)
    return outs


def _compress_kernel(pt_ref, *refs, n_page_refs, paged, rows_per_ref, n_sub):
    del pt_ref
    page_refs = refs[:n_page_refs]
    w1d_ref, w1_ref, pe_ref, w2_ref, kc_ref, vc_ref = refs[n_page_refs:]
    n_take = rows_per_ref // CMP_STRIDE

    def tap_rows(c, j):
        take = pl.ds(j, n_take, stride=CMP_STRIDE)
        if paged:
            return jnp.concatenate([r[0, c, take, :] for r in page_refs], axis=0)
        return page_refs[c][0, take, :]

    for kv, o_ref in enumerate((kc_ref, vc_ref)):
        pew = _dot(pe_ref[kv].astype(BF16), w1_ref[kv])[0:1]
        for c2 in range(2):
            acc = None
            for jp in range(CMP_STRIDE // 2):
                lhs = jnp.concatenate([tap_rows(2 * kv + c2, 2 * jp), tap_rows(2 * kv + c2, 2 * jp + 1)], axis=1)
                d = _dot(lhs.astype(BF16), w1d_ref[kv, jp])
                acc = d if acc is None else acc + d
            for e in range(2):
                a = acc[:, 2 * e * CMP_HIDDEN:(2 * e + 1) * CMP_HIDDEN]
                b2 = acc[:, (2 * e + 1) * CMP_HIDDEN:(2 * e + 2) * CMP_HIDDEN]
                hid = a + pltpu.roll(b2, a.shape[0] - 1, 0) + pew
                act = hid * _sigmoid(hid)
                out = _dot(act.astype(BF16), w2_ref[kv])
                for k in range(out.shape[0] // n_sub):
                    o_ref[k, 2 * c2 + e] = out[k * n_sub:(k + 1) * n_sub]


def _compress(pages, page_table, w1d_bf, w1_bf, pe8, w2_bf, *, paged):
    if paged:
        nb, n_pages = page_table.shape
        rows_per_ref = pages.shape[2]
        n_sub = n_pages * rows_per_ref // CMP_STRIDE
        seqs = 2
        page_specs = [
            pl.BlockSpec((1, 4, rows_per_ref, 128),
                         functools.partial(lambda k, p, b, pt: (pt[seqs * b + k, p], 0, 0, 0), k, p))
            for k in range(seqs) for p in range(n_pages)]
    else:
        nb, rows_per_ref, _ = pages.shape
        n_sub = rows_per_ref // CMP_STRIDE
        seqs = 1
        page_specs = [pl.BlockSpec((1, rows_per_ref, 128), functools.partial(lambda c, b, pt: (b, 0, c), c))
                      for c in range(4)]
    page_args = [pages] * len(page_specs)
    kern = functools.partial(_compress_kernel, n_page_refs=len(page_args), paged=paged, rows_per_ref=rows_per_ref,
                             n_sub=n_sub)
    const = lambda *shape: pl.BlockSpec(shape, lambda b, pt: (0,) * len(shape))
    out_spec = pl.BlockSpec((seqs, N_KV_HEADS, n_sub, HEAD_DIM), lambda b, pt: (b, 0, 0, 0))
    return pl.pallas_call(
        kern,
        grid_spec=pltpu.PrefetchScalarGridSpec(
            num_scalar_prefetch=1,
            grid=(nb // seqs,),
            in_specs=page_specs + [
                const(2, CMP_STRIDE // 2, 4 * HEAD_DIM, 4 * CMP_HIDDEN),
                const(2, CMP_BLOCK * HEAD_DIM, CMP_HIDDEN),
                const(2, 8, CMP_BLOCK * HEAD_DIM),
                const(2, CMP_HIDDEN, HEAD_DIM),
            ],
            out_specs=[out_spec, out_spec],
        ),
        out_shape=[jax.ShapeDtypeStruct((nb, N_KV_HEADS, n_sub, HEAD_DIM), F32)] * 2,
        compiler_params=_params(("parallel",), 48),
        name="compress",
    )(page_table, *page_args, w1d_bf, w1_bf, pe8, w2_bf)


def _attn_prompt_kernel(q_ref, gt_ref, st_ref, kc_ref, vc_ref, ks_ref, vs_ref, kw_ref, vw_ref, cm_ref, dm_ref,
                        dc_ref, wb_ref, o_ref, qa_scr, pri_scr, m_scr, ala_scr, alb_scr, sa_scr, sb_scr, pa_scr,
                        pb_scr, acc_scr, *, tq, tk, rc, n_sel):
    c0 = pl.program_id(2) * tq
    rows = Q_PER_KV * tq
    band = WINDOW + tq

    lane = lax.broadcasted_iota(jnp.int32, (tq, 128), 1)
    qb = q_ref[0] * (HEAD_DIM ** -0.5)
    for r in range(Q_PER_KV):
        pair = qb[:, (r // 2) * 128:(r // 2 + 1) * 128]
        if r % 2:
            pair = pltpu.roll(pair, HEAD_DIM, 1)
        qa_scr[r * tq:(r + 1) * tq, 0:128] = jnp.where(lane < HEAD_DIM, pair, st_ref[0, r]).astype(BF16)
    ql = qa_scr[:, 0:128]

    n_cmp_pad = kc_ref.shape[2]
    s = _dot_nt(ql, kc_ref[0, 0])
    mask = jnp.concatenate([dc_ref[...]] * Q_PER_KV, axis=0) <= c0
    s = jnp.where(mask, s, NEG)
    m = jnp.max(s, axis=-1, keepdims=True)
    e = jnp.where(mask, jnp.exp(s - m), 0.0)
    p = e / jnp.maximum(jnp.sum(e, axis=-1, keepdims=True), 1e-30)
    o_c = _dot(p.astype(BF16), vc_ref[0, 0])

    psum = p[0:tq] + p[tq:2 * tq] + p[2 * tq:3 * tq] + p[3 * tq:4 * tq]
    hi, lo = _split_bf16(psum)
    imp = _dot_nt(cm_ref[...], hi) + _dot_nt(cm_ref[...], lo)
    blk = lax.broadcasted_iota(jnp.int32, (n_sel, tq), 0)
    tcol = c0 + lax.broadcasted_iota(jnp.int32, (n_sel, tq), 1)
    cur = tcol >> 6
    forced = (blk == 0) | (blk == cur) | (blk == cur - 1)
    valid = blk * SEL_BLOCK <= tcol
    pri = jnp.where(valid, jnp.where(forced, FORCED_PRIORITY, imp), -1.0)
    pri_scr[...] = pri
    n_valid = c0 // SEL_BLOCK + tq // SEL_BLOCK
    n_walk = jnp.where(n_valid > N_SELECT, (n_valid + 7) // 8, 0)

    def walk(g8, rank):
        grp = pri_scr[pl.ds(pl.multiple_of(g8 * 8, 8), 8), :]
        for i in range(8):
            row = grp[i:i + 1, :]
            beats = (row > pri) | ((row == pri) & (blk > g8 * 8 + i))
            rank = rank + beats.astype(jnp.int32)
        return rank

    rank = lax.fori_loop(0, n_walk, walk, jnp.zeros((n_sel, tq), jnp.int32))
    bias = jnp.where(rank < min(N_SELECT, n_sel), 0.0, NEG).T
    right = jnp.concatenate([bias, jnp.zeros((tq, 128 - n_sel), F32)], axis=1).astype(BF16)
    for r in range(Q_PER_KV):
        qa_scr[r * tq:(r + 1) * tq, 128:256] = right

    m_scr[...] = jnp.full_like(m_scr, NEG)
    acc_scr[...] = jnp.zeros_like(acc_scr)

    def scores(j):
        return _dot_nt(qa_scr[...], ks_ref[0, 0, pl.ds(pl.multiple_of(j * tk, tk), tk), :])

    def soft_pv(s_scr, p_scr, al_scr, j, masked):
        k0 = pl.multiple_of(j * tk, tk)
        for i in range(rows // rc):
            rs = slice(i * rc, (i + 1) * rc)
            qs = (i * rc) % tq
            sc = s_scr[rs, :]
            if masked:
                sc = jnp.where(dm_ref[qs:qs + rc, :] <= c0 - k0, sc, NEG)
            m_old = m_scr[rs]
            m_new = jnp.maximum(m_old, jnp.max(sc, axis=-1, keepdims=True))
            al_scr[rs] = jnp.exp(m_old - m_new)
            p_scr[rs] = jnp.exp(sc - m_new).astype(BF16)
            m_scr[rs] = m_new
        acc_scr[...] = al_scr[...] * acc_scr[...] + _dot(p_scr[...], vs_ref[0, 0, pl.ds(k0, tk), :])

    n_tiles = c0 // tk + 1
    n_loop = (n_tiles - 1) // 2
    sa_scr[...] = scores(0)

    def tile_pair(jj, carry):
        sb_scr[...] = scores(2 * jj + 1)
        soft_pv(sa_scr, pa_scr, ala_scr, 2 * jj, False)
        sa_scr[...] = scores(2 * jj + 2)
        soft_pv(sb_scr, pb_scr, alb_scr, 2 * jj + 1, False)
        return carry

    lax.fori_loop(0, n_loop, tile_pair, 0)

    @pl.when(n_tiles % 2 == 0)
    def _():
        sb_scr[...] = scores(2 * n_loop + 1)
        soft_pv(sa_scr, pa_scr, ala_scr, 2 * n_loop, False)
        soft_pv(sb_scr, pb_scr, alb_scr, 2 * n_loop + 1, True)

    @pl.when(n_tiles % 2 == 1)
    def _():
        soft_pv(sa_scr, pa_scr, ala_scr, 2 * n_loop, True)

    acc = acc_scr[...]
    o_s = acc[:, :HEAD_DIM] / acc[:, HEAD_DIM:HEAD_DIM + 1]

    w0 = pl.multiple_of(jnp.maximum(c0 - WINDOW, 0), 128)
    sw = _dot_nt(ql, kw_ref[0, 0, pl.ds(w0, band), :]) + jnp.concatenate([wb_ref[0]] * Q_PER_KV, axis=0)
    pw = jnp.exp(sw - jnp.max(sw, axis=-1, keepdims=True))
    aw = _dot(pw.astype(BF16), vw_ref[0, 0, pl.ds(w0, band), :])
    o_w = aw[:, :HEAD_DIM] / aw[:, HEAD_DIM:HEAD_DIM + 1]

    gate = _sigmoid(gt_ref[0])
    outs = []
    for r in range(Q_PER_KV):
        rs = slice(r * tq, (r + 1) * tq)
        outs.append(gate[:, 3 * r:3 * r + 1] * o_c[rs] + gate[:, 3 * r + 1:3 * r + 2] * o_s[rs]
                    + gate[:, 3 * r + 2:3 * r + 3] * o_w[rs])
    o_ref[0] = jnp.concatenate(outs, axis=1)


def _attn_prompt(q, gate_pre, slope_tab, kc_aug, vc, ks_aug, vs, kw_aug, vw, cmat_t, *, tq=128, tk=512, rc=64):
    b, t, _ = q.shape
    n_sel = t // SEL_BLOCK
    n_cmp_pad = kc_aug.shape[2]
    kern = functools.partial(_attn_prompt_kernel, tq=tq, tk=tk, rc=rc, n_sel=n_sel)
    rows = Q_PER_KV * tq
    band = WINDOW + tq
    qi = np.arange(tq)[:, None]
    dmat = jnp.asarray(np.arange(tk)[None, :] - qi, jnp.int32)
    dcmp = jnp.asarray(CMP_STRIDE * np.arange(n_cmp_pad)[None, :] + (CMP_BLOCK - 1) - qi, jnp.int32)
    dist = (np.minimum(np.arange(WINDOW // tq + 1) * tq, WINDOW)[:, None, None] + qi[None]
            - np.arange(band)[None, None, :])
    wbias = jnp.asarray(np.where((dist >= 0) & (dist < WINDOW), 0.0, NEG), F32)
    per_bg = lambda *shape: pl.BlockSpec((1, 1) + shape, lambda bi, gi, ci: (bi, gi, 0, 0))
    return pl.pallas_call(
        kern,
        grid=(b, N_KV_HEADS, t // tq),
        in_specs=[
            pl.BlockSpec((1, tq, KV_WIDTH), lambda bi, gi, ci: (bi, ci, gi)),
            pl.BlockSpec((1, tq, 128), lambda bi, gi, ci: (bi, ci, gi)),
            pl.BlockSpec((1, Q_PER_KV, tq, 128), lambda bi, gi, ci: (gi, 0, 0, 0)),
            per_bg(n_cmp_pad, 128),
            per_bg(n_cmp_pad, HEAD_DIM),
            per_bg(t, 256),
            per_bg(t, 128),
            per_bg(t, 128),
            per_bg(t, 128),
            pl.BlockSpec((n_sel, n_cmp_pad), lambda bi, gi, ci: (0, 0)),
            pl.BlockSpec((tq, tk), lambda bi, gi, ci: (0, 0)),
            pl.BlockSpec((tq, n_cmp_pad), lambda bi, gi, ci: (0, 0)),
            pl.BlockSpec((1, tq, band), lambda bi, gi, ci: (jnp.minimum(ci, WINDOW // tq), 0, 0)),
        ],
        out_specs=pl.BlockSpec((1, tq, KV_WIDTH), lambda bi, gi, ci: (bi, ci, gi)),
        out_shape=jax.ShapeDtypeStruct((b, t, N_HEADS * HEAD_DIM), F32),
        scratch_shapes=[
            pltpu.VMEM((rows, 256), BF16),
            pltpu.VMEM((n_sel, tq), F32),
            pltpu.VMEM((rows, 1), F32),
            pltpu.VMEM((rows, 1), F32),
            pltpu.VMEM((rows, 1), F32),
            pltpu.VMEM((rows, tk), F32),
            pltpu.VMEM((rows, tk), F32),
            pltpu.VMEM((rows, tk), BF16),
            pltpu.VMEM((rows, tk), BF16),
            pltpu.VMEM((rows, 128), F32),
        ],
        compiler_params=_params(("parallel", "parallel", "arbitrary"), 48),
        name="attn_prompt",
    )(q, gate_pre, slope_tab, kc_aug, vc, ks_aug, vs, kw_aug, vw, cmat_t, dmat, dcmp, wbias)


def _softmax_lanes(s):
    e = jnp.exp(s - jnp.max(s, axis=-1, keepdims=True))
    return e / jnp.sum(e, axis=-1, keepdims=True)


def _attn_sample_kernel(pt_ref, *refs, n_pages, past_len, n_sel):
    del pt_ref
    q_ref, gt_ref, sl_ref, kc_ref, vc_ref = refs[:5]
    page_refs = refs[5:5 + n_pages]
    new_ref, win_ref, ex_ref, cm_ref, rm_ref, o_ref, wn_ref = refs[5 + n_pages:]
    q = q_ref[0]
    sl = sl_ref[:, 0:1]
    new = new_ref[0]

    n_cmp_pad = kc_ref.shape[1]
    dist_c = (past_len - (CMP_BLOCK - 1)
              - CMP_STRIDE * lax.broadcasted_iota(jnp.int32, (N_HEADS, n_cmp_pad), 1)).astype(F32)
    s = _dot_nt(q, kc_ref[0]) - sl * dist_c
    mask = dist_c >= 0
    s = jnp.where(mask, s, NEG)
    e = jnp.where(mask, jnp.exp(s - jnp.max(s, axis=-1, keepdims=True)), 0.0)
    p_c = e / jnp.maximum(jnp.sum(e, axis=-1, keepdims=True), 1e-30)
    o_c = _dot(p_c.astype(BF16), vc_ref[0])

    hi, lo = _split_bf16(p_c)
    imp = _dot(hi, cm_ref[...]) + _dot(lo, cm_ref[...])
    hi, lo = _split_bf16(imp)
    imp = _dot(rm_ref[...], hi) + _dot(rm_ref[...], lo)
    n_blk = imp.shape[1]
    blk = lax.broadcasted_iota(jnp.int32, (N_HEADS, n_blk), 1)
    cur = past_len // SEL_BLOCK
    forced = (blk == 0) | (blk == cur) | (blk == cur - 1)
    valid = blk * SEL_BLOCK <= past_len
    pri = jnp.where(valid, jnp.where(forced, FORCED_PRIORITY, imp), -1.0)
    pri = jnp.where(blk < n_sel, pri, -2.0)
    rank = jnp.zeros((N_HEADS, n_blk), jnp.int32)
    for s2 in range(n_sel):
        col = pri[:, s2:s2 + 1]
        beats = (col > pri) | ((col == pri) & (blk > s2))
        rank = rank + beats.astype(jnp.int32)
    bias = jnp.where(rank < min(N_SELECT, n_sel), 0.0, NEG)

    k_sel = jnp.concatenate([jnp.concatenate([r[0, 0], r[0, 1]], axis=1) for r in page_refs],
                            axis=0).astype(BF16)
    v_sel = jnp.concatenate([jnp.concatenate([r[0, 2], r[0, 3]], axis=1) for r in page_refs],
                            axis=0).astype(BF16)
    dist_s = (past_len - lax.broadcasted_iota(jnp.int32, (N_HEADS, past_len), 1)).astype(F32)
    s = _dot_nt(q, k_sel) - sl * dist_s + _dot(bias.astype(BF16), ex_ref[...])
    qf = q.astype(F32)
    k_new = new[:, 2 * KV_WIDTH:3 * KV_WIDTH].astype(BF16).astype(F32)
    v_new = new[:, 3 * KV_WIDTH:4 * KV_WIDTH].astype(BF16).astype(F32)
    s_new = jnp.sum(qf * k_new, axis=-1, keepdims=True) + bias[:, past_len // SEL_BLOCK:past_len // SEL_BLOCK + 1]
    m = jnp.maximum(jnp.max(s, axis=-1, keepdims=True), s_new)
    e = jnp.exp(s - m)
    e_new = jnp.exp(s_new - m)
    l = jnp.sum(e, axis=-1, keepdims=True) + e_new
    o_s = (_dot(e.astype(BF16), v_sel) + e_new.astype(BF16).astype(F32) * v_new) / l

    wb = win_ref.shape[1]
    rolled = pltpu.roll(win_ref[0], wb - 1, 0)
    rowi = lax.broadcasted_iota(jnp.int32, rolled.shape, 0)
    wn = jnp.where(rowi == wb - 1, new[:, 4 * KV_WIDTH:6 * KV_WIDTH], rolled)
    wn_ref[0] = wn
    dist_w = (wb - 1 - lax.broadcasted_iota(jnp.int32, (N_HEADS, wb), 1)).astype(F32)
    p_w = _softmax_lanes(_dot_nt(q, wn[:, 0:KV_WIDTH].astype(BF16)) - sl * dist_w)
    o_w = _dot(p_w.astype(BF16), wn[:, KV_WIDTH:2 * KV_WIDTH].astype(BF16))

    gate = _sigmoid(gt_ref[0])
    o_ref[0] = gate[:, 0:1] * o_c + gate[:, 1:2] * o_s + gate[:, 2:3] * o_w


def _attn_sample(qbd, gate16, slope16, kc_all, vc_all, pages, page_table, new_rows, state_win, expand, cmat, rmat):
    nb, n_pages = page_table.shape
    page_rows = pages.shape[2]
    past_len = n_pages * page_rows
    n_sel = -(-(past_len + 1) // SEL_BLOCK)
    wb = state_win.shape[1]
    n_cmp_pad = kc_all.shape[1]
    kern = functools.partial(_attn_sample_kernel, n_pages=n_pages, past_len=past_len, n_sel=n_sel)
    per_b = lambda *shape: pl.BlockSpec((1,) + shape, lambda b, pt: (b,) + (0,) * len(shape))
    const = lambda *shape: pl.BlockSpec(shape, lambda b, pt: (0,) * len(shape))
    page_specs = [
        pl.BlockSpec((1, 4, page_rows, 128), functools.partial(lambda p, b, pt: (pt[b, p], 1, 0, 0), p))
        for p in range(n_pages)]
    return pl.pallas_call(
        kern,
        grid_spec=pltpu.PrefetchScalarGridSpec(
            num_scalar_prefetch=1,
            grid=(nb,),
            in_specs=[per_b(N_HEADS, KV_WIDTH), per_b(N_HEADS, 128), const(N_HEADS, 128),
                      per_b(n_cmp_pad, KV_WIDTH), per_b(n_cmp_pad, KV_WIDTH)]
            + page_specs
            + [per_b(1, 6 * KV_WIDTH), per_b(wb, 2 * KV_WIDTH), const(*expand.shape), const(*cmat.shape),
               const(*rmat.shape)],
            out_specs=[per_b(N_HEADS, KV_WIDTH), per_b(wb, 2 * KV_WIDTH)],
        ),
        out_shape=[jax.ShapeDtypeStruct((nb, N_HEADS, KV_WIDTH), F32),
                   jax.ShapeDtypeStruct((nb, wb, 2 * KV_WIDTH), F32)],
        compiler_params=_params(("parallel",), 48),
        name="attn_sample",
    )(page_table, qbd, gate16, slope16, kc_all, vc_all, *([pages] * n_pages), new_rows, state_win, expand, cmat,
      rmat)


def _alibi_slopes():
    return jnp.exp2(-8.0 * (jnp.arange(N_HEADS, dtype=F32) + 1.0) / N_HEADS)


def _cmp_to_sel(n_cmp_pad, n_cmp, n_sel_pad):
    m = np.zeros((n_cmp_pad, n_sel_pad), np.float32)
    for n in range(n_cmp):
        for k in range(CMP_BLOCK // CMP_STRIDE):
            m[n, (n + k) * CMP_STRIDE // SEL_BLOCK] += 1.0
    return m


def _pos_cols(pos, width):
    cols = np.zeros((pos.shape[0], width), np.float32)
    cols[:, 0] = cols[:, 1] = pos % 64
    cols[:, 2] = cols[:, 3] = pos // 64
    return cols


def kernel(x_prompt, x_sample, state_pool, cache_kv_pages, state_win, page_table, norm_mix, norm_mlp, w_up, w_down,
           pool_w, pool_scale, norm_kv, w_kv, cmp_pe, cmp_w1, cmp_w2, w_qg, b_gate, w_o, norm_final):
    b, t, d = x_prompt.shape
    nb = x_sample.shape[0]
    n_phys, page_rows = cache_kv_pages.shape[:2]
    n_q = N_HEADS * HEAD_DIM
    row = lambda v: v.reshape(1, -1)

    wu = w_up.astype(BF16)
    wd = w_down.astype(BF16)
    pw = pool_w[0].astype(BF16)
    wkv = w_kv.astype(BF16)
    wo = w_o[0].astype(BF16)
    hh = np.arange(N_HEADS)
    gcols = ((hh // Q_PER_KV) * 128 + (hh % Q_PER_KV) * N_BRANCH)[:, None] + np.arange(N_BRANCH)[None, :]
    gcols = gcols.reshape(-1)
    wg = jnp.zeros((d, N_KV_HEADS * 128), F32).at[:, gcols].set(w_qg[0][:, n_q:])
    wqg = jnp.concatenate([w_qg[0][:, :n_q], wg], axis=1).astype(BF16)
    bqg = jnp.zeros((1, n_q + N_KV_HEADS * 128), F32).at[0, n_q + gcols].set(b_gate[0])
    w1_bf = cmp_w1.astype(BF16)
    w1r = w1_bf.reshape(2, 2, CMP_STRIDE, HEAD_DIM, CMP_HIDDEN)
    w1c = jnp.concatenate([w1r[:, 0], w1r[:, 1]], axis=-1)
    w1d = jnp.concatenate([jnp.concatenate([w1c, jnp.zeros_like(w1c)], axis=-1),
                           jnp.concatenate([jnp.zeros_like(w1c), w1c], axis=-1)], axis=2)
    w1d = w1d.reshape(2, CMP_STRIDE // 2, 4 * HEAD_DIM, 4 * CMP_HIDDEN)
    pe8 = jnp.broadcast_to(cmp_pe.reshape(2, 1, CMP_BLOCK * HEAD_DIM), (2, 8, CMP_BLOCK * HEAD_DIM))
    w2_bf = cmp_w2.astype(BF16)

    slopes = _alibi_slopes()
    s_hi = slopes.astype(BF16).astype(F32)
    s_lo = (slopes - s_hi).astype(BF16).astype(F32)
    slope_cols = jnp.stack([s_hi, s_lo, 64.0 * s_hi, 64.0 * s_lo], axis=-1)

    x1, u_last = _pool_prompt(x_prompt, row(norm_mix[0]), pw, row(pool_scale[0]))
    pool_p = u_last[:, None, 16 - POOL_BUF:, :]
    x2 = _mlp(x1.reshape(b * t, d), row(norm_mlp[0]), wu[0], wd[0], row(norm_final), final_norm=False, tm=1024)
    tq = 256
    n_sub_p = t // CMP_STRIDE
    n_sel_p = t // SEL_BLOCK
    tpos = np.arange(t)
    pc_t = jnp.asarray(np.concatenate([np.zeros((t, HEAD_DIM), np.float32), _pos_cols(tpos, HEAD_DIM)], axis=1))
    onehot_t = jnp.asarray(np.eye(n_sel_p, 128, dtype=np.float32)[tpos // SEL_BLOCK], BF16)
    kv_rows, kv_win, q_p, gate_p, ks_aug, vs, kw_aug, vw = _proj(
        x2, row(norm_kv), row(norm_mix[1]), wkv, wqg, bqg, pc_t, onehot_t, b=b, t=t)
    kv_rows_p = kv_rows.reshape(b, t, 2, 2, N_KV_HEADS, HEAD_DIM)
    win_new_p = kv_win.reshape(b, t, 2, N_KV_HEADS, HEAD_DIM)[:, -min(WINDOW, t):]

    dummy_pt = jnp.zeros((b, 1), jnp.int32)
    kc_p, vc_p = _compress(kv_rows.reshape(b, t, ROW_WIDTH), dummy_pt, w1d, w1_bf, pe8, w2_bf, paged=False)
    cache_kv_pages, kc_p = lax.optimization_barrier((cache_kv_pages, kc_p))

    bcast = lambda a: jnp.broadcast_to(a[None, None], (b, N_KV_HEADS) + a.shape)
    pc_c = jnp.asarray(_pos_cols(CMP_STRIDE * np.arange(n_sub_p) + CMP_BLOCK - 1, HEAD_DIM), BF16)
    kc_aug = jnp.concatenate([kc_p.astype(BF16), bcast(pc_c)], axis=-1)
    cmat_t = jnp.asarray(_cmp_to_sel(n_sub_p, n_sub_p - 1, n_sel_p).T, BF16)
    slope_tab = jnp.zeros((N_KV_HEADS, Q_PER_KV, tq, 128), F32).at[:, :, :, HEAD_DIM:HEAD_DIM + 4].set(
        jnp.broadcast_to(slope_cols.reshape(N_KV_HEADS, Q_PER_KV, 1, 4), (N_KV_HEADS, Q_PER_KV, tq, 4)))
    o_p = _attn_prompt(q_p.reshape(b, t, n_q), gate_p.reshape(b, t, N_KV_HEADS * 128), slope_tab, kc_aug,
                       vc_p.astype(BF16), ks_aug, vs, kw_aug, vw, cmat_t, tq=tq)
    x_sample, state_pool, page_table, o_p = lax.optimization_barrier((x_sample, state_pool, page_table, o_p))
    x3, = _lin(o_p.reshape(b * t, n_q), wo, res=x2, tm=512)
    y_prompt = _mlp(x3, row(norm_mlp[1]), wu[1], wd[1], row(norm_final), final_norm=True, tm=1024).reshape(b, t, d)

    xs0 = x_sample.reshape(nb, d)
    xs1, u_s = _pool_sample(xs0, state_pool[:, 0], row(norm_mix[0]), pw, row(pool_scale[0]))
    pool_s = jnp.concatenate([state_pool[:, 0, 1:], u_s[:, None]], axis=1)[:, None]
    xs2 = _mlp(xs1, row(norm_mlp[0]), wu[0], wd[0], row(norm_final), final_norm=False, tm=nb)
    kv_s, = _lin(xs2, wkv, g=row(norm_kv), tm=nb)
    kv_rows_s = kv_s[:, :ROW_WIDTH].reshape(nb, 1, 2, 2, N_KV_HEADS, HEAD_DIM)

    pages = cache_kv_pages.reshape(n_phys, page_rows, ROW_WIDTH // 128, 128).transpose(0, 2, 1, 3)
    kc_s, vc_s = _compress(pages, page_table, w1d, w1_bf, pe8, w2_bf, paged=True)
    n_sub_s = kc_s.shape[2]
    all_heads = lambda a: a.transpose(0, 2, 1, 3).reshape(nb, n_sub_s, KV_WIDTH).astype(BF16)

    q_s, gate_s = _lin(xs2, wqg, g=row(norm_mix[1]), b=bqg, splits=(n_q, N_KV_HEADS * 128), tm=nb)
    q4 = q_s.reshape(nb, N_KV_HEADS, Q_PER_KV, HEAD_DIM) * (HEAD_DIM ** -0.5)
    qbd = jnp.einsum('bgrd,gh->bgrhd', q4, jnp.eye(N_KV_HEADS, dtype=F32)).reshape(nb, N_HEADS, KV_WIDTH)
    g16 = gate_s.reshape(nb, N_KV_HEADS, 128)[:, :, :Q_PER_KV * N_BRANCH].reshape(nb, N_HEADS, N_BRANCH)
    g16 = jnp.pad(g16, ((0, 0), (0, 0), (0, 128 - N_BRANCH)))
    slope16 = jnp.broadcast_to(slopes[:, None], (N_HEADS, 128))
    past_len = page_table.shape[1] * page_rows
    n_blk_pad = 64
    expand = jnp.asarray(np.eye(n_blk_pad, dtype=np.float32)[:, np.arange(past_len) // SEL_BLOCK], BF16)
    cmat_s = jnp.asarray(_cmp_to_sel(n_sub_s, n_sub_s - 1, n_blk_pad), BF16)
    rmat = jnp.asarray(np.kron(np.eye(N_KV_HEADS), np.ones((Q_PER_KV, Q_PER_KV))), BF16)
    o_s, win_new_s = _attn_sample(
        qbd.astype(BF16), g16, slope16, all_heads(kc_s), all_heads(vc_s), pages, page_table,
        kv_s.reshape(nb, 1, 6 * KV_WIDTH), state_win.reshape(nb, state_win.shape[1], 2 * KV_WIDTH),
        expand, cmat_s, rmat)
    o5 = o_s.reshape(nb, N_KV_HEADS, Q_PER_KV, N_KV_HEADS, HEAD_DIM)
    o_heads = jnp.stack([o5[:, gh, :, gh] for gh in range(N_KV_HEADS)], axis=1).reshape(nb, n_q)
    xs3, = _lin(o_heads, wo, res=xs2, tm=nb)
    y_sample = _mlp(xs3, row(norm_mlp[1]), wu[1], wd[1], row(norm_final), final_norm=True, tm=nb).reshape(nb, 1, d)

    return (y_prompt, y_sample, pool_p, pool_s, kv_rows_p, kv_rows_s, win_new_p,
            win_new_s.reshape(state_win.shape))
```

```python
import functools

import numpy as np
import jax
import jax.numpy as jnp
from jax import lax
from jax.experimental import pallas as pl
from jax.experimental.pallas import tpu as pltpu

D_MODEL = 1024
POOL_WINDOWS = (2, 4, 8, 16)
POOL_GROUP_DIM = D_MODEL // len(POOL_WINDOWS)
POOL_BUF = max(POOL_WINDOWS) - 1
N_HEADS = 16
HEAD_DIM = 64
N_KV_HEADS = 4
Q_PER_KV = N_HEADS // N_KV_HEADS
N_BRANCH = 3
CMP_BLOCK = 32
CMP_STRIDE = 16
CMP_HIDDEN = 2 * HEAD_DIM
SEL_BLOCK = 64
N_SELECT = 16
WINDOW = 512
RMS_EPS = 1e-6
FORCED_PRIORITY = 1e6
KV_WIDTH = N_KV_HEADS * HEAD_DIM
ROW_WIDTH = 2 * 2 * KV_WIDTH

NEG = -1e30
F32 = jnp.float32
BF16 = jnp.bfloat16
MIB = 1024 * 1024


def _params(semantics, vmem_mib):
    return pltpu.CompilerParams(dimension_semantics=semantics, vmem_limit_bytes=vmem_mib * MIB)


def _rms(x, g):
    return x * lax.rsqrt(jnp.mean(x * x, axis=-1, keepdims=True) + RMS_EPS) * g


def _dot(a, b):
    return jnp.dot(a, b, preferred_element_type=F32)


def _dot_nt(a, b):
    return lax.dot_general(a, b, (((1,), (1,)), ((), ())), preferred_element_type=F32)


def _split_bf16(x):
    hi = x.astype(BF16)
    lo = (x - hi.astype(F32)).astype(BF16)
    return hi, lo


def _sigmoid(x):
    return 1.0 / (1.0 + jnp.exp(-x))


def _pool_prompt_kernel(x_ref, xp_ref, g_ref, w_ref, sc_ref, y_ref, ul_ref, u_scr, *, tt):
    i = pl.program_id(1)
    g = g_ref[...]
    x = x_ref[0]
    u = _rms(x, g)
    up = _rms(xp_ref[0], g) * (i > 0).astype(F32)
    u_scr[0:16, :] = up
    u_scr[16:16 + tt, :] = u
    t = i * tt + lax.broadcasted_iota(jnp.int32, (tt, 1), 0)
    for gi, win in enumerate(POOL_WINDOWS):
        cols = slice(gi * POOL_GROUP_DIM, (gi + 1) * POOL_GROUP_DIM)
        acc = u[:, cols]
        for k in range(1, win):
            acc = acc + u_scr[16 - k:16 - k + tt, cols]
        cnt = jnp.minimum(t + 1, win).astype(F32)
        diff = acc / cnt - u[:, cols]
        yg = _dot(diff.astype(BF16), w_ref[gi])
        y_ref[0, :, cols] = x[:, cols] + yg * sc_ref[:, cols]
    ul_ref[0] = u[tt - 16:, :]


def _pool_prompt(x, g, w_bf, scale, *, tt=256):
    b, t, d = x.shape
    n_t = t // tt
    kern = functools.partial(_pool_prompt_kernel, tt=tt)
    return pl.pallas_call(
        kern,
        grid=(b, n_t),
        in_specs=[
            pl.BlockSpec((1, tt, d), lambda bi, i: (bi, i, 0)),
            pl.BlockSpec((1, 16, d), lambda bi, i: (bi, jnp.maximum(i * (tt // 16) - 1, 0), 0)),
            pl.BlockSpec((1, d), lambda bi, i: (0, 0)),
            pl.BlockSpec((len(POOL_WINDOWS), POOL_GROUP_DIM, POOL_GROUP_DIM), lambda bi, i: (0, 0, 0)),
            pl.BlockSpec((1, d), lambda bi, i: (0, 0)),
        ],
        out_specs=[
            pl.BlockSpec((1, tt, d), lambda bi, i: (bi, i, 0)),
            pl.BlockSpec((1, 16, d), lambda bi, i: (bi, 0, 0)),
        ],
        out_shape=[jax.ShapeDtypeStruct((b, t, d), F32), jax.ShapeDtypeStruct((b, 16, d), F32)],
        scratch_shapes=[pltpu.VMEM((16 + tt, d), F32)],
        compiler_params=_params(("parallel", "arbitrary"), 32),
        name="pool_prompt",
    )(x, x, g, w_bf, scale)


def _pool_sample_kernel(x_ref, past_ref, g_ref, w_ref, sc_ref, y_ref, u_ref):
    x = x_ref[...]
    u = _rms(x, g_ref[...])
    for gi, win in enumerate(POOL_WINDOWS):
        cols = slice(gi * POOL_GROUP_DIM, (gi + 1) * POOL_GROUP_DIM)
        acc = u[:, cols]
        for k in range(1, win):
            acc = acc + past_ref[:, POOL_BUF - k, cols]
        diff = acc / float(win) - u[:, cols]
        yg = _dot(diff.astype(BF16), w_ref[gi])
        y_ref[:, cols] = x[:, cols] + yg * sc_ref[:, cols]
    u_ref[...] = u


def _pool_sample(x, past, g, w_bf, scale, *, bt=32):
    nb, d = x.shape
    return pl.pallas_call(
        _pool_sample_kernel,
        grid=(nb // bt,),
        in_specs=[
            pl.BlockSpec((bt, d), lambda i: (i, 0)),
            pl.BlockSpec((bt, POOL_BUF, d), lambda i: (i, 0, 0)),
            pl.BlockSpec((1, d), lambda i: (0, 0)),
            pl.BlockSpec((len(POOL_WINDOWS), POOL_GROUP_DIM, POOL_GROUP_DIM), lambda i: (0, 0, 0)),
            pl.BlockSpec((1, d), lambda i: (0, 0)),
        ],
        out_specs=[pl.BlockSpec((bt, d), lambda i: (i, 0)), pl.BlockSpec((bt, d), lambda i: (i, 0))],
        out_shape=[jax.ShapeDtypeStruct((nb, d), F32), jax.ShapeDtypeStruct((nb, d), F32)],
        compiler_params=_params(("parallel",), 32),
        name="pool_sample",
    )(x, past, g, w_bf, scale)


def _mlp_kernel(*refs, final_norm, pre_proj):
    if pre_proj:
        x_ref, a_ref, wo_ref, g_ref, wu_ref, wd_ref, gf_ref, o_ref, xn_scr, acc_scr = refs
    else:
        x_ref, g_ref, wu_ref, wd_ref, gf_ref, o_ref, xn_scr, acc_scr = refs
    j = pl.program_id(1)

    @pl.when(j == 0)
    def _():
        x = x_ref[...]
        if pre_proj:
            x = x + _dot(a_ref[...].astype(BF16), wo_ref[...])
            o_ref[...] = x
        xn_scr[...] = _rms(x, g_ref[...]).astype(BF16)
        acc_scr[...] = jnp.zeros_like(acc_scr)

    h = jnp.maximum(_dot(xn_scr[...], wu_ref[...]), 0.0)
    acc_scr[...] += _dot((h * h).astype(BF16), wd_ref[...])

    @pl.when(j == pl.num_programs(1) - 1)
    def _():
        r = (o_ref[...] if pre_proj else x_ref[...]) + acc_scr[...]
        o_ref[...] = _rms(r, gf_ref[...]) if final_norm else r


def _mlp(x, g, wu_bf, wd_bf, gf, *, final_norm, tm, tf=1024, attn=None, wo_bf=None, vmem_mib=48):
    m, d = x.shape
    f = wu_bf.shape[1]
    pre_proj = attn is not None
    kern = functools.partial(_mlp_kernel, final_norm=final_norm, pre_proj=pre_proj)
    rows = pl.BlockSpec((tm, d), lambda i, j: (i, 0))
    vec = pl.BlockSpec((1, d), lambda i, j: (0, 0))
    pre_specs = [pl.BlockSpec((tm, attn.shape[1]), lambda i, j: (i, 0)),
                 pl.BlockSpec(wo_bf.shape, lambda i, j: (0, 0))] if pre_proj else []
    pre_args = [attn, wo_bf] if pre_proj else []
    return pl.pallas_call(
        kern,
        grid=(m // tm, f // tf),
        in_specs=[rows] + pre_specs + [
            vec,
            pl.BlockSpec((d, tf), lambda i, j: (0, j)),
            pl.BlockSpec((tf, d), lambda i, j: (j, 0)),
            vec,
        ],
        out_specs=rows,
        out_shape=jax.ShapeDtypeStruct((m, d), F32),
        scratch_shapes=[pltpu.VMEM((tm, d), BF16), pltpu.VMEM((tm, d), F32)],
        compiler_params=_params(("parallel", "arbitrary"), vmem_mib),
        name="mlp",
    )(x, *pre_args, g, wu_bf, wd_bf, gf)


def _proj_kernel(x_ref, gkv_ref, gq_ref, wkv_ref, wqg_ref, bqg_ref, pc_ref, oh_ref,
                 rows_ref, win_ref, q_ref, gate_ref, ks_ref, vs_ref, kw_ref, vw_ref):
    x = x_ref[...]
    xh = x * lax.rsqrt(jnp.mean(x * x, axis=-1, keepdims=True) + RMS_EPS)
    hkv = _dot((xh * gkv_ref[...]).astype(BF16), wkv_ref[...])
    hq = _dot((xh * gq_ref[...]).astype(BF16), wqg_ref[...]) + bqg_ref[...]
    n_q = N_HEADS * HEAD_DIM
    rows_ref[...] = hkv[:, :ROW_WIDTH]
    win_ref[...] = hkv[:, ROW_WIDTH:]
    q_ref[...] = hq[:, :n_q]
    gate_ref[...] = hq[:, n_q:]

    lane = lax.broadcasted_iota(jnp.int32, (x.shape[0], 128), 1)
    pc = pc_ref[...]
    ones = (lane == HEAD_DIM).astype(F32)

    def head(col0, gh, filler):
        pair = hkv[:, col0 + (gh // 2) * 128:col0 + (gh // 2 + 1) * 128]
        if gh % 2:
            pair = pltpu.roll(pair, HEAD_DIM, 1)
        return jnp.where(lane < HEAD_DIM, pair, filler).astype(BF16)

    for gh in range(N_KV_HEADS):
        ks_ref[0, gh, :, 0:128] = head(2 * KV_WIDTH, gh, pc)
        ks_ref[0, gh, :, 128:256] = oh_ref[...]
        vs_ref[0, gh] = head(3 * KV_WIDTH, gh, ones)
        kw_ref[0, gh] = head(4 * KV_WIDTH, gh, pc)
        vw_ref[0, gh] = head(5 * KV_WIDTH, gh, ones)


def _proj(x, g_kv, g_q, wkv_bf, wqg_bf, bqg, pos_cols, onehot, *, b, t, tm=512):
    m, d = x.shape
    nt = t // tm
    n_kv = wkv_bf.shape[1]
    n_qg = wqg_bf.shape[1]
    n_q = N_HEADS * HEAD_DIM
    const = lambda *shape: pl.BlockSpec(shape, lambda i: (0,) * len(shape))
    rows_of = lambda n: pl.BlockSpec((tm, n), lambda i: (i, 0))
    heads_of = lambda n: pl.BlockSpec((1, N_KV_HEADS, tm, n), lambda i: (i // nt, 0, i % nt, 0))
    heads_shape = lambda n: jax.ShapeDtypeStruct((b, N_KV_HEADS, t, n), BF16)
    return pl.pallas_call(
        _proj_kernel,
        grid=(m // tm,),
        in_specs=[rows_of(d), const(1, d), const(1, d), const(d, n_kv), const(d, n_qg), const(1, n_qg),
                  pl.BlockSpec((tm, 128), lambda i: (i % nt, 0)), pl.BlockSpec((tm, 128), lambda i: (i % nt, 0))],
        out_specs=[rows_of(ROW_WIDTH), rows_of(n_kv - ROW_WIDTH), rows_of(n_q), rows_of(n_qg - n_q),
                   heads_of(256), heads_of(128), heads_of(128), heads_of(128)],
        out_shape=[jax.ShapeDtypeStruct((m, ROW_WIDTH), F32), jax.ShapeDtypeStruct((m, n_kv - ROW_WIDTH), F32),
                   jax.ShapeDtypeStruct((m, n_q), F32), jax.ShapeDtypeStruct((m, n_qg - n_q), F32),
                   heads_shape(256), heads_shape(128), heads_shape(128), heads_shape(128)],
        compiler_params=_params(("parallel",), 56),
        name="proj",
    )(x, g_kv, g_q, wkv_bf, wqg_bf, bqg, pos_cols, onehot)


def _lin_kernel(*refs, norm, bias, residual, splits):
    refs = list(refs)
    x_ref = refs.pop(0)
    g_ref = refs.pop(0) if norm else None
    w_ref = refs.pop(0)
    b_ref = refs.pop(0) if bias else None
    r_ref = refs.pop(0) if residual else None
    x = x_ref[...]
    if norm:
        x = _rms(x, g_ref[...])
    h = _dot(x.astype(BF16), w_ref[...])
    if bias:
        h = h + b_ref[...]
    if residual:
        h = h + r_ref[...]
    c0 = 0
    for o_ref, n in zip(refs, splits):
        o_ref[...] = h[:, c0:c0 + n]
        c0 += n


def _lin(x, w_bf, *, g=None, b=None, res=None, splits=None, tm):
    m, k = x.shape
    n = w_bf.shape[1]
    splits = splits or (n,)
    kern = functools.partial(_lin_kernel, norm=g is not None, bias=b is not None,
                             residual=res is not None, splits=splits)
    args = [x]
    in_specs = [pl.BlockSpec((tm, k), lambda i: (i, 0))]
    if g is not None:
        args.append(g)
        in_specs.append(pl.BlockSpec((1, k), lambda i: (0, 0)))
    args.append(w_bf)
    in_specs.append(pl.BlockSpec((k, n), lambda i: (0, 0)))
    if b is not None:
        args.append(b)
        in_specs.append(pl.BlockSpec((1, n), lambda i: (0, 0)))
    if res is not None:
        args.append(res)
        in_specs.append(pl.BlockSpec((tm, n), lambda i: (i, 0)))
    outs = pl.pallas_call(
        kern,
        grid=(m // tm,),
        in_specs=in_specs,
        out_specs=[pl.BlockSpec((tm, s), lambda i: (i, 0)) for s in splits],
        out_shape=[jax.ShapeDtypeStruct((m, s), F32) for s in splits],
        compiler_params=_params(("parallel",), 48),
        name="lin",
    )(*args)
    return outs


def _compress_kernel(pt_ref, *refs, n_page_refs, paged, rows_per_ref, n_sub):
    del pt_ref
    page_refs = refs[:n_page_refs]
    w1d_ref, w1_ref, pe_ref, w2_ref, kc_ref, vc_ref = refs[n_page_refs:]
    n_take = rows_per_ref // CMP_STRIDE

    def tap_rows(c, j):
        take = pl.ds(j, n_take, stride=CMP_STRIDE)
        if paged:
            return jnp.concatenate([r[0, c, take, :] for r in page_refs], axis=0)
        return page_refs[c][0, take, :]

    for kv, o_ref in enumerate((kc_ref, vc_ref)):
        pew = _dot(pe_ref[kv].astype(BF16), w1_ref[kv])[0:1]
        for c2 in range(2):
            acc = None
            for jp in range(CMP_STRIDE // 2):
                lhs = jnp.concatenate([tap_rows(2 * kv + c2, 2 * jp), tap_rows(2 * kv + c2, 2 * jp + 1)], axis=1)
                d = _dot(lhs.astype(BF16), w1d_ref[kv, jp])
                acc = d if acc is None else acc + d
            for e in range(2):
                a = acc[:, 2 * e * CMP_HIDDEN:(2 * e + 1) * CMP_HIDDEN]
                b2 = acc[:, (2 * e + 1) * CMP_HIDDEN:(2 * e + 2) * CMP_HIDDEN]
                hid = a + pltpu.roll(b2, a.shape[0] - 1, 0) + pew
                act = hid * _sigmoid(hid)
                out = _dot(act.astype(BF16), w2_ref[kv])
                for k in range(out.shape[0] // n_sub):
                    o_ref[k, 2 * c2 + e] = out[k * n_sub:(k + 1) * n_sub]


def _compress(pages, page_table, w1d_bf, w1_bf, pe8, w2_bf, *, paged):
    if paged:
        nb, n_pages = page_table.shape
        rows_per_ref = pages.shape[2]
        n_sub = n_pages * rows_per_ref // CMP_STRIDE
        seqs = 2
        page_specs = [
            pl.BlockSpec((1, 4, rows_per_ref, 128),
                         functools.partial(lambda k, p, b, pt: (pt[seqs * b + k, p], 0, 0, 0), k, p))
            for k in range(seqs) for p in range(n_pages)]
    else:
        nb, rows_per_ref, _ = pages.shape
        n_sub = rows_per_ref // CMP_STRIDE
        seqs = 1
        page_specs = [pl.BlockSpec((1, rows_per_ref, 128), functools.partial(lambda c, b, pt: (b, 0, c), c))
                      for c in range(4)]
    page_args = [pages] * len(page_specs)
    kern = functools.partial(_compress_kernel, n_page_refs=len(page_args), paged=paged, rows_per_ref=rows_per_ref,
                             n_sub=n_sub)
    const = lambda *shape: pl.BlockSpec(shape, lambda b, pt: (0,) * len(shape))
    out_spec = pl.BlockSpec((seqs, N_KV_HEADS, n_sub, HEAD_DIM), lambda b, pt: (b, 0, 0, 0))
    return pl.pallas_call(
        kern,
        grid_spec=pltpu.PrefetchScalarGridSpec(
            num_scalar_prefetch=1,
            grid=(nb // seqs,),
            in_specs=page_specs + [
                const(2, CMP_STRIDE // 2, 4 * HEAD_DIM, 4 * CMP_HIDDEN),
                const(2, CMP_BLOCK * HEAD_DIM, CMP_HIDDEN),
                const(2, 8, CMP_BLOCK * HEAD_DIM),
                const(2, CMP_HIDDEN, HEAD_DIM),
            ],
            out_specs=[out_spec, out_spec],
        ),
        out_shape=[jax.ShapeDtypeStruct((nb, N_KV_HEADS, n_sub, HEAD_DIM), F32)] * 2,
        compiler_params=_params(("parallel",), 48),
        name="compress",
    )(page_table, *page_args, w1d_bf, w1_bf, pe8, w2_bf)


def _attn_prompt_kernel(q_ref, gt_ref, st_ref, kc_ref, vc_ref, ks_ref, vs_ref, kw_ref, vw_ref, cm_ref, dm_ref,
                        dc_ref, wb_ref, o_ref, qa_scr, rank_scr, ow_scr, m_scr, ala_scr, alb_scr, sa_scr, sb_scr, pa_scr,
                        pb_scr, acc_scr, *, tq, tk, rc, n_sel):
    c0 = pl.program_id(2) * tq
    rows = Q_PER_KV * tq
    band = WINDOW + tq

    lane = lax.broadcasted_iota(jnp.int32, (tq, 128), 1)
    qb = q_ref[0] * (HEAD_DIM ** -0.5)
    for r in range(Q_PER_KV):
        pair = qb[:, (r // 2) * 128:(r // 2 + 1) * 128]
        if r % 2:
            pair = pltpu.roll(pair, HEAD_DIM, 1)
        qa_scr[r * tq:(r + 1) * tq, 0:128] = jnp.where(lane < HEAD_DIM, pair, st_ref[0, r]).astype(BF16)
    ql = qa_scr[:, 0:128]

    n_cmp_pad = kc_ref.shape[2]
    s = _dot_nt(ql, kc_ref[0, 0])
    mask = jnp.concatenate([dc_ref[...]] * Q_PER_KV, axis=0) <= c0
    s = jnp.where(mask, s, NEG)
    m = jnp.max(s, axis=-1, keepdims=True)
    e = jnp.where(mask, jnp.exp(s - m), 0.0)
    p = e / jnp.maximum(jnp.sum(e, axis=-1, keepdims=True), 1e-30)
    o_c = _dot(p.astype(BF16), vc_ref[0, 0])

    w0 = pl.multiple_of(jnp.maximum(c0 - WINDOW, 0), 128)
    sw = _dot_nt(ql, kw_ref[0, 0, pl.ds(w0, band), :]) + jnp.concatenate([wb_ref[0]] * Q_PER_KV, axis=0)
    pw = jnp.exp(sw - jnp.max(sw, axis=-1, keepdims=True))
    aw = _dot(pw.astype(BF16), vw_ref[0, 0, pl.ds(w0, band), :])
    ow_scr[...] = aw[:, :HEAD_DIM] / aw[:, HEAD_DIM:HEAD_DIM + 1]

    psum = p[0:tq] + p[tq:2 * tq] + p[2 * tq:3 * tq] + p[3 * tq:4 * tq]
    hi, lo = _split_bf16(psum)
    imp = _dot_nt(cm_ref[...], hi) + _dot_nt(cm_ref[...], lo)
    blk = lax.broadcasted_iota(jnp.int32, (n_sel, tq), 0)
    tcol = c0 + lax.broadcasted_iota(jnp.int32, (n_sel, tq), 1)
    cur = tcol >> 6
    forced = (blk == 0) | (blk == cur) | (blk == cur - 1)
    valid = blk * SEL_BLOCK <= tcol
    pri = jnp.where(valid, jnp.where(forced, FORCED_PRIORITY, imp), -1.0)
    n_valid = c0 // SEL_BLOCK + tq // SEL_BLOCK
    rank_scr[...] = jnp.zeros_like(rank_scr)
    sub8 = lax.broadcasted_iota(jnp.int32, (8, tq), 0)
    for g8 in range(n_sel // 8):
        lo8, hi8 = 8 * g8, 8 * g8 + 8

        @pl.when((n_valid > N_SELECT) & (lo8 < n_valid))
        def _(lo8=lo8, hi8=hi8):
            mid = pri[lo8:hi8]
            below = above = None
            inside = jnp.zeros((8, tq), jnp.int32)
            for i in range(8):
                row = mid[i:i + 1, :]
                if lo8:
                    b_i = (row > pri[:lo8]).astype(jnp.int32)
                    below = b_i if below is None else below + b_i
                inside = inside + ((row > mid) | ((row == mid) & (sub8 > i))).astype(jnp.int32)
                if hi8 < n_sel:
                    a_i = (row >= pri[hi8:]).astype(jnp.int32)
                    above = a_i if above is None else above + a_i
            parts = [part for part in (below, inside, above) if part is not None]
            rank_scr[...] += jnp.concatenate(parts, axis=0)

    bias = jnp.where(rank_scr[...] < min(N_SELECT, n_sel), 0.0, NEG).T
    right = jnp.concatenate([bias, jnp.zeros((tq, 128 - n_sel), F32)], axis=1).astype(BF16)
    for r in range(Q_PER_KV):
        qa_scr[r * tq:(r + 1) * tq, 128:256] = right

    m_scr[...] = jnp.full_like(m_scr, NEG)
    acc_scr[...] = jnp.zeros_like(acc_scr)

    def scores(j):
        return _dot_nt(qa_scr[...], ks_ref[0, 0, pl.ds(pl.multiple_of(j * tk, tk), tk), :])

    def soft_pv(s_scr, p_scr, al_scr, j, masked):
        k0 = pl.multiple_of(j * tk, tk)
        for i in range(rows // rc):
            rs = slice(i * rc, (i + 1) * rc)
            qs = (i * rc) % tq
            sc = s_scr[rs, :]
            if masked:
                sc = jnp.where(dm_ref[qs:qs + rc, :] <= c0 - k0, sc, NEG)
            m_old = m_scr[rs]
            m_new = jnp.maximum(m_old, jnp.max(sc, axis=-1, keepdims=True))
            al_scr[rs] = jnp.exp(m_old - m_new)
            p_scr[rs] = jnp.exp(sc - m_new).astype(BF16)
            m_scr[rs] = m_new
        acc_scr[...] = al_scr[...] * acc_scr[...] + _dot(p_scr[...], vs_ref[0, 0, pl.ds(k0, tk), :])

    n_tiles = c0 // tk + 1
    n_loop = (n_tiles - 1) // 2
    sa_scr[...] = scores(0)

    def tile_pair(jj, carry):
        sb_scr[...] = scores(2 * jj + 1)
        soft_pv(sa_scr, pa_scr, ala_scr, 2 * jj, False)
        sa_scr[...] = scores(2 * jj + 2)
        soft_pv(sb_scr, pb_scr, alb_scr, 2 * jj + 1, False)
        return carry

    lax.fori_loop(0, n_loop, tile_pair, 0)

    @pl.when(n_tiles % 2 == 0)
    def _():
        sb_scr[...] = scores(2 * n_loop + 1)
        soft_pv(sa_scr, pa_scr, ala_scr, 2 * n_loop, False)
        soft_pv(sb_scr, pb_scr, alb_scr, 2 * n_loop + 1, True)

    @pl.when(n_tiles % 2 == 1)
    def _():
        soft_pv(sa_scr, pa_scr, ala_scr, 2 * n_loop, True)

    o_w = ow_scr[...]
    acc = acc_scr[...]
    o_s = acc[:, :HEAD_DIM] / acc[:, HEAD_DIM:HEAD_DIM + 1]

    gate = _sigmoid(gt_ref[0])
    outs = []
    for r in range(Q_PER_KV):
        rs = slice(r * tq, (r + 1) * tq)
        outs.append(gate[:, 3 * r:3 * r + 1] * o_c[rs] + gate[:, 3 * r + 1:3 * r + 2] * o_s[rs]
                    + gate[:, 3 * r + 2:3 * r + 3] * o_w[rs])
    o_ref[0] = jnp.concatenate(outs, axis=1)


def _attn_prompt(q, gate_pre, slope_tab, kc_aug, vc, ks_aug, vs, kw_aug, vw, cmat_t, *, tq=128, tk=512, rc=64):
    b, t, _ = q.shape
    n_sel = t // SEL_BLOCK
    n_cmp_pad = kc_aug.shape[2]
    kern = functools.partial(_attn_prompt_kernel, tq=tq, tk=tk, rc=rc, n_sel=n_sel)
    rows = Q_PER_KV * tq
    band = WINDOW + tq
    qi = np.arange(tq)[:, None]
    dmat = jnp.asarray(np.arange(tk)[None, :] - qi, jnp.int32)
    dcmp = jnp.asarray(CMP_STRIDE * np.arange(n_cmp_pad)[None, :] + (CMP_BLOCK - 1) - qi, jnp.int32)
    dist = (np.minimum(np.arange(WINDOW // tq + 1) * tq, WINDOW)[:, None, None] + qi[None]
            - np.arange(band)[None, None, :])
    wbias = jnp.asarray(np.where((dist >= 0) & (dist < WINDOW), 0.0, NEG), F32)
    per_bg = lambda *shape: pl.BlockSpec((1, 1) + shape, lambda bi, gi, ci: (bi, gi, 0, 0))
    return pl.pallas_call(
        kern,
        grid=(b, N_KV_HEADS, t // tq),
        in_specs=[
            pl.BlockSpec((1, tq, KV_WIDTH), lambda bi, gi, ci: (bi, ci, gi)),
            pl.BlockSpec((1, tq, 128), lambda bi, gi, ci: (bi, ci, gi)),
            pl.BlockSpec((1, Q_PER_KV, tq, 128), lambda bi, gi, ci: (gi, 0, 0, 0)),
            per_bg(n_cmp_pad, 128),
            per_bg(n_cmp_pad, HEAD_DIM),
            per_bg(t, 256),
            per_bg(t, 128),
            per_bg(t, 128),
            per_bg(t, 128),
            pl.BlockSpec((n_sel, n_cmp_pad), lambda bi, gi, ci: (0, 0)),
            pl.BlockSpec((tq, tk), lambda bi, gi, ci: (0, 0)),
            pl.BlockSpec((tq, n_cmp_pad), lambda bi, gi, ci: (0, 0)),
            pl.BlockSpec((1, tq, band), lambda bi, gi, ci: (jnp.minimum(ci, WINDOW // tq), 0, 0)),
        ],
        out_specs=pl.BlockSpec((1, tq, KV_WIDTH), lambda bi, gi, ci: (bi, ci, gi)),
        out_shape=jax.ShapeDtypeStruct((b, t, N_HEADS * HEAD_DIM), F32),
        scratch_shapes=[
            pltpu.VMEM((rows, 256), BF16),
            pltpu.VMEM((n_sel, tq), jnp.int32),
            pltpu.VMEM((rows, HEAD_DIM), F32),
            pltpu.VMEM((rows, 1), F32),
            pltpu.VMEM((rows, 1), F32),
            pltpu.VMEM((rows, 1), F32),
            pltpu.VMEM((rows, tk), F32),
            pltpu.VMEM((rows, tk), F32),
            pltpu.VMEM((rows, tk), BF16),
            pltpu.VMEM((rows, tk), BF16),
            pltpu.VMEM((rows, 128), F32),
        ],
        compiler_params=_params(("parallel", "parallel", "arbitrary"), 48),
        name="attn_prompt",
    )(q, gate_pre, slope_tab, kc_aug, vc, ks_aug, vs, kw_aug, vw, cmat_t, dmat, dcmp, wbias)


def _softmax_lanes(s):
    e = jnp.exp(s - jnp.max(s, axis=-1, keepdims=True))
    return e / jnp.sum(e, axis=-1, keepdims=True)


def _attn_sample_kernel(pt_ref, *refs, n_pages, past_len, n_sel):
    del pt_ref
    q_ref, gt_ref, sl_ref, kc_ref, vc_ref = refs[:5]
    page_refs = refs[5:5 + n_pages]
    new_ref, win_ref, ex_ref, cm_ref, rm_ref, o_ref, wn_ref = refs[5 + n_pages:]
    q = q_ref[0]
    sl = sl_ref[:, 0:1]
    new = new_ref[0]

    n_cmp_pad = kc_ref.shape[1]
    dist_c = (past_len - (CMP_BLOCK - 1)
              - CMP_STRIDE * lax.broadcasted_iota(jnp.int32, (N_HEADS, n_cmp_pad), 1)).astype(F32)
    s = _dot_nt(q, kc_ref[0]) - sl * dist_c
    mask = dist_c >= 0
    s = jnp.where(mask, s, NEG)
    e = jnp.where(mask, jnp.exp(s - jnp.max(s, axis=-1, keepdims=True)), 0.0)
    p_c = e / jnp.maximum(jnp.sum(e, axis=-1, keepdims=True), 1e-30)
    o_c = _dot(p_c.astype(BF16), vc_ref[0])

    hi, lo = _split_bf16(p_c)
    imp = _dot(hi, cm_ref[...]) + _dot(lo, cm_ref[...])
    hi, lo = _split_bf16(imp)
    imp = _dot(rm_ref[...], hi) + _dot(rm_ref[...], lo)
    n_blk = imp.shape[1]
    blk = lax.broadcasted_iota(jnp.int32, (N_HEADS, n_blk), 1)
    cur = past_len // SEL_BLOCK
    forced = (blk == 0) | (blk == cur) | (blk == cur - 1)
    valid = blk * SEL_BLOCK <= past_len
    pri = jnp.where(valid, jnp.where(forced, FORCED_PRIORITY, imp), -1.0)
    pri = jnp.where(blk < n_sel, pri, -2.0)
    rank = jnp.zeros((N_HEADS, n_blk), jnp.int32)
    for s2 in range(n_sel):
        col = pri[:, s2:s2 + 1]
        beats = (col > pri) | ((col == pri) & (blk > s2))
        rank = rank + beats.astype(jnp.int32)
    bias = jnp.where(rank < min(N_SELECT, n_sel), 0.0, NEG)

    k_sel = jnp.concatenate([jnp.concatenate([r[0, 0], r[0, 1]], axis=1) for r in page_refs],
                            axis=0).astype(BF16)
    v_sel = jnp.concatenate([jnp.concatenate([r[0, 2], r[0, 3]], axis=1) for r in page_refs],
                            axis=0).astype(BF16)
    dist_s = (past_len - lax.broadcasted_iota(jnp.int32, (N_HEADS, past_len), 1)).astype(F32)
    s = _dot_nt(q, k_sel) - sl * dist_s + _dot(bias.astype(BF16), ex_ref[...])
    qf = q.astype(F32)
    k_new = new[:, 2 * KV_WIDTH:3 * KV_WIDTH].astype(BF16).astype(F32)
    v_new = new[:, 3 * KV_WIDTH:4 * KV_WIDTH].astype(BF16).astype(F32)
    s_new = jnp.sum(qf * k_new, axis=-1, keepdims=True) + bias[:, past_len // SEL_BLOCK:past_len // SEL_BLOCK + 1]
    m = jnp.maximum(jnp.max(s, axis=-1, keepdims=True), s_new)
    e = jnp.exp(s - m)
    e_new = jnp.exp(s_new - m)
    l = jnp.sum(e, axis=-1, keepdims=True) + e_new
    o_s = (_dot(e.astype(BF16), v_sel) + e_new.astype(BF16).astype(F32) * v_new) / l

    wb = win_ref.shape[1]
    rolled = pltpu.roll(win_ref[0], wb - 1, 0)
    rowi = lax.broadcasted_iota(jnp.int32, rolled.shape, 0)
    wn = jnp.where(rowi == wb - 1, new[:, 4 * KV_WIDTH:6 * KV_WIDTH], rolled)
    wn_ref[0] = wn
    dist_w = (wb - 1 - lax.broadcasted_iota(jnp.int32, (N_HEADS, wb), 1)).astype(F32)
    p_w = _softmax_lanes(_dot_nt(q, wn[:, 0:KV_WIDTH].astype(BF16)) - sl * dist_w)
    o_w = _dot(p_w.astype(BF16), wn[:, KV_WIDTH:2 * KV_WIDTH].astype(BF16))

    gate = _sigmoid(gt_ref[0])
    o_ref[0] = gate[:, 0:1] * o_c + gate[:, 1:2] * o_s + gate[:, 2:3] * o_w


def _attn_sample(qbd, gate16, slope16, kc_all, vc_all, pages, page_table, new_rows, state_win, expand, cmat, rmat):
    nb, n_pages = page_table.shape
    page_rows = pages.shape[2]
    past_len = n_pages * page_rows
    n_sel = -(-(past_len + 1) // SEL_BLOCK)
    wb = state_win.shape[1]
    n_cmp_pad = kc_all.shape[1]
    kern = functools.partial(_attn_sample_kernel, n_pages=n_pages, past_len=past_len, n_sel=n_sel)
    per_b = lambda *shape: pl.BlockSpec((1,) + shape, lambda b, pt: (b,) + (0,) * len(shape))
    const = lambda *shape: pl.BlockSpec(shape, lambda b, pt: (0,) * len(shape))
    page_specs = [
        pl.BlockSpec((1, 4, page_rows, 128), functools.partial(lambda p, b, pt: (pt[b, p], 1, 0, 0), p))
        for p in range(n_pages)]
    return pl.pallas_call(
        kern,
        grid_spec=pltpu.PrefetchScalarGridSpec(
            num_scalar_prefetch=1,
            grid=(nb,),
            in_specs=[per_b(N_HEADS, KV_WIDTH), per_b(N_HEADS, 128), const(N_HEADS, 128),
                      per_b(n_cmp_pad, KV_WIDTH), per_b(n_cmp_pad, KV_WIDTH)]
            + page_specs
            + [per_b(1, 6 * KV_WIDTH), per_b(wb, 2 * KV_WIDTH), const(*expand.shape), const(*cmat.shape),
               const(*rmat.shape)],
            out_specs=[per_b(N_HEADS, KV_WIDTH), per_b(wb, 2 * KV_WIDTH)],
        ),
        out_shape=[jax.ShapeDtypeStruct((nb, N_HEADS, KV_WIDTH), F32),
                   jax.ShapeDtypeStruct((nb, wb, 2 * KV_WIDTH), F32)],
        compiler_params=_params(("parallel",), 48),
        name="attn_sample",
    )(page_table, qbd, gate16, slope16, kc_all, vc_all, *([pages] * n_pages), new_rows, state_win, expand, cmat,
      rmat)


def _alibi_slopes():
    return jnp.exp2(-8.0 * (jnp.arange(N_HEADS, dtype=F32) + 1.0) / N_HEADS)


def _cmp_to_sel(n_cmp_pad, n_cmp, n_sel_pad):
    m = np.zeros((n_cmp_pad, n_sel_pad), np.float32)
    for n in range(n_cmp):
        for k in range(CMP_BLOCK // CMP_STRIDE):
            m[n, (n + k) * CMP_STRIDE // SEL_BLOCK] += 1.0
    return m


def _pos_cols(pos, width):
    cols = np.zeros((pos.shape[0], width), np.float32)
    cols[:, 0] = cols[:, 1] = pos % 64
    cols[:, 2] = cols[:, 3] = pos // 64
    return cols


def kernel(x_prompt, x_sample, state_pool, cache_kv_pages, state_win, page_table, norm_mix, norm_mlp, w_up, w_down,
           pool_w, pool_scale, norm_kv, w_kv, cmp_pe, cmp_w1, cmp_w2, w_qg, b_gate, w_o, norm_final):
    b, t, d = x_prompt.shape
    nb = x_sample.shape[0]
    n_phys, page_rows = cache_kv_pages.shape[:2]
    n_q = N_HEADS * HEAD_DIM
    row = lambda v: v.reshape(1, -1)

    wu = w_up.astype(BF16)
    wd = w_down.astype(BF16)
    pw = pool_w[0].astype(BF16)
    wkv = w_kv.astype(BF16)
    wo = w_o[0].astype(BF16)
    hh = np.arange(N_HEADS)
    gcols = ((hh // Q_PER_KV) * 128 + (hh % Q_PER_KV) * N_BRANCH)[:, None] + np.arange(N_BRANCH)[None, :]
    gcols = gcols.reshape(-1)
    wg = jnp.zeros((d, N_KV_HEADS * 128), F32).at[:, gcols].set(w_qg[0][:, n_q:])
    wqg = jnp.concatenate([w_qg[0][:, :n_q], wg], axis=1).astype(BF16)
    bqg = jnp.zeros((1, n_q + N_KV_HEADS * 128), F32).at[0, n_q + gcols].set(b_gate[0])
    w1_bf = cmp_w1.astype(BF16)
    w1r = w1_bf.reshape(2, 2, CMP_STRIDE, HEAD_DIM, CMP_HIDDEN)
    w1c = jnp.concatenate([w1r[:, 0], w1r[:, 1]], axis=-1)
    w1d = jnp.concatenate([jnp.concatenate([w1c, jnp.zeros_like(w1c)], axis=-1),
                           jnp.concatenate([jnp.zeros_like(w1c), w1c], axis=-1)], axis=2)
    w1d = w1d.reshape(2, CMP_STRIDE // 2, 4 * HEAD_DIM, 4 * CMP_HIDDEN)
    pe8 = jnp.broadcast_to(cmp_pe.reshape(2, 1, CMP_BLOCK * HEAD_DIM), (2, 8, CMP_BLOCK * HEAD_DIM))
    w2_bf = cmp_w2.astype(BF16)

    slopes = _alibi_slopes()
    s_hi = slopes.astype(BF16).astype(F32)
    s_lo = (slopes - s_hi).astype(BF16).astype(F32)
    slope_cols = jnp.stack([s_hi, s_lo, 64.0 * s_hi, 64.0 * s_lo], axis=-1)

    x1, u_last = _pool_prompt(x_prompt, row(norm_mix[0]), pw, row(pool_scale[0]))
    pool_p = u_last[:, None, 16 - POOL_BUF:, :]
    x2 = _mlp(x1.reshape(b * t, d), row(norm_mlp[0]), wu[0], wd[0], row(norm_final), final_norm=False, tm=1024)
    tq = 256
    n_sub_p = t // CMP_STRIDE
    n_sel_p = t // SEL_BLOCK
    tpos = np.arange(t)
    pc_t = jnp.asarray(np.concatenate([np.zeros((t, HEAD_DIM), np.float32), _pos_cols(tpos, HEAD_DIM)], axis=1))
    onehot_t = jnp.asarray(np.eye(n_sel_p, 128, dtype=np.float32)[tpos // SEL_BLOCK], BF16)
    kv_rows, kv_win, q_p, gate_p, ks_aug, vs, kw_aug, vw = _proj(
        x2, row(norm_kv), row(norm_mix[1]), wkv, wqg, bqg, pc_t, onehot_t, b=b, t=t)
    kv_rows_p = kv_rows.reshape(b, t, 2, 2, N_KV_HEADS, HEAD_DIM)
    win_new_p = kv_win.reshape(b, t, 2, N_KV_HEADS, HEAD_DIM)[:, -min(WINDOW, t):]

    dummy_pt = jnp.zeros((b, 1), jnp.int32)
    kc_p, vc_p = _compress(kv_rows.reshape(b, t, ROW_WIDTH), dummy_pt, w1d, w1_bf, pe8, w2_bf, paged=False)
    cache_kv_pages, kc_p = lax.optimization_barrier((cache_kv_pages, kc_p))

    bcast = lambda a: jnp.broadcast_to(a[None, None], (b, N_KV_HEADS) + a.shape)
    pc_c = jnp.asarray(_pos_cols(CMP_STRIDE * np.arange(n_sub_p) + CMP_BLOCK - 1, HEAD_DIM), BF16)
    kc_aug = jnp.concatenate([kc_p.astype(BF16), bcast(pc_c)], axis=-1)
    cmat_t = jnp.asarray(_cmp_to_sel(n_sub_p, n_sub_p - 1, n_sel_p).T, BF16)
    slope_tab = jnp.zeros((N_KV_HEADS, Q_PER_KV, tq, 128), F32).at[:, :, :, HEAD_DIM:HEAD_DIM + 4].set(
        jnp.broadcast_to(slope_cols.reshape(N_KV_HEADS, Q_PER_KV, 1, 4), (N_KV_HEADS, Q_PER_KV, tq, 4)))
    o_p = _attn_prompt(q_p.reshape(b, t, n_q), gate_p.reshape(b, t, N_KV_HEADS * 128), slope_tab, kc_aug,
                       vc_p.astype(BF16), ks_aug, vs, kw_aug, vw, cmat_t, tq=tq)
    x_sample, state_pool, page_table, o_p = lax.optimization_barrier((x_sample, state_pool, page_table, o_p))
    y_prompt = _mlp(x2, row(norm_mlp[1]), wu[1], wd[1], row(norm_final), final_norm=True, tm=1024, tf=512,
                    attn=o_p.reshape(b * t, n_q), wo_bf=wo, vmem_mib=56).reshape(b, t, d)

    xs0 = x_sample.reshape(nb, d)
    xs1, u_s = _pool_sample(xs0, state_pool[:, 0], row(norm_mix[0]), pw, row(pool_scale[0]))
    pool_s = jnp.concatenate([state_pool[:, 0, 1:], u_s[:, None]], axis=1)[:, None]
    xs2 = _mlp(xs1, row(norm_mlp[0]), wu[0], wd[0], row(norm_final), final_norm=False, tm=nb)
    kv_s, = _lin(xs2, wkv, g=row(norm_kv), tm=nb)
    kv_rows_s = kv_s[:, :ROW_WIDTH].reshape(nb, 1, 2, 2, N_KV_HEADS, HEAD_DIM)

    pages = cache_kv_pages.reshape(n_phys, page_rows, ROW_WIDTH // 128, 128).transpose(0, 2, 1, 3)
    kc_s, vc_s = _compress(pages, page_table, w1d, w1_bf, pe8, w2_bf, paged=True)
    n_sub_s = kc_s.shape[2]
    all_heads = lambda a: a.transpose(0, 2, 1, 3).reshape(nb, n_sub_s, KV_WIDTH).astype(BF16)

    q_s, gate_s = _lin(xs2, wqg, g=row(norm_mix[1]), b=bqg, splits=(n_q, N_KV_HEADS * 128), tm=nb)
    q4 = q_s.reshape(nb, N_KV_HEADS, Q_PER_KV, HEAD_DIM) * (HEAD_DIM ** -0.5)
    qbd = jnp.einsum('bgrd,gh->bgrhd', q4, jnp.eye(N_KV_HEADS, dtype=F32)).reshape(nb, N_HEADS, KV_WIDTH)
    g16 = gate_s.reshape(nb, N_KV_HEADS, 128)[:, :, :Q_PER_KV * N_BRANCH].reshape(nb, N_HEADS, N_BRANCH)
    g16 = jnp.pad(g16, ((0, 0), (0, 0), (0, 128 - N_BRANCH)))
    slope16 = jnp.broadcast_to(slopes[:, None], (N_HEADS, 128))
    past_len = page_table.shape[1] * page_rows
    n_blk_pad = 64
    expand = jnp.asarray(np.eye(n_blk_pad, dtype=np.float32)[:, np.arange(past_len) // SEL_BLOCK], BF16)
    cmat_s = jnp.asarray(_cmp_to_sel(n_sub_s, n_sub_s - 1, n_blk_pad), BF16)
    rmat = jnp.asarray(np.kron(np.eye(N_KV_HEADS), np.ones((Q_PER_KV, Q_PER_KV))), BF16)
    o_s, win_new_s = _attn_sample(
        qbd.astype(BF16), g16, slope16, all_heads(kc_s), all_heads(vc_s), pages, page_table,
        kv_s.reshape(nb, 1, 6 * KV_WIDTH), state_win.reshape(nb, state_win.shape[1], 2 * KV_WIDTH),
        expand, cmat_s, rmat)
    o5 = o_s.reshape(nb, N_KV_HEADS, Q_PER_KV, N_KV_HEADS, HEAD_DIM)
    o_heads = jnp.stack([o5[:, gh, :, gh] for gh in range(N_KV_HEADS)], axis=1).reshape(nb, n_q)
    y_sample = _mlp(xs2, row(norm_mlp[1]), wu[1], wd[1], row(norm_final), final_norm=True, tm=nb,
                    attn=o_heads, wo_bf=wo).reshape(nb, 1, d)

    return (y_prompt, y_sample, pool_p, pool_s, kv_rows_p, kv_rows_s, win_new_p,
            win_new_s.reshape(state_win.shape))
```

```python
import functools

import numpy as np
import jax
import jax.numpy as jnp
from jax import lax
from jax.experimental import pallas as pl
from jax.experimental.pallas import tpu as pltpu

D_MODEL = 1024
POOL_WINDOWS = (2, 4, 8, 16)
POOL_GROUP_DIM = D_MODEL // len(POOL_WINDOWS)
POOL_BUF = max(POOL_WINDOWS) - 1
N_HEADS = 16
HEAD_DIM = 64
N_KV_HEADS = 4
Q_PER_KV = N_HEADS // N_KV_HEADS
N_BRANCH = 3
CMP_BLOCK = 32
CMP_STRIDE = 16
CMP_HIDDEN = 2 * HEAD_DIM
SEL_BLOCK = 64
N_SELECT = 16
WINDOW = 512
RMS_EPS = 1e-6
FORCED_PRIORITY = 1e6
KV_WIDTH = N_KV_HEADS * HEAD_DIM
ROW_WIDTH = 2 * 2 * KV_WIDTH

NEG = -1e30
F32 = jnp.float32
BF16 = jnp.bfloat16
MIB = 1024 * 1024


def _params(semantics, vmem_mib):
    return pltpu.CompilerParams(dimension_semantics=semantics, vmem_limit_bytes=vmem_mib * MIB)


def _rms(x, g):
    return x * lax.rsqrt(jnp.mean(x * x, axis=-1, keepdims=True) + RMS_EPS) * g


def _dot(a, b):
    return jnp.dot(a, b, preferred_element_type=F32)


def _dot_nt(a, b):
    return lax.dot_general(a, b, (((1,), (1,)), ((), ())), preferred_element_type=F32)


def _split_bf16(x):
    hi = x.astype(BF16)
    lo = (x - hi.astype(F32)).astype(BF16)
    return hi, lo


def _sigmoid(x):
    return 1.0 / (1.0 + jnp.exp(-x))


def _pool_prompt_kernel(x_ref, xp_ref, g_ref, w_ref, sc_ref, y_ref, ul_ref, u_scr, *, tt):
    i = pl.program_id(1)
    g = g_ref[...]
    x = x_ref[0]
    u = _rms(x, g)
    up = _rms(xp_ref[0], g) * (i > 0).astype(F32)
    u_scr[0:16, :] = up
    u_scr[16:16 + tt, :] = u
    t = i * tt + lax.broadcasted_iota(jnp.int32, (tt, 1), 0)
    for gi, win in enumerate(POOL_WINDOWS):
        cols = slice(gi * POOL_GROUP_DIM, (gi + 1) * POOL_GROUP_DIM)
        acc = u[:, cols]
        for k in range(1, win):
            acc = acc + u_scr[16 - k:16 - k + tt, cols]
        cnt = jnp.minimum(t + 1, win).astype(F32)
        diff = acc / cnt - u[:, cols]
        yg = _dot(diff.astype(BF16), w_ref[gi])
        y_ref[0, :, cols] = x[:, cols] + yg * sc_ref[:, cols]
    ul_ref[0] = u[tt - 16:, :]


def _pool_prompt(x, g, w_bf, scale, *, tt=256):
    b, t, d = x.shape
    n_t = t // tt
    kern = functools.partial(_pool_prompt_kernel, tt=tt)
    return pl.pallas_call(
        kern,
        grid=(b, n_t),
        in_specs=[
            pl.BlockSpec((1, tt, d), lambda bi, i: (bi, i, 0)),
            pl.BlockSpec((1, 16, d), lambda bi, i: (bi, jnp.maximum(i * (tt // 16) - 1, 0), 0)),
            pl.BlockSpec((1, d), lambda bi, i: (0, 0)),
            pl.BlockSpec((len(POOL_WINDOWS), POOL_GROUP_DIM, POOL_GROUP_DIM), lambda bi, i: (0, 0, 0)),
            pl.BlockSpec((1, d), lambda bi, i: (0, 0)),
        ],
        out_specs=[
            pl.BlockSpec((1, tt, d), lambda bi, i: (bi, i, 0)),
            pl.BlockSpec((1, 16, d), lambda bi, i: (bi, 0, 0)),
        ],
        out_shape=[jax.ShapeDtypeStruct((b, t, d), F32), jax.ShapeDtypeStruct((b, 16, d), F32)],
        scratch_shapes=[pltpu.VMEM((16 + tt, d), F32)],
        compiler_params=_params(("parallel", "arbitrary"), 32),
        name="pool_prompt",
    )(x, x, g, w_bf, scale)


def _pool_sample_kernel(x_ref, past_ref, g_ref, w_ref, sc_ref, y_ref, u_ref):
    x = x_ref[...]
    u = _rms(x, g_ref[...])
    for gi, win in enumerate(POOL_WINDOWS):
        cols = slice(gi * POOL_GROUP_DIM, (gi + 1) * POOL_GROUP_DIM)
        acc = u[:, cols]
        for k in range(1, win):
            acc = acc + past_ref[:, POOL_BUF - k, cols]
        diff = acc / float(win) - u[:, cols]
        yg = _dot(diff.astype(BF16), w_ref[gi])
        y_ref[:, cols] = x[:, cols] + yg * sc_ref[:, cols]
    u_ref[...] = u


def _pool_sample(x, past, g, w_bf, scale, *, bt=32):
    nb, d = x.shape
    return pl.pallas_call(
        _pool_sample_kernel,
        grid=(nb // bt,),
        in_specs=[
            pl.BlockSpec((bt, d), lambda i: (i, 0)),
            pl.BlockSpec((bt, POOL_BUF, d), lambda i: (i, 0, 0)),
            pl.BlockSpec((1, d), lambda i: (0, 0)),
            pl.BlockSpec((len(POOL_WINDOWS), POOL_GROUP_DIM, POOL_GROUP_DIM), lambda i: (0, 0, 0)),
            pl.BlockSpec((1, d), lambda i: (0, 0)),
        ],
        out_specs=[pl.BlockSpec((bt, d), lambda i: (i, 0)), pl.BlockSpec((bt, d), lambda i: (i, 0))],
        out_shape=[jax.ShapeDtypeStruct((nb, d), F32), jax.ShapeDtypeStruct((nb, d), F32)],
        compiler_params=_params(("parallel",), 32),
        name="pool_sample",
    )(x, past, g, w_bf, scale)


def _mlp_kernel(*refs, final_norm, pre_proj):
    if pre_proj:
        x_ref, a_ref, wo_ref, g_ref, wu_ref, wd_ref, gf_ref, o_ref, xn_scr, acc_scr = refs
    else:
        x_ref, g_ref, wu_ref, wd_ref, gf_ref, o_ref, xn_scr, acc_scr = refs
    j = pl.program_id(1)

    @pl.when(j == 0)
    def _():
        x = x_ref[...]
        if pre_proj:
            x = x + _dot(a_ref[...].astype(BF16), wo_ref[...])
            o_ref[...] = x
        xn_scr[...] = _rms(x, g_ref[...]).astype(BF16)
        acc_scr[...] = jnp.zeros_like(acc_scr)

    h = jnp.maximum(_dot(xn_scr[...], wu_ref[...]), 0.0)
    acc_scr[...] += _dot((h * h).astype(BF16), wd_ref[...])

    @pl.when(j == pl.num_programs(1) - 1)
    def _():
        r = (o_ref[...] if pre_proj else x_ref[...]) + acc_scr[...]
        o_ref[...] = _rms(r, gf_ref[...]) if final_norm else r


def _mlp(x, g, wu_bf, wd_bf, gf, *, final_norm, tm, tf=1024, attn=None, wo_bf=None, vmem_mib=48):
    m, d = x.shape
    f = wu_bf.shape[1]
    pre_proj = attn is not None
    kern = functools.partial(_mlp_kernel, final_norm=final_norm, pre_proj=pre_proj)
    rows = pl.BlockSpec((tm, d), lambda i, j: (i, 0))
    vec = pl.BlockSpec((1, d), lambda i, j: (0, 0))
    pre_specs = [pl.BlockSpec((tm, attn.shape[1]), lambda i, j: (i, 0)),
                 pl.BlockSpec(wo_bf.shape, lambda i, j: (0, 0))] if pre_proj else []
    pre_args = [attn, wo_bf] if pre_proj else []
    return pl.pallas_call(
        kern,
        grid=(m // tm, f // tf),
        in_specs=[rows] + pre_specs + [
            vec,
            pl.BlockSpec((d, tf), lambda i, j: (0, j)),
            pl.BlockSpec((tf, d), lambda i, j: (j, 0)),
            vec,
        ],
        out_specs=rows,
        out_shape=jax.ShapeDtypeStruct((m, d), F32),
        scratch_shapes=[pltpu.VMEM((tm, d), BF16), pltpu.VMEM((tm, d), F32)],
        compiler_params=_params(("parallel", "arbitrary"), vmem_mib),
        name="mlp",
    )(x, *pre_args, g, wu_bf, wd_bf, gf)


def _proj_kernel(x_ref, gkv_ref, gq_ref, wkv_ref, wqg_ref, bqg_ref, pc_ref, oh_ref,
                 rows_ref, win_ref, q_ref, gate_ref, ks_ref, vs_ref, kw_ref, vw_ref):
    x = x_ref[...]
    xh = x * lax.rsqrt(jnp.mean(x * x, axis=-1, keepdims=True) + RMS_EPS)
    hkv = _dot((xh * gkv_ref[...]).astype(BF16), wkv_ref[...])
    hq = _dot((xh * gq_ref[...]).astype(BF16), wqg_ref[...]) + bqg_ref[...]
    n_q = N_HEADS * HEAD_DIM
    rows_ref[...] = hkv[:, :ROW_WIDTH]
    win_ref[...] = hkv[:, ROW_WIDTH:]
    q_ref[...] = hq[:, :n_q]
    gate_ref[...] = hq[:, n_q:]

    lane = lax.broadcasted_iota(jnp.int32, (x.shape[0], 128), 1)
    pc = pc_ref[...]
    ones = (lane == HEAD_DIM).astype(F32)

    def head(col0, gh, filler):
        pair = hkv[:, col0 + (gh // 2) * 128:col0 + (gh // 2 + 1) * 128]
        if gh % 2:
            pair = pltpu.roll(pair, HEAD_DIM, 1)
        return jnp.where(lane < HEAD_DIM, pair, filler).astype(BF16)

    for gh in range(N_KV_HEADS):
        ks_ref[0, gh, :, 0:128] = head(2 * KV_WIDTH, gh, pc)
        ks_ref[0, gh, :, 128:256] = oh_ref[...]
        vs_ref[0, gh] = head(3 * KV_WIDTH, gh, ones)
        kw_ref[0, gh] = head(4 * KV_WIDTH, gh, pc)
        vw_ref[0, gh] = head(5 * KV_WIDTH, gh, ones)


def _proj(x, g_kv, g_q, wkv_bf, wqg_bf, bqg, pos_cols, onehot, *, b, t, tm=512):
    m, d = x.shape
    nt = t // tm
    n_kv = wkv_bf.shape[1]
    n_qg = wqg_bf.shape[1]
    n_q = N_HEADS * HEAD_DIM
    const = lambda *shape: pl.BlockSpec(shape, lambda i: (0,) * len(shape))
    rows_of = lambda n: pl.BlockSpec((tm, n), lambda i: (i, 0))
    heads_of = lambda n: pl.BlockSpec((1, N_KV_HEADS, tm, n), lambda i: (i // nt, 0, i % nt, 0))
    heads_shape = lambda n: jax.ShapeDtypeStruct((b, N_KV_HEADS, t, n), BF16)
    return pl.pallas_call(
        _proj_kernel,
        grid=(m // tm,),
        in_specs=[rows_of(d), const(1, d), const(1, d), const(d, n_kv), const(d, n_qg), const(1, n_qg),
                  pl.BlockSpec((tm, 128), lambda i: (i % nt, 0)), pl.BlockSpec((tm, 128), lambda i: (i % nt, 0))],
        out_specs=[rows_of(ROW_WIDTH), rows_of(n_kv - ROW_WIDTH), rows_of(n_q), rows_of(n_qg - n_q),
                   heads_of(256), heads_of(128), heads_of(128), heads_of(128)],
        out_shape=[jax.ShapeDtypeStruct((m, ROW_WIDTH), F32), jax.ShapeDtypeStruct((m, n_kv - ROW_WIDTH), F32),
                   jax.ShapeDtypeStruct((m, n_q), F32), jax.ShapeDtypeStruct((m, n_qg - n_q), F32),
                   heads_shape(256), heads_shape(128), heads_shape(128), heads_shape(128)],
        compiler_params=_params(("parallel",), 56),
        name="proj",
    )(x, g_kv, g_q, wkv_bf, wqg_bf, bqg, pos_cols, onehot)


def _lin_kernel(*refs, norm, bias, residual, splits):
    refs = list(refs)
    x_ref = refs.pop(0)
    g_ref = refs.pop(0) if norm else None
    w_ref = refs.pop(0)
    b_ref = refs.pop(0) if bias else None
    r_ref = refs.pop(0) if residual else None
    x = x_ref[...]
    if norm:
        x = _rms(x, g_ref[...])
    h = _dot(x.astype(BF16), w_ref[...])
    if bias:
        h = h + b_ref[...]
    if residual:
        h = h + r_ref[...]
    c0 = 0
    for o_ref, n in zip(refs, splits):
        o_ref[...] = h[:, c0:c0 + n]
        c0 += n


def _lin(x, w_bf, *, g=None, b=None, res=None, splits=None, tm):
    m, k = x.shape
    n = w_bf.shape[1]
    splits = splits or (n,)
    kern = functools.partial(_lin_kernel, norm=g is not None, bias=b is not None,
                             residual=res is not None, splits=splits)
    args = [x]
    in_specs = [pl.BlockSpec((tm, k), lambda i: (i, 0))]
    if g is not None:
        args.append(g)
        in_specs.append(pl.BlockSpec((1, k), lambda i: (0, 0)))
    args.append(w_bf)
    in_specs.append(pl.BlockSpec((k, n), lambda i: (0, 0)))
    if b is not None:
        args.append(b)
        in_specs.append(pl.BlockSpec((1, n), lambda i: (0, 0)))
    if res is not None:
        args.append(res)
        in_specs.append(pl.BlockSpec((tm, n), lambda i: (i, 0)))
    outs = pl.pallas_call(
        kern,
        grid=(m // tm,),
        in_specs=in_specs,
        out_specs=[pl.BlockSpec((tm, s), lambda i: (i, 0)) for s in splits],
        out_shape=[jax.ShapeDtypeStruct((m, s), F32) for s in splits],
        compiler_params=_params(("parallel",), 48),
        name="lin",
    )(*args)
    return outs


def _compress_kernel(pt_ref, *refs, n_page_refs, paged, rows_per_ref, n_sub):
    del pt_ref
    page_refs = refs[:n_page_refs]
    w1d_ref, w1_ref, pe_ref, w2_ref, kc_ref, vc_ref = refs[n_page_refs:]
    n_take = rows_per_ref // CMP_STRIDE

    def tap_rows(c, j):
        take = pl.ds(j, n_take, stride=CMP_STRIDE)
        if paged:
            return jnp.concatenate([r[0, c, take, :] for r in page_refs], axis=0)
        return page_refs[c][0, take, :]

    for kv, o_ref in enumerate((kc_ref, vc_ref)):
        pew = _dot(pe_ref[kv].astype(BF16), w1_ref[kv])[0:1]
        for c2 in range(2):
            acc = None
            for jp in range(CMP_STRIDE // 2):
                lhs = jnp.concatenate([tap_rows(2 * kv + c2, 2 * jp), tap_rows(2 * kv + c2, 2 * jp + 1)], axis=1)
                d = _dot(lhs.astype(BF16), w1d_ref[kv, jp])
                acc = d if acc is None else acc + d
            for e in range(2):
                a = acc[:, 2 * e * CMP_HIDDEN:(2 * e + 1) * CMP_HIDDEN]
                b2 = acc[:, (2 * e + 1) * CMP_HIDDEN:(2 * e + 2) * CMP_HIDDEN]
                hid = a + pltpu.roll(b2, a.shape[0] - 1, 0) + pew
                act = hid * _sigmoid(hid)
                out = _dot(act.astype(BF16), w2_ref[kv])
                for k in range(out.shape[0] // n_sub):
                    o_ref[k, 2 * c2 + e] = out[k * n_sub:(k + 1) * n_sub]


def _compress(pages, page_table, w1d_bf, w1_bf, pe8, w2_bf, *, paged):
    if paged:
        nb, n_pages = page_table.shape
        rows_per_ref = pages.shape[2]
        n_sub = n_pages * rows_per_ref // CMP_STRIDE
        seqs = 2
        page_specs = [
            pl.BlockSpec((1, 4, rows_per_ref, 128),
                         functools.partial(lambda k, p, b, pt: (pt[seqs * b + k, p], 0, 0, 0), k, p))
            for k in range(seqs) for p in range(n_pages)]
    else:
        nb, rows_per_ref, _ = pages.shape
        n_sub = rows_per_ref // CMP_STRIDE
        seqs = 1
        page_specs = [pl.BlockSpec((1, rows_per_ref, 128), functools.partial(lambda c, b, pt: (b, 0, c), c))
                      for c in range(4)]
    page_args = [pages] * len(page_specs)
    kern = functools.partial(_compress_kernel, n_page_refs=len(page_args), paged=paged, rows_per_ref=rows_per_ref,
                             n_sub=n_sub)
    const = lambda *shape: pl.BlockSpec(shape, lambda b, pt: (0,) * len(shape))
    out_spec = pl.BlockSpec((seqs, N_KV_HEADS, n_sub, HEAD_DIM), lambda b, pt: (b, 0, 0, 0))
    return pl.pallas_call(
        kern,
        grid_spec=pltpu.PrefetchScalarGridSpec(
            num_scalar_prefetch=1,
            grid=(nb // seqs,),
            in_specs=page_specs + [
                const(2, CMP_STRIDE // 2, 4 * HEAD_DIM, 4 * CMP_HIDDEN),
                const(2, CMP_BLOCK * HEAD_DIM, CMP_HIDDEN),
                const(2, 8, CMP_BLOCK * HEAD_DIM),
                const(2, CMP_HIDDEN, HEAD_DIM),
            ],
            out_specs=[out_spec, out_spec],
        ),
        out_shape=[jax.ShapeDtypeStruct((nb, N_KV_HEADS, n_sub, HEAD_DIM), F32)] * 2,
        compiler_params=_params(("parallel",), 48),
        name="compress",
    )(page_table, *page_args, w1d_bf, w1_bf, pe8, w2_bf)


def _attn_prompt_kernel(q_ref, gt_ref, st_ref, kc_ref, vc_ref, ks_ref, vs_ref, kw_ref, vw_ref, cm_ref, dm_ref,
                        dc_ref, wb_ref, o_ref, qa_scr, rank_scr, ow_scr, m_scr, ala_scr, alb_scr, sa_scr, sb_scr, pa_scr,
                        pb_scr, acc_scr, *, tq, tk, rc, n_sel):
    c0 = pl.program_id(2) * tq
    rows = Q_PER_KV * tq
    band = WINDOW + tq

    lane = lax.broadcasted_iota(jnp.int32, (tq, 128), 1)
    qb = q_ref[0] * (HEAD_DIM ** -0.5)
    for r in range(Q_PER_KV):
        pair = qb[:, (r // 2) * 128:(r // 2 + 1) * 128]
        if r % 2:
            pair = pltpu.roll(pair, HEAD_DIM, 1)
        qa_scr[r * tq:(r + 1) * tq, 0:128] = jnp.where(lane < HEAD_DIM, pair, st_ref[0, r]).astype(BF16)
    ql = qa_scr[:, 0:128]

    n_cmp_pad = kc_ref.shape[2]
    s = _dot_nt(ql, kc_ref[0, 0])
    mask = jnp.concatenate([dc_ref[...]] * Q_PER_KV, axis=0) <= c0
    s = jnp.where(mask, s, NEG)
    m = jnp.max(s, axis=-1, keepdims=True)
    e = jnp.where(mask, jnp.exp(s - m), 0.0)
    p = e / jnp.maximum(jnp.sum(e, axis=-1, keepdims=True), 1e-30)
    o_c = _dot(p.astype(BF16), vc_ref[0, 0])

    w0 = pl.multiple_of(jnp.maximum(c0 - WINDOW, 0), 128)
    sw = _dot_nt(ql, kw_ref[0, 0, pl.ds(w0, band), :]) + jnp.concatenate([wb_ref[0]] * Q_PER_KV, axis=0)
    pw = jnp.exp(sw - jnp.max(sw, axis=-1, keepdims=True))
    aw = _dot(pw.astype(BF16), vw_ref[0, 0, pl.ds(w0, band), :])
    ow_scr[...] = aw[:, :HEAD_DIM] / aw[:, HEAD_DIM:HEAD_DIM + 1]

    psum = p[0:tq] + p[tq:2 * tq] + p[2 * tq:3 * tq] + p[3 * tq:4 * tq]
    hi, lo = _split_bf16(psum)
    imp = _dot_nt(cm_ref[...], hi) + _dot_nt(cm_ref[...], lo)
    blk = lax.broadcasted_iota(jnp.int32, (n_sel, tq), 0)
    tcol = c0 + lax.broadcasted_iota(jnp.int32, (n_sel, tq), 1)
    cur = tcol >> 6
    forced = (blk == 0) | (blk == cur) | (blk == cur - 1)
    valid = blk * SEL_BLOCK <= tcol
    pri = jnp.where(valid, jnp.where(forced, FORCED_PRIORITY, imp), -1.0)
    n_valid = c0 // SEL_BLOCK + tq // SEL_BLOCK
    rank_scr[...] = jnp.zeros_like(rank_scr)
    sub8 = lax.broadcasted_iota(jnp.int32, (8, tq), 0)
    for g8 in range(n_sel // 8):
        lo8, hi8 = 8 * g8, 8 * g8 + 8

        @pl.when((n_valid > N_SELECT) & (lo8 < n_valid))
        def _(lo8=lo8, hi8=hi8):
            mid = pri[lo8:hi8]
            below = above = None
            inside = jnp.zeros((8, tq), jnp.int32)
            for i in range(8):
                row = mid[i:i + 1, :]
                if lo8:
                    b_i = (row > pri[:lo8]).astype(jnp.int32)
                    below = b_i if below is None else below + b_i
                inside = inside + ((row > mid) | ((row == mid) & (sub8 > i))).astype(jnp.int32)
                if hi8 < n_sel:
                    a_i = (row >= pri[hi8:]).astype(jnp.int32)
                    above = a_i if above is None else above + a_i
            parts = [part for part in (below, inside, above) if part is not None]
            rank_scr[...] += jnp.concatenate(parts, axis=0)

    bias = jnp.where(rank_scr[...] < min(N_SELECT, n_sel), 0.0, NEG).T
    right = jnp.concatenate([bias, jnp.zeros((tq, 128 - n_sel), F32)], axis=1).astype(BF16)
    for r in range(Q_PER_KV):
        qa_scr[r * tq:(r + 1) * tq, 128:256] = right

    m_scr[...] = jnp.full_like(m_scr, NEG)
    acc_scr[...] = jnp.zeros_like(acc_scr)

    def scores(j):
        return _dot_nt(qa_scr[...], ks_ref[0, 0, pl.ds(pl.multiple_of(j * tk, tk), tk), :])

    def soft_pv(s_scr, p_scr, al_scr, j, masked):
        k0 = pl.multiple_of(j * tk, tk)
        for i in range(rows // rc):
            rs = slice(i * rc, (i + 1) * rc)
            qs = (i * rc) % tq
            sc = s_scr[rs, :]
            if masked:
                sc = jnp.where(dm_ref[qs:qs + rc, :] <= c0 - k0, sc, NEG)
            m_old = m_scr[rs]
            m_new = jnp.maximum(m_old, jnp.max(sc, axis=-1, keepdims=True))
            al_scr[rs] = jnp.exp(m_old - m_new)
            p_scr[rs] = jnp.exp(sc - m_new).astype(BF16)
            m_scr[rs] = m_new
        acc_scr[...] = al_scr[...] * acc_scr[...] + _dot(p_scr[...], vs_ref[0, 0, pl.ds(k0, tk), :])

    n_tiles = c0 // tk + 1
    n_loop = (n_tiles - 1) // 2
    sa_scr[...] = scores(0)

    def tile_pair(jj, carry):
        sb_scr[...] = scores(2 * jj + 1)
        soft_pv(sa_scr, pa_scr, ala_scr, 2 * jj, False)
        sa_scr[...] = scores(2 * jj + 2)
        soft_pv(sb_scr, pb_scr, alb_scr, 2 * jj + 1, False)
        return carry

    lax.fori_loop(0, n_loop, tile_pair, 0)

    @pl.when(n_tiles % 2 == 0)
    def _():
        sb_scr[...] = scores(2 * n_loop + 1)
        soft_pv(sa_scr, pa_scr, ala_scr, 2 * n_loop, False)
        soft_pv(sb_scr, pb_scr, alb_scr, 2 * n_loop + 1, True)

    @pl.when(n_tiles % 2 == 1)
    def _():
        soft_pv(sa_scr, pa_scr, ala_scr, 2 * n_loop, True)

    o_w = ow_scr[...]
    acc = acc_scr[...]
    o_s = acc[:, :HEAD_DIM] / acc[:, HEAD_DIM:HEAD_DIM + 1]

    gate = _sigmoid(gt_ref[0])
    outs = []
    for r in range(Q_PER_KV):
        rs = slice(r * tq, (r + 1) * tq)
        outs.append(gate[:, 3 * r:3 * r + 1] * o_c[rs] + gate[:, 3 * r + 1:3 * r + 2] * o_s[rs]
                    + gate[:, 3 * r + 2:3 * r + 3] * o_w[rs])
    o_ref[0] = jnp.concatenate(outs, axis=1)


def _attn_prompt(q, gate_pre, slope_tab, kc_aug, vc, ks_aug, vs, kw_aug, vw, cmat_t, *, tq=128, tk=512, rc=64):
    b, t, _ = q.shape
    n_sel = t // SEL_BLOCK
    n_cmp_pad = kc_aug.shape[2]
    kern = functools.partial(_attn_prompt_kernel, tq=tq, tk=tk, rc=rc, n_sel=n_sel)
    rows = Q_PER_KV * tq
    band = WINDOW + tq
    qi = np.arange(tq)[:, None]
    dmat = jnp.asarray(np.arange(tk)[None, :] - qi, jnp.int32)
    dcmp = jnp.asarray(CMP_STRIDE * np.arange(n_cmp_pad)[None, :] + (CMP_BLOCK - 1) - qi, jnp.int32)
    dist = (np.minimum(np.arange(WINDOW // tq + 1) * tq, WINDOW)[:, None, None] + qi[None]
            - np.arange(band)[None, None, :])
    wbias = jnp.asarray(np.where((dist >= 0) & (dist < WINDOW), 0.0, NEG), F32)
    per_bg = lambda *shape: pl.BlockSpec((1, 1) + shape, lambda bi, gi, ci: (bi, gi, 0, 0))
    return pl.pallas_call(
        kern,
        grid=(b, N_KV_HEADS, t // tq),
        in_specs=[
            pl.BlockSpec((1, tq, KV_WIDTH), lambda bi, gi, ci: (bi, ci, gi)),
            pl.BlockSpec((1, tq, 128), lambda bi, gi, ci: (bi, ci, gi)),
            pl.BlockSpec((1, Q_PER_KV, tq, 128), lambda bi, gi, ci: (gi, 0, 0, 0)),
            per_bg(n_cmp_pad, 128),
            per_bg(n_cmp_pad, HEAD_DIM),
            per_bg(t, 256),
            per_bg(t, 128),
            per_bg(t, 128),
            per_bg(t, 128),
            pl.BlockSpec((n_sel, n_cmp_pad), lambda bi, gi, ci: (0, 0)),
            pl.BlockSpec((tq, tk), lambda bi, gi, ci: (0, 0)),
            pl.BlockSpec((tq, n_cmp_pad), lambda bi, gi, ci: (0, 0)),
            pl.BlockSpec((1, tq, band), lambda bi, gi, ci: (jnp.minimum(ci, WINDOW // tq), 0, 0)),
        ],
        out_specs=pl.BlockSpec((1, tq, KV_WIDTH), lambda bi, gi, ci: (bi, ci, gi)),
        out_shape=jax.ShapeDtypeStruct((b, t, N_HEADS * HEAD_DIM), F32),
        scratch_shapes=[
            pltpu.VMEM((rows, 256), BF16),
            pltpu.VMEM((n_sel, tq), jnp.int32),
            pltpu.VMEM((rows, HEAD_DIM), F32),
            pltpu.VMEM((rows, 1), F32),
            pltpu.VMEM((rows, 1), F32),
            pltpu.VMEM((rows, 1), F32),
            pltpu.VMEM((rows, tk), F32),
            pltpu.VMEM((rows, tk), F32),
            pltpu.VMEM((rows, tk), BF16),
            pltpu.VMEM((rows, tk), BF16),
            pltpu.VMEM((rows, 128), F32),
        ],
        compiler_params=_params(("parallel", "parallel", "arbitrary"), 48),
        name="attn_prompt",
    )(q, gate_pre, slope_tab, kc_aug, vc, ks_aug, vs, kw_aug, vw, cmat_t, dmat, dcmp, wbias)


def _softmax_lanes(s):
    e = jnp.exp(s - jnp.max(s, axis=-1, keepdims=True))
    return e / jnp.sum(e, axis=-1, keepdims=True)


def _attn_sample_kernel(pt_ref, *refs, n_pages, past_len, n_sel, seqs):
    del pt_ref
    page_refs = refs[5:5 + seqs * n_pages]
    for k in range(seqs):
        _attn_sample_one(k, *refs[:5], page_refs[k * n_pages:(k + 1) * n_pages], *refs[5 + seqs * n_pages:],
                         past_len=past_len, n_sel=n_sel)


def _attn_sample_one(k, q_ref, gt_ref, sl_ref, kc_ref, vc_ref, page_refs, new_ref, win_ref, ex_ref, cm_ref, rm_ref,
                     o_ref, wn_ref, *, past_len, n_sel):
    q = q_ref[k]
    sl = sl_ref[:, 0:1]
    new = new_ref[k]

    n_cmp_pad = kc_ref.shape[1]
    dist_c = (past_len - (CMP_BLOCK - 1)
              - CMP_STRIDE * lax.broadcasted_iota(jnp.int32, (N_HEADS, n_cmp_pad), 1)).astype(F32)
    s = _dot_nt(q, kc_ref[k]) - sl * dist_c
    mask = dist_c >= 0
    s = jnp.where(mask, s, NEG)
    e = jnp.where(mask, jnp.exp(s - jnp.max(s, axis=-1, keepdims=True)), 0.0)
    p_c = e / jnp.maximum(jnp.sum(e, axis=-1, keepdims=True), 1e-30)
    o_c = _dot(p_c.astype(BF16), vc_ref[k])

    hi, lo = _split_bf16(p_c)
    imp = _dot(hi, cm_ref[...]) + _dot(lo, cm_ref[...])
    hi, lo = _split_bf16(imp)
    imp = _dot(rm_ref[...], hi) + _dot(rm_ref[...], lo)
    n_blk = imp.shape[1]
    blk = lax.broadcasted_iota(jnp.int32, (N_HEADS, n_blk), 1)
    cur = past_len // SEL_BLOCK
    forced = (blk == 0) | (blk == cur) | (blk == cur - 1)
    valid = blk * SEL_BLOCK <= past_len
    pri = jnp.where(valid, jnp.where(forced, FORCED_PRIORITY, imp), -1.0)
    pri = jnp.where(blk < n_sel, pri, -2.0)
    rank = jnp.zeros((N_HEADS, n_blk), jnp.int32)
    for s2 in range(n_sel):
        col = pri[:, s2:s2 + 1]
        beats = (col > pri) | ((col == pri) & (blk > s2))
        rank = rank + beats.astype(jnp.int32)
    bias = jnp.where(rank < min(N_SELECT, n_sel), 0.0, NEG)

    k_sel = jnp.concatenate([jnp.concatenate([r[0, 0], r[0, 1]], axis=1) for r in page_refs],
                            axis=0).astype(BF16)
    v_sel = jnp.concatenate([jnp.concatenate([r[0, 2], r[0, 3]], axis=1) for r in page_refs],
                            axis=0).astype(BF16)
    dist_s = (past_len - lax.broadcasted_iota(jnp.int32, (N_HEADS, past_len), 1)).astype(F32)
    s = _dot_nt(q, k_sel) - sl * dist_s + _dot(bias.astype(BF16), ex_ref[...])
    qf = q.astype(F32)
    k_new = new[:, 2 * KV_WIDTH:3 * KV_WIDTH].astype(BF16).astype(F32)
    v_new = new[:, 3 * KV_WIDTH:4 * KV_WIDTH].astype(BF16).astype(F32)
    s_new = jnp.sum(qf * k_new, axis=-1, keepdims=True) + bias[:, past_len // SEL_BLOCK:past_len // SEL_BLOCK + 1]
    m = jnp.maximum(jnp.max(s, axis=-1, keepdims=True), s_new)
    e = jnp.exp(s - m)
    e_new = jnp.exp(s_new - m)
    l = jnp.sum(e, axis=-1, keepdims=True) + e_new
    o_s = (_dot(e.astype(BF16), v_sel) + e_new.astype(BF16).astype(F32) * v_new) / l

    wb = win_ref.shape[1]
    rolled = pltpu.roll(win_ref[k], wb - 1, 0)
    rowi = lax.broadcasted_iota(jnp.int32, rolled.shape, 0)
    wn = jnp.where(rowi == wb - 1, new[:, 4 * KV_WIDTH:6 * KV_WIDTH], rolled)
    wn_ref[k] = wn
    dist_w = (wb - 1 - lax.broadcasted_iota(jnp.int32, (N_HEADS, wb), 1)).astype(F32)
    p_w = _softmax_lanes(_dot_nt(q, wn[:, 0:KV_WIDTH].astype(BF16)) - sl * dist_w)
    o_w = _dot(p_w.astype(BF16), wn[:, KV_WIDTH:2 * KV_WIDTH].astype(BF16))

    gate = _sigmoid(gt_ref[k])
    o_ref[k] = gate[:, 0:1] * o_c + gate[:, 1:2] * o_s + gate[:, 2:3] * o_w


def _attn_sample(qbd, gate16, slope16, kc_all, vc_all, pages, page_table, new_rows, state_win, expand, cmat, rmat):
    nb, n_pages = page_table.shape
    page_rows = pages.shape[2]
    past_len = n_pages * page_rows
    n_sel = -(-(past_len + 1) // SEL_BLOCK)
    wb = state_win.shape[1]
    n_cmp_pad = kc_all.shape[1]
    seqs = 2
    kern = functools.partial(_attn_sample_kernel, n_pages=n_pages, past_len=past_len, n_sel=n_sel, seqs=seqs)
    per_b = lambda *shape: pl.BlockSpec((seqs,) + shape, lambda b, pt: (b,) + (0,) * len(shape))
    const = lambda *shape: pl.BlockSpec(shape, lambda b, pt: (0,) * len(shape))
    page_specs = [
        pl.BlockSpec((1, 4, page_rows, 128),
                     functools.partial(lambda k, p, b, pt: (pt[seqs * b + k, p], 1, 0, 0), k, p))
        for k in range(seqs) for p in range(n_pages)]
    return pl.pallas_call(
        kern,
        grid_spec=pltpu.PrefetchScalarGridSpec(
            num_scalar_prefetch=1,
            grid=(nb // seqs,),
            in_specs=[per_b(N_HEADS, KV_WIDTH), per_b(N_HEADS, 128), const(N_HEADS, 128),
                      per_b(n_cmp_pad, KV_WIDTH), per_b(n_cmp_pad, KV_WIDTH)]
            + page_specs
            + [per_b(1, 6 * KV_WIDTH), per_b(wb, 2 * KV_WIDTH), const(*expand.shape), const(*cmat.shape),
               const(*rmat.shape)],
            out_specs=[per_b(N_HEADS, KV_WIDTH), per_b(wb, 2 * KV_WIDTH)],
        ),
        out_shape=[jax.ShapeDtypeStruct((nb, N_HEADS, KV_WIDTH), F32),
                   jax.ShapeDtypeStruct((nb, wb, 2 * KV_WIDTH), F32)],
        compiler_params=_params(("parallel",), 48),
        name="attn_sample",
    )(page_table, qbd, gate16, slope16, kc_all, vc_all, *([pages] * (seqs * n_pages)), new_rows, state_win, expand,
      cmat, rmat)


def _alibi_slopes():
    return jnp.exp2(-8.0 * (jnp.arange(N_HEADS, dtype=F32) + 1.0) / N_HEADS)


def _cmp_to_sel(n_cmp_pad, n_cmp, n_sel_pad):
    m = np.zeros((n_cmp_pad, n_sel_pad), np.float32)
    for n in range(n_cmp):
        for k in range(CMP_BLOCK // CMP_STRIDE):
            m[n, (n + k) * CMP_STRIDE // SEL_BLOCK] += 1.0
    return m


def _pos_cols(pos, width):
    cols = np.zeros((pos.shape[0], width), np.float32)
    cols[:, 0] = cols[:, 1] = pos % 64
    cols[:, 2] = cols[:, 3] = pos // 64
    return cols


def kernel(x_prompt, x_sample, state_pool, cache_kv_pages, state_win, page_table, norm_mix, norm_mlp, w_up, w_down,
           pool_w, pool_scale, norm_kv, w_kv, cmp_pe, cmp_w1, cmp_w2, w_qg, b_gate, w_o, norm_final):
    b, t, d = x_prompt.shape
    nb = x_sample.shape[0]
    n_phys, page_rows = cache_kv_pages.shape[:2]
    n_q = N_HEADS * HEAD_DIM
    row = lambda v: v.reshape(1, -1)

    wu = w_up.astype(BF16)
    wd = w_down.astype(BF16)
    pw = pool_w[0].astype(BF16)
    wkv = w_kv.astype(BF16)
    wo = w_o[0].astype(BF16)
    hh = np.arange(N_HEADS)
    gcols = ((hh // Q_PER_KV) * 128 + (hh % Q_PER_KV) * N_BRANCH)[:, None] + np.arange(N_BRANCH)[None, :]
    gcols = gcols.reshape(-1)
    wg = jnp.zeros((d, N_KV_HEADS * 128), F32).at[:, gcols].set(w_qg[0][:, n_q:])
    wqg = jnp.concatenate([w_qg[0][:, :n_q], wg], axis=1).astype(BF16)
    bqg = jnp.zeros((1, n_q + N_KV_HEADS * 128), F32).at[0, n_q + gcols].set(b_gate[0])
    w1_bf = cmp_w1.astype(BF16)
    w1r = w1_bf.reshape(2, 2, CMP_STRIDE, HEAD_DIM, CMP_HIDDEN)
    w1c = jnp.concatenate([w1r[:, 0], w1r[:, 1]], axis=-1)
    w1d = jnp.concatenate([jnp.concatenate([w1c, jnp.zeros_like(w1c)], axis=-1),
                           jnp.concatenate([jnp.zeros_like(w1c), w1c], axis=-1)], axis=2)
    w1d = w1d.reshape(2, CMP_STRIDE // 2, 4 * HEAD_DIM, 4 * CMP_HIDDEN)
    pe8 = jnp.broadcast_to(cmp_pe.reshape(2, 1, CMP_BLOCK * HEAD_DIM), (2, 8, CMP_BLOCK * HEAD_DIM))
    w2_bf = cmp_w2.astype(BF16)

    slopes = _alibi_slopes()
    s_hi = slopes.astype(BF16).astype(F32)
    s_lo = (slopes - s_hi).astype(BF16).astype(F32)
    slope_cols = jnp.stack([s_hi, s_lo, 64.0 * s_hi, 64.0 * s_lo], axis=-1)

    x1, u_last = _pool_prompt(x_prompt, row(norm_mix[0]), pw, row(pool_scale[0]))
    pool_p = u_last[:, None, 16 - POOL_BUF:, :]
    x2 = _mlp(x1.reshape(b * t, d), row(norm_mlp[0]), wu[0], wd[0], row(norm_final), final_norm=False, tm=1024)
    tq = 256
    n_sub_p = t // CMP_STRIDE
    n_sel_p = t // SEL_BLOCK
    tpos = np.arange(t)
    pc_t = jnp.asarray(np.concatenate([np.zeros((t, HEAD_DIM), np.float32), _pos_cols(tpos, HEAD_DIM)], axis=1))
    onehot_t = jnp.asarray(np.eye(n_sel_p, 128, dtype=np.float32)[tpos // SEL_BLOCK], BF16)
    kv_rows, kv_win, q_p, gate_p, ks_aug, vs, kw_aug, vw = _proj(
        x2, row(norm_kv), row(norm_mix[1]), wkv, wqg, bqg, pc_t, onehot_t, b=b, t=t)
    kv_rows_p = kv_rows.reshape(b, t, 2, 2, N_KV_HEADS, HEAD_DIM)
    win_new_p = kv_win.reshape(b, t, 2, N_KV_HEADS, HEAD_DIM)[:, -min(WINDOW, t):]

    dummy_pt = jnp.zeros((b, 1), jnp.int32)
    kc_p, vc_p = _compress(kv_rows.reshape(b, t, ROW_WIDTH), dummy_pt, w1d, w1_bf, pe8, w2_bf, paged=False)
    cache_kv_pages, kc_p = lax.optimization_barrier((cache_kv_pages, kc_p))

    bcast = lambda a: jnp.broadcast_to(a[None, None], (b, N_KV_HEADS) + a.shape)
    pc_c = jnp.asarray(_pos_cols(CMP_STRIDE * np.arange(n_sub_p) + CMP_BLOCK - 1, HEAD_DIM), BF16)
    kc_aug = jnp.concatenate([kc_p.astype(BF16), bcast(pc_c)], axis=-1)
    cmat_t = jnp.asarray(_cmp_to_sel(n_sub_p, n_sub_p - 1, n_sel_p).T, BF16)
    slope_tab = jnp.zeros((N_KV_HEADS, Q_PER_KV, tq, 128), F32).at[:, :, :, HEAD_DIM:HEAD_DIM + 4].set(
        jnp.broadcast_to(slope_cols.reshape(N_KV_HEADS, Q_PER_KV, 1, 4), (N_KV_HEADS, Q_PER_KV, tq, 4)))
    o_p = _attn_prompt(q_p.reshape(b, t, n_q), gate_p.reshape(b, t, N_KV_HEADS * 128), slope_tab, kc_aug,
                       vc_p.astype(BF16), ks_aug, vs, kw_aug, vw, cmat_t, tq=tq)
    x_sample, state_pool, page_table, o_p = lax.optimization_barrier((x_sample, state_pool, page_table, o_p))
    y_prompt = _mlp(x2, row(norm_mlp[1]), wu[1], wd[1], row(norm_final), final_norm=True, tm=1024, tf=512,
                    attn=o_p.reshape(b * t, n_q), wo_bf=wo, vmem_mib=56).reshape(b, t, d)

    xs0 = x_sample.reshape(nb, d)
    xs1, u_s = _pool_sample(xs0, state_pool[:, 0], row(norm_mix[0]), pw, row(pool_scale[0]))
    pool_s = jnp.concatenate([state_pool[:, 0, 1:], u_s[:, None]], axis=1)[:, None]
    xs2 = _mlp(xs1, row(norm_mlp[0]), wu[0], wd[0], row(norm_final), final_norm=False, tm=nb)
    kv_s, = _lin(xs2, wkv, g=row(norm_kv), tm=nb)
    kv_rows_s = kv_s[:, :ROW_WIDTH].reshape(nb, 1, 2, 2, N_KV_HEADS, HEAD_DIM)

    pages = cache_kv_pages.reshape(n_phys, page_rows, ROW_WIDTH // 128, 128).transpose(0, 2, 1, 3)
    kc_s, vc_s = _compress(pages, page_table, w1d, w1_bf, pe8, w2_bf, paged=True)
    n_sub_s = kc_s.shape[2]
    all_heads = lambda a: a.transpose(0, 2, 1, 3).reshape(nb, n_sub_s, KV_WIDTH).astype(BF16)

    q_s, gate_s = _lin(xs2, wqg, g=row(norm_mix[1]), b=bqg, splits=(n_q, N_KV_HEADS * 128), tm=nb)
    q4 = q_s.reshape(nb, N_KV_HEADS, Q_PER_KV, HEAD_DIM) * (HEAD_DIM ** -0.5)
    qbd = jnp.einsum('bgrd,gh->bgrhd', q4, jnp.eye(N_KV_HEADS, dtype=F32)).reshape(nb, N_HEADS, KV_WIDTH)
    g16 = gate_s.reshape(nb, N_KV_HEADS, 128)[:, :, :Q_PER_KV * N_BRANCH].reshape(nb, N_HEADS, N_BRANCH)
    g16 = jnp.pad(g16, ((0, 0), (0, 0), (0, 128 - N_BRANCH)))
    slope16 = jnp.broadcast_to(slopes[:, None], (N_HEADS, 128))
    past_len = page_table.shape[1] * page_rows
    n_blk_pad = 64
    expand = jnp.asarray(np.eye(n_blk_pad, dtype=np.float32)[:, np.arange(past_len) // SEL_BLOCK], BF16)
    cmat_s = jnp.asarray(_cmp_to_sel(n_sub_s, n_sub_s - 1, n_blk_pad), BF16)
    rmat = jnp.asarray(np.kron(np.eye(N_KV_HEADS), np.ones((Q_PER_KV, Q_PER_KV))), BF16)
    o_s, win_new_s = _attn_sample(
        qbd.astype(BF16), g16, slope16, all_heads(kc_s), all_heads(vc_s), pages, page_table,
        kv_s.reshape(nb, 1, 6 * KV_WIDTH), state_win.reshape(nb, state_win.shape[1], 2 * KV_WIDTH),
        expand, cmat_s, rmat)
    o5 = o_s.reshape(nb, N_KV_HEADS, Q_PER_KV, N_KV_HEADS, HEAD_DIM)
    o_heads = jnp.stack([o5[:, gh, :, gh] for gh in range(N_KV_HEADS)], axis=1).reshape(nb, n_q)
    y_sample = _mlp(xs2, row(norm_mlp[1]), wu[1], wd[1], row(norm_final), final_norm=True, tm=nb,
                    attn=o_heads, wo_bf=wo).reshape(nb, 1, d)

    return (y_prompt, y_sample, pool_p, pool_s, kv_rows_p, kv_rows_s, win_new_p,
            win_new_s.reshape(state_win.shape))
```

```python
import functools

import numpy as np
import jax
import jax.numpy as jnp
from jax import lax
from jax.experimental import pallas as pl
from jax.experimental.pallas import tpu as pltpu

D_MODEL = 1024
POOL_WINDOWS = (2, 4, 8, 16)
POOL_GROUP_DIM = D_MODEL // len(POOL_WINDOWS)
POOL_BUF = max(POOL_WINDOWS) - 1
N_HEADS = 16
HEAD_DIM = 64
N_KV_HEADS = 4
Q_PER_KV = N_HEADS // N_KV_HEADS
N_BRANCH = 3
CMP_BLOCK = 32
CMP_STRIDE = 16
CMP_HIDDEN = 2 * HEAD_DIM
SEL_BLOCK = 64
N_SELECT = 16
WINDOW = 512
RMS_EPS = 1e-6
FORCED_PRIORITY = 1e6
KV_WIDTH = N_KV_HEADS * HEAD_DIM
ROW_WIDTH = 2 * 2 * KV_WIDTH

NEG = -1e30
F32 = jnp.float32
BF16 = jnp.bfloat16
MIB = 1024 * 1024


def _params(semantics, vmem_mib):
    return pltpu.CompilerParams(dimension_semantics=semantics, vmem_limit_bytes=vmem_mib * MIB)


def _rms(x, g):
    return x * lax.rsqrt(jnp.mean(x * x, axis=-1, keepdims=True) + RMS_EPS) * g


def _dot(a, b):
    return jnp.dot(a, b, preferred_element_type=F32)


def _dot_nt(a, b):
    return lax.dot_general(a, b, (((1,), (1,)), ((), ())), preferred_element_type=F32)


def _split_bf16(x):
    hi = x.astype(BF16)
    lo = (x - hi.astype(F32)).astype(BF16)
    return hi, lo


def _sigmoid(x):
    return 1.0 / (1.0 + jnp.exp(-x))


def _pool_prompt_kernel(x_ref, xp_ref, g_ref, w_ref, sc_ref, y_ref, ul_ref, u_scr, *, tt):
    i = pl.program_id(1)
    g = g_ref[...]
    x = x_ref[0]
    u = _rms(x, g)
    up = _rms(xp_ref[0], g) * (i > 0).astype(F32)
    u_scr[0:16, :] = up
    u_scr[16:16 + tt, :] = u
    t = i * tt + lax.broadcasted_iota(jnp.int32, (tt, 1), 0)
    for gi, win in enumerate(POOL_WINDOWS):
        cols = slice(gi * POOL_GROUP_DIM, (gi + 1) * POOL_GROUP_DIM)
        acc = u[:, cols]
        for k in range(1, win):
            acc = acc + u_scr[16 - k:16 - k + tt, cols]
        cnt = jnp.minimum(t + 1, win).astype(F32)
        diff = acc / cnt - u[:, cols]
        yg = _dot(diff.astype(BF16), w_ref[gi])
        y_ref[0, :, cols] = x[:, cols] + yg * sc_ref[:, cols]
    ul_ref[0] = u[tt - 16:, :]


def _pool_prompt(x, g, w_bf, scale, *, tt=256):
    b, t, d = x.shape
    n_t = t // tt
    kern = functools.partial(_pool_prompt_kernel, tt=tt)
    return pl.pallas_call(
        kern,
        grid=(b, n_t),
        in_specs=[
            pl.BlockSpec((1, tt, d), lambda bi, i: (bi, i, 0)),
            pl.BlockSpec((1, 16, d), lambda bi, i: (bi, jnp.maximum(i * (tt // 16) - 1, 0), 0)),
            pl.BlockSpec((1, d), lambda bi, i: (0, 0)),
            pl.BlockSpec((len(POOL_WINDOWS), POOL_GROUP_DIM, POOL_GROUP_DIM), lambda bi, i: (0, 0, 0)),
            pl.BlockSpec((1, d), lambda bi, i: (0, 0)),
        ],
        out_specs=[
            pl.BlockSpec((1, tt, d), lambda bi, i: (bi, i, 0)),
            pl.BlockSpec((1, 16, d), lambda bi, i: (bi, 0, 0)),
        ],
        out_shape=[jax.ShapeDtypeStruct((b, t, d), F32), jax.ShapeDtypeStruct((b, 16, d), F32)],
        scratch_shapes=[pltpu.VMEM((16 + tt, d), F32)],
        compiler_params=_params(("parallel", "arbitrary"), 32),
        name="pool_prompt",
    )(x, x, g, w_bf, scale)


def _pool_sample_kernel(x_ref, past_ref, g_ref, w_ref, sc_ref, y_ref, u_ref):
    x = x_ref[...]
    u = _rms(x, g_ref[...])
    for gi, win in enumerate(POOL_WINDOWS):
        cols = slice(gi * POOL_GROUP_DIM, (gi + 1) * POOL_GROUP_DIM)
        acc = u[:, cols]
        for k in range(1, win):
            acc = acc + past_ref[:, POOL_BUF - k, cols]
        diff = acc / float(win) - u[:, cols]
        yg = _dot(diff.astype(BF16), w_ref[gi])
        y_ref[:, cols] = x[:, cols] + yg * sc_ref[:, cols]
    u_ref[...] = u


def _pool_sample(x, past, g, w_bf, scale, *, bt=32):
    nb, d = x.shape
    return pl.pallas_call(
        _pool_sample_kernel,
        grid=(nb // bt,),
        in_specs=[
            pl.BlockSpec((bt, d), lambda i: (i, 0)),
            pl.BlockSpec((bt, POOL_BUF, d), lambda i: (i, 0, 0)),
            pl.BlockSpec((1, d), lambda i: (0, 0)),
            pl.BlockSpec((len(POOL_WINDOWS), POOL_GROUP_DIM, POOL_GROUP_DIM), lambda i: (0, 0, 0)),
            pl.BlockSpec((1, d), lambda i: (0, 0)),
        ],
        out_specs=[pl.BlockSpec((bt, d), lambda i: (i, 0)), pl.BlockSpec((bt, d), lambda i: (i, 0))],
        out_shape=[jax.ShapeDtypeStruct((nb, d), F32), jax.ShapeDtypeStruct((nb, d), F32)],
        compiler_params=_params(("parallel",), 32),
        name="pool_sample",
    )(x, past, g, w_bf, scale)


def _mlp_kernel(*refs, final_norm, pre_proj):
    if pre_proj:
        x_ref, a_ref, wo_ref, g_ref, wu_ref, wd_ref, gf_ref, o_ref, xn_scr, acc_scr = refs
    else:
        x_ref, g_ref, wu_ref, wd_ref, gf_ref, o_ref, xn_scr, acc_scr = refs
    j = pl.program_id(1)

    @pl.when(j == 0)
    def _():
        x = x_ref[...]
        if pre_proj:
            x = x + _dot(a_ref[...].astype(BF16), wo_ref[...])
            o_ref[...] = x
        xn_scr[...] = _rms(x, g_ref[...]).astype(BF16)
        acc_scr[...] = jnp.zeros_like(acc_scr)

    h = jnp.maximum(_dot(xn_scr[...], wu_ref[...]), 0.0)
    acc_scr[...] += _dot((h * h).astype(BF16), wd_ref[...])

    @pl.when(j == pl.num_programs(1) - 1)
    def _():
        r = (o_ref[...] if pre_proj else x_ref[...]) + acc_scr[...]
        o_ref[...] = _rms(r, gf_ref[...]) if final_norm else r


def _mlp(x, g, wu_bf, wd_bf, gf, *, final_norm, tm, tf=1024, attn=None, wo_bf=None, vmem_mib=48):
    m, d = x.shape
    f = wu_bf.shape[1]
    pre_proj = attn is not None
    kern = functools.partial(_mlp_kernel, final_norm=final_norm, pre_proj=pre_proj)
    rows = pl.BlockSpec((tm, d), lambda i, j: (i, 0))
    vec = pl.BlockSpec((1, d), lambda i, j: (0, 0))
    pre_specs = [pl.BlockSpec((tm, attn.shape[1]), lambda i, j: (i, 0)),
                 pl.BlockSpec(wo_bf.shape, lambda i, j: (0, 0))] if pre_proj else []
    pre_args = [attn, wo_bf] if pre_proj else []
    return pl.pallas_call(
        kern,
        grid=(m // tm, f // tf),
        in_specs=[rows] + pre_specs + [
            vec,
            pl.BlockSpec((d, tf), lambda i, j: (0, j)),
            pl.BlockSpec((tf, d), lambda i, j: (j, 0)),
            vec,
        ],
        out_specs=rows,
        out_shape=jax.ShapeDtypeStruct((m, d), F32),
        scratch_shapes=[pltpu.VMEM((tm, d), BF16), pltpu.VMEM((tm, d), F32)],
        compiler_params=_params(("parallel", "arbitrary"), vmem_mib),
        name="mlp",
    )(x, *pre_args, g, wu_bf, wd_bf, gf)


def _proj_kernel(x_ref, gkv_ref, gq_ref, wkv_ref, wqg_ref, bqg_ref, pc_ref, oh_ref,
                 rows_ref, win_ref, q_ref, gate_ref, ks_ref, vs_ref, kw_ref, vw_ref):
    x = x_ref[...]
    xh = x * lax.rsqrt(jnp.mean(x * x, axis=-1, keepdims=True) + RMS_EPS)
    hkv = _dot((xh * gkv_ref[...]).astype(BF16), wkv_ref[...])
    hq = _dot((xh * gq_ref[...]).astype(BF16), wqg_ref[...]) + bqg_ref[...]
    n_q = N_HEADS * HEAD_DIM
    rows_ref[...] = hkv[:, :ROW_WIDTH]
    win_ref[...] = hkv[:, ROW_WIDTH:]
    q_ref[...] = hq[:, :n_q]
    gate_ref[...] = hq[:, n_q:]

    lane = lax.broadcasted_iota(jnp.int32, (x.shape[0], 128), 1)
    pc = pc_ref[...]
    ones = (lane == HEAD_DIM).astype(F32)

    def head(col0, gh, filler):
        pair = hkv[:, col0 + (gh // 2) * 128:col0 + (gh // 2 + 1) * 128]
        if gh % 2:
            pair = pltpu.roll(pair, HEAD_DIM, 1)
        return jnp.where(lane < HEAD_DIM, pair, filler).astype(BF16)

    for gh in range(N_KV_HEADS):
        ks_ref[0, gh, :, 0:128] = head(2 * KV_WIDTH, gh, pc)
        ks_ref[0, gh, :, 128:256] = oh_ref[...]
        vs_ref[0, gh] = head(3 * KV_WIDTH, gh, ones)
        kw_ref[0, gh] = head(4 * KV_WIDTH, gh, pc)
        vw_ref[0, gh] = head(5 * KV_WIDTH, gh, ones)


def _proj(x, g_kv, g_q, wkv_bf, wqg_bf, bqg, pos_cols, onehot, *, b, t, tm=512):
    m, d = x.shape
    nt = t // tm
    n_kv = wkv_bf.shape[1]
    n_qg = wqg_bf.shape[1]
    n_q = N_HEADS * HEAD_DIM
    const = lambda *shape: pl.BlockSpec(shape, lambda i: (0,) * len(shape))
    rows_of = lambda n: pl.BlockSpec((tm, n), lambda i: (i, 0))
    heads_of = lambda n: pl.BlockSpec((1, N_KV_HEADS, tm, n), lambda i: (i // nt, 0, i % nt, 0))
    heads_shape = lambda n: jax.ShapeDtypeStruct((b, N_KV_HEADS, t, n), BF16)
    return pl.pallas_call(
        _proj_kernel,
        grid=(m // tm,),
        in_specs=[rows_of(d), const(1, d), const(1, d), const(d, n_kv), const(d, n_qg), const(1, n_qg),
                  pl.BlockSpec((tm, 128), lambda i: (i % nt, 0)), pl.BlockSpec((tm, 128), lambda i: (i % nt, 0))],
        out_specs=[rows_of(ROW_WIDTH), rows_of(n_kv - ROW_WIDTH), rows_of(n_q), rows_of(n_qg - n_q),
                   heads_of(256), heads_of(128), heads_of(128), heads_of(128)],
        out_shape=[jax.ShapeDtypeStruct((m, ROW_WIDTH), F32), jax.ShapeDtypeStruct((m, n_kv - ROW_WIDTH), F32),
                   jax.ShapeDtypeStruct((m, n_q), F32), jax.ShapeDtypeStruct((m, n_qg - n_q), F32),
                   heads_shape(256), heads_shape(128), heads_shape(128), heads_shape(128)],
        compiler_params=_params(("parallel",), 56),
        name="proj",
    )(x, g_kv, g_q, wkv_bf, wqg_bf, bqg, pos_cols, onehot)


def _lin_kernel(*refs, norm, bias, residual, splits):
    refs = list(refs)
    x_ref = refs.pop(0)
    g_ref = refs.pop(0) if norm else None
    w_ref = refs.pop(0)
    b_ref = refs.pop(0) if bias else None
    r_ref = refs.pop(0) if residual else None
    x = x_ref[...]
    if norm:
        x = _rms(x, g_ref[...])
    h = _dot(x.astype(BF16), w_ref[...])
    if bias:
        h = h + b_ref[...]
    if residual:
        h = h + r_ref[...]
    c0 = 0
    for o_ref, n in zip(refs, splits):
        o_ref[...] = h[:, c0:c0 + n]
        c0 += n


def _lin(x, w_bf, *, g=None, b=None, res=None, splits=None, tm):
    m, k = x.shape
    n = w_bf.shape[1]
    splits = splits or (n,)
    kern = functools.partial(_lin_kernel, norm=g is not None, bias=b is not None,
                             residual=res is not None, splits=splits)
    args = [x]
    in_specs = [pl.BlockSpec((tm, k), lambda i: (i, 0))]
    if g is not None:
        args.append(g)
        in_specs.append(pl.BlockSpec((1, k), lambda i: (0, 0)))
    args.append(w_bf)
    in_specs.append(pl.BlockSpec((k, n), lambda i: (0, 0)))
    if b is not None:
        args.append(b)
        in_specs.append(pl.BlockSpec((1, n), lambda i: (0, 0)))
    if res is not None:
        args.append(res)
        in_specs.append(pl.BlockSpec((tm, n), lambda i: (i, 0)))
    outs = pl.pallas_call(
        kern,
        grid=(m // tm,),
        in_specs=in_specs,
        out_specs=[pl.BlockSpec((tm, s), lambda i: (i, 0)) for s in splits],
        out_shape=[jax.ShapeDtypeStruct((m, s), F32) for s in splits],
        compiler_params=_params(("parallel",), 48),
        name="lin",
    )(*args)
    return outs


def _compress_kernel(pt_ref, *refs, n_page_refs, paged, rows_per_ref, n_sub):
    del pt_ref
    page_refs = refs[:n_page_refs]
    w1d_ref, w1_ref, pe_ref, w2_ref, kc_ref, vc_ref = refs[n_page_refs:]
    n_take = rows_per_ref // CMP_STRIDE

    def tap_rows(c, j):
        take = pl.ds(j, n_take, stride=CMP_STRIDE)
        if paged:
            return jnp.concatenate([r[0, c, take, :] for r in page_refs], axis=0)
        return page_refs[c][0, take, :]

    for kv, o_ref in enumerate((kc_ref, vc_ref)):
        pew = _dot(pe_ref[kv].astype(BF16), w1_ref[kv])[0:1]
        for c2 in range(2):
            acc = None
            for jp in range(CMP_STRIDE // 2):
                lhs = jnp.concatenate([tap_rows(2 * kv + c2, 2 * jp), tap_rows(2 * kv + c2, 2 * jp + 1)], axis=1)
                d = _dot(lhs.astype(BF16), w1d_ref[kv, jp])
                acc = d if acc is None else acc + d
            for e in range(2):
                a = acc[:, 2 * e * CMP_HIDDEN:(2 * e + 1) * CMP_HIDDEN]
                b2 = acc[:, (2 * e + 1) * CMP_HIDDEN:(2 * e + 2) * CMP_HIDDEN]
                hid = a + pltpu.roll(b2, a.shape[0] - 1, 0) + pew
                act = hid * _sigmoid(hid)
                out = _dot(act.astype(BF16), w2_ref[kv])
                for k in range(out.shape[0] // n_sub):
                    o_ref[k, 2 * c2 + e] = out[k * n_sub:(k + 1) * n_sub]


def _compress(pages, page_table, w1d_bf, w1_bf, pe8, w2_bf, *, paged):
    if paged:
        nb, n_pages = page_table.shape
        rows_per_ref = pages.shape[2]
        n_sub = n_pages * rows_per_ref // CMP_STRIDE
        seqs = 4
        page_specs = [
            pl.BlockSpec((1, 4, rows_per_ref, 128),
                         functools.partial(lambda k, p, b, pt: (pt[seqs * b + k, p], 0, 0, 0), k, p))
            for k in range(seqs) for p in range(n_pages)]
    else:
        nb, rows_per_ref, _ = pages.shape
        n_sub = rows_per_ref // CMP_STRIDE
        seqs = 1
        page_specs = [pl.BlockSpec((1, rows_per_ref, 128), functools.partial(lambda c, b, pt: (b, 0, c), c))
                      for c in range(4)]
    page_args = [pages] * len(page_specs)
    kern = functools.partial(_compress_kernel, n_page_refs=len(page_args), paged=paged, rows_per_ref=rows_per_ref,
                             n_sub=n_sub)
    const = lambda *shape: pl.BlockSpec(shape, lambda b, pt: (0,) * len(shape))
    out_spec = pl.BlockSpec((seqs, N_KV_HEADS, n_sub, HEAD_DIM), lambda b, pt: (b, 0, 0, 0))
    return pl.pallas_call(
        kern,
        grid_spec=pltpu.PrefetchScalarGridSpec(
            num_scalar_prefetch=1,
            grid=(nb // seqs,),
            in_specs=page_specs + [
                const(2, CMP_STRIDE // 2, 4 * HEAD_DIM, 4 * CMP_HIDDEN),
                const(2, CMP_BLOCK * HEAD_DIM, CMP_HIDDEN),
                const(2, 8, CMP_BLOCK * HEAD_DIM),
                const(2, CMP_HIDDEN, HEAD_DIM),
            ],
            out_specs=[out_spec, out_spec],
        ),
        out_shape=[jax.ShapeDtypeStruct((nb, N_KV_HEADS, n_sub, HEAD_DIM), F32)] * 2,
        compiler_params=_params(("parallel",), 48),
        name="compress",
    )(page_table, *page_args, w1d_bf, w1_bf, pe8, w2_bf)


def _attn_prompt_kernel(q_ref, gt_ref, st_ref, kc_ref, vc_ref, ks_ref, vs_ref, kw_ref, vw_ref, cm_ref, dm_ref,
                        dc_ref, wb_ref, o_ref, qa_scr, rank_scr, ow_scr, m_scr, ala_scr, alb_scr, sa_scr, sb_scr, pa_scr,
                        pb_scr, acc_scr, *, tq, tk, rc, n_sel):
    c0 = pl.program_id(2) * tq
    rows = Q_PER_KV * tq
    band = WINDOW + tq

    lane = lax.broadcasted_iota(jnp.int32, (tq, 128), 1)
    qb = q_ref[0] * (HEAD_DIM ** -0.5)
    for r in range(Q_PER_KV):
        pair = qb[:, (r // 2) * 128:(r // 2 + 1) * 128]
        if r % 2:
            pair = pltpu.roll(pair, HEAD_DIM, 1)
        qa_scr[r * tq:(r + 1) * tq, 0:128] = jnp.where(lane < HEAD_DIM, pair, st_ref[0, r]).astype(BF16)
    ql = qa_scr[:, 0:128]

    n_cmp_pad = kc_ref.shape[2]
    s = _dot_nt(ql, kc_ref[0, 0])
    mask = jnp.concatenate([dc_ref[...]] * Q_PER_KV, axis=0) <= c0
    s = jnp.where(mask, s, NEG)
    m = jnp.max(s, axis=-1, keepdims=True)
    e = jnp.where(mask, jnp.exp(s - m), 0.0)
    p = e / jnp.maximum(jnp.sum(e, axis=-1, keepdims=True), 1e-30)
    o_c = _dot(p.astype(BF16), vc_ref[0, 0])

    w0 = pl.multiple_of(jnp.maximum(c0 - WINDOW, 0), 128)
    sw = _dot_nt(ql, kw_ref[0, 0, pl.ds(w0, band), :]) + jnp.concatenate([wb_ref[0]] * Q_PER_KV, axis=0)
    pw = jnp.exp(sw - jnp.max(sw, axis=-1, keepdims=True))
    aw = _dot(pw.astype(BF16), vw_ref[0, 0, pl.ds(w0, band), :])
    ow_scr[...] = aw[:, :HEAD_DIM] / aw[:, HEAD_DIM:HEAD_DIM + 1]

    psum = p[0:tq] + p[tq:2 * tq] + p[2 * tq:3 * tq] + p[3 * tq:4 * tq]
    hi, lo = _split_bf16(psum)
    imp = _dot_nt(cm_ref[...], hi) + _dot_nt(cm_ref[...], lo)
    blk = lax.broadcasted_iota(jnp.int32, (n_sel, tq), 0)
    tcol = c0 + lax.broadcasted_iota(jnp.int32, (n_sel, tq), 1)
    cur = tcol >> 6
    forced = (blk == 0) | (blk == cur) | (blk == cur - 1)
    valid = blk * SEL_BLOCK <= tcol
    pri = jnp.where(valid, jnp.where(forced, FORCED_PRIORITY, imp), -1.0)
    n_valid = c0 // SEL_BLOCK + tq // SEL_BLOCK
    rank_scr[...] = jnp.zeros_like(rank_scr)
    sub8 = lax.broadcasted_iota(jnp.int32, (8, tq), 0)
    for g8 in range(n_sel // 8):
        lo8, hi8 = 8 * g8, 8 * g8 + 8

        @pl.when((n_valid > N_SELECT) & (lo8 < n_valid))
        def _(lo8=lo8, hi8=hi8):
            mid = pri[lo8:hi8]
            below = above = None
            inside = jnp.zeros((8, tq), jnp.int32)
            for i in range(8):
                row = mid[i:i + 1, :]
                if lo8:
                    b_i = (row > pri[:lo8]).astype(jnp.int32)
                    below = b_i if below is None else below + b_i
                inside = inside + ((row > mid) | ((row == mid) & (sub8 > i))).astype(jnp.int32)
                if hi8 < n_sel:
                    a_i = (row >= pri[hi8:]).astype(jnp.int32)
                    above = a_i if above is None else above + a_i
            parts = [part for part in (below, inside, above) if part is not None]
            rank_scr[...] += jnp.concatenate(parts, axis=0)

    bias = jnp.where(rank_scr[...] < min(N_SELECT, n_sel), 0.0, NEG).T
    right = jnp.concatenate([bias, jnp.zeros((tq, 128 - n_sel), F32)], axis=1).astype(BF16)
    for r in range(Q_PER_KV):
        qa_scr[r * tq:(r + 1) * tq, 128:256] = right

    m_scr[...] = jnp.full_like(m_scr, NEG)
    acc_scr[...] = jnp.zeros_like(acc_scr)

    def scores(j):
        return _dot_nt(qa_scr[...], ks_ref[0, 0, pl.ds(pl.multiple_of(j * tk, tk), tk), :])

    def soft_pv(s_scr, p_scr, al_scr, j, masked):
        k0 = pl.multiple_of(j * tk, tk)
        for i in range(rows // rc):
            rs = slice(i * rc, (i + 1) * rc)
            qs = (i * rc) % tq
            sc = s_scr[rs, :]
            if masked:
                sc = jnp.where(dm_ref[qs:qs + rc, :] <= c0 - k0, sc, NEG)
            m_old = m_scr[rs]
            m_new = jnp.maximum(m_old, jnp.max(sc, axis=-1, keepdims=True))
            al_scr[rs] = jnp.exp(m_old - m_new)
            p_scr[rs] = jnp.exp(sc - m_new).astype(BF16)
            m_scr[rs] = m_new
        acc_scr[...] = al_scr[...] * acc_scr[...] + _dot(p_scr[...], vs_ref[0, 0, pl.ds(k0, tk), :])

    n_tiles = c0 // tk + 1
    n_loop = (n_tiles - 1) // 2
    sa_scr[...] = scores(0)

    def tile_pair(jj, carry):
        sb_scr[...] = scores(2 * jj + 1)
        soft_pv(sa_scr, pa_scr, ala_scr, 2 * jj, False)
        sa_scr[...] = scores(2 * jj + 2)
        soft_pv(sb_scr, pb_scr, alb_scr, 2 * jj + 1, False)
        return carry

    lax.fori_loop(0, n_loop, tile_pair, 0)

    @pl.when(n_tiles % 2 == 0)
    def _():
        sb_scr[...] = scores(2 * n_loop + 1)
        soft_pv(sa_scr, pa_scr, ala_scr, 2 * n_loop, False)
        soft_pv(sb_scr, pb_scr, alb_scr, 2 * n_loop + 1, True)

    @pl.when(n_tiles % 2 == 1)
    def _():
        soft_pv(sa_scr, pa_scr, ala_scr, 2 * n_loop, True)

    o_w = ow_scr[...]
    acc = acc_scr[...]
    o_s = acc[:, :HEAD_DIM] / acc[:, HEAD_DIM:HEAD_DIM + 1]

    gate = _sigmoid(gt_ref[0])
    outs = []
    for r in range(Q_PER_KV):
        rs = slice(r * tq, (r + 1) * tq)
        outs.append(gate[:, 3 * r:3 * r + 1] * o_c[rs] + gate[:, 3 * r + 1:3 * r + 2] * o_s[rs]
                    + gate[:, 3 * r + 2:3 * r + 3] * o_w[rs])
    o_ref[0] = jnp.concatenate(outs, axis=1)


def _attn_prompt(q, gate_pre, slope_tab, kc_aug, vc, ks_aug, vs, kw_aug, vw, cmat_t, *, tq=128, tk=512, rc=64):
    b, t, _ = q.shape
    n_sel = t // SEL_BLOCK
    n_cmp_pad = kc_aug.shape[2]
    kern = functools.partial(_attn_prompt_kernel, tq=tq, tk=tk, rc=rc, n_sel=n_sel)
    rows = Q_PER_KV * tq
    band = WINDOW + tq
    qi = np.arange(tq)[:, None]
    dmat = jnp.asarray(np.arange(tk)[None, :] - qi, jnp.int32)
    dcmp = jnp.asarray(CMP_STRIDE * np.arange(n_cmp_pad)[None, :] + (CMP_BLOCK - 1) - qi, jnp.int32)
    dist = (np.minimum(np.arange(WINDOW // tq + 1) * tq, WINDOW)[:, None, None] + qi[None]
            - np.arange(band)[None, None, :])
    wbias = jnp.asarray(np.where((dist >= 0) & (dist < WINDOW), 0.0, NEG), F32)
    per_bg = lambda *shape: pl.BlockSpec((1, 1) + shape, lambda bi, gi, ci: (bi, gi, 0, 0))
    return pl.pallas_call(
        kern,
        grid=(b, N_KV_HEADS, t // tq),
        in_specs=[
            pl.BlockSpec((1, tq, KV_WIDTH), lambda bi, gi, ci: (bi, ci, gi)),
            pl.BlockSpec((1, tq, 128), lambda bi, gi, ci: (bi, ci, gi)),
            pl.BlockSpec((1, Q_PER_KV, tq, 128), lambda bi, gi, ci: (gi, 0, 0, 0)),
            per_bg(n_cmp_pad, 128),
            per_bg(n_cmp_pad, HEAD_DIM),
            per_bg(t, 256),
            per_bg(t, 128),
            per_bg(t, 128),
            per_bg(t, 128),
            pl.BlockSpec((n_sel, n_cmp_pad), lambda bi, gi, ci: (0, 0)),
            pl.BlockSpec((tq, tk), lambda bi, gi, ci: (0, 0)),
            pl.BlockSpec((tq, n_cmp_pad), lambda bi, gi, ci: (0, 0)),
            pl.BlockSpec((1, tq, band), lambda bi, gi, ci: (jnp.minimum(ci, WINDOW // tq), 0, 0)),
        ],
        out_specs=pl.BlockSpec((1, tq, KV_WIDTH), lambda bi, gi, ci: (bi, ci, gi)),
        out_shape=jax.ShapeDtypeStruct((b, t, N_HEADS * HEAD_DIM), F32),
        scratch_shapes=[
            pltpu.VMEM((rows, 256), BF16),
            pltpu.VMEM((n_sel, tq), jnp.int32),
            pltpu.VMEM((rows, HEAD_DIM), F32),
            pltpu.VMEM((rows, 1), F32),
            pltpu.VMEM((rows, 1), F32),
            pltpu.VMEM((rows, 1), F32),
            pltpu.VMEM((rows, tk), F32),
            pltpu.VMEM((rows, tk), F32),
            pltpu.VMEM((rows, tk), BF16),
            pltpu.VMEM((rows, tk), BF16),
            pltpu.VMEM((rows, 128), F32),
        ],
        compiler_params=_params(("parallel", "parallel", "arbitrary"), 48),
        name="attn_prompt",
    )(q, gate_pre, slope_tab, kc_aug, vc, ks_aug, vs, kw_aug, vw, cmat_t, dmat, dcmp, wbias)


def _softmax_lanes(s):
    e = jnp.exp(s - jnp.max(s, axis=-1, keepdims=True))
    return e / jnp.sum(e, axis=-1, keepdims=True)


def _attn_sample_kernel(pt_ref, *refs, n_pages, past_len, n_sel, seqs):
    del pt_ref
    page_refs = refs[5:5 + seqs * n_pages]
    for k in range(seqs):
        _attn_sample_one(k, *refs[:5], page_refs[k * n_pages:(k + 1) * n_pages], *refs[5 + seqs * n_pages:],
                         past_len=past_len, n_sel=n_sel)


def _attn_sample_one(k, q_ref, gt_ref, sl_ref, kc_ref, vc_ref, page_refs, new_ref, win_ref, ex_ref, cm_ref, rm_ref,
                     o_ref, wn_ref, *, past_len, n_sel):
    q = q_ref[k]
    sl = sl_ref[:, 0:1]
    new = new_ref[k]

    n_cmp_pad = kc_ref.shape[1]
    dist_c = (past_len - (CMP_BLOCK - 1)
              - CMP_STRIDE * lax.broadcasted_iota(jnp.int32, (N_HEADS, n_cmp_pad), 1)).astype(F32)
    s = _dot_nt(q, kc_ref[k]) - sl * dist_c
    mask = dist_c >= 0
    s = jnp.where(mask, s, NEG)
    e = jnp.where(mask, jnp.exp(s - jnp.max(s, axis=-1, keepdims=True)), 0.0)
    p_c = e / jnp.maximum(jnp.sum(e, axis=-1, keepdims=True), 1e-30)
    o_c = _dot(p_c.astype(BF16), vc_ref[k])

    hi, lo = _split_bf16(p_c)
    imp = _dot(hi, cm_ref[...]) + _dot(lo, cm_ref[...])
    hi, lo = _split_bf16(imp)
    imp = _dot(rm_ref[...], hi) + _dot(rm_ref[...], lo)
    n_blk = imp.shape[1]
    blk = lax.broadcasted_iota(jnp.int32, (N_HEADS, n_blk), 1)
    cur = past_len // SEL_BLOCK
    forced = (blk == 0) | (blk == cur) | (blk == cur - 1)
    valid = blk * SEL_BLOCK <= past_len
    pri = jnp.where(valid, jnp.where(forced, FORCED_PRIORITY, imp), -1.0)
    pri = jnp.where(blk < n_sel, pri, -2.0)
    rank = jnp.zeros((N_HEADS, n_blk), jnp.int32)
    for s2 in range(n_sel):
        col = pri[:, s2:s2 + 1]
        beats = (col > pri) | ((col == pri) & (blk > s2))
        rank = rank + beats.astype(jnp.int32)
    bias = jnp.where(rank < min(N_SELECT, n_sel), 0.0, NEG)

    k_sel = jnp.concatenate([jnp.concatenate([r[0, 0], r[0, 1]], axis=1) for r in page_refs],
                            axis=0).astype(BF16)
    v_sel = jnp.concatenate([jnp.concatenate([r[0, 2], r[0, 3]], axis=1) for r in page_refs],
                            axis=0).astype(BF16)
    dist_s = (past_len - lax.broadcasted_iota(jnp.int32, (N_HEADS, past_len), 1)).astype(F32)
    s = _dot_nt(q, k_sel) - sl * dist_s + _dot(bias.astype(BF16), ex_ref[...])
    qf = q.astype(F32)
    k_new = new[:, 2 * KV_WIDTH:3 * KV_WIDTH].astype(BF16).astype(F32)
    v_new = new[:, 3 * KV_WIDTH:4 * KV_WIDTH].astype(BF16).astype(F32)
    s_new = jnp.sum(qf * k_new, axis=-1, keepdims=True) + bias[:, past_len // SEL_BLOCK:past_len // SEL_BLOCK + 1]
    m = jnp.maximum(jnp.max(s, axis=-1, keepdims=True), s_new)
    e = jnp.exp(s - m)
    e_new = jnp.exp(s_new - m)
    l = jnp.sum(e, axis=-1, keepdims=True) + e_new
    o_s = (_dot(e.astype(BF16), v_sel) + e_new.astype(BF16).astype(F32) * v_new) / l

    wb = win_ref.shape[1]
    rolled = pltpu.roll(win_ref[k], wb - 1, 0)
    rowi = lax.broadcasted_iota(jnp.int32, rolled.shape, 0)
    wn = jnp.where(rowi == wb - 1, new[:, 4 * KV_WIDTH:6 * KV_WIDTH], rolled)
    wn_ref[k] = wn
    dist_w = (wb - 1 - lax.broadcasted_iota(jnp.int32, (N_HEADS, wb), 1)).astype(F32)
    p_w = _softmax_lanes(_dot_nt(q, wn[:, 0:KV_WIDTH].astype(BF16)) - sl * dist_w)
    o_w = _dot(p_w.astype(BF16), wn[:, KV_WIDTH:2 * KV_WIDTH].astype(BF16))

    gate = _sigmoid(gt_ref[k])
    o_ref[k] = gate[:, 0:1] * o_c + gate[:, 1:2] * o_s + gate[:, 2:3] * o_w


def _attn_sample(qbd, gate16, slope16, kc_all, vc_all, pages, page_table, new_rows, state_win, expand, cmat, rmat):
    nb, n_pages = page_table.shape
    page_rows = pages.shape[2]
    past_len = n_pages * page_rows
    n_sel = -(-(past_len + 1) // SEL_BLOCK)
    wb = state_win.shape[1]
    n_cmp_pad = kc_all.shape[1]
    seqs = 2
    kern = functools.partial(_attn_sample_kernel, n_pages=n_pages, past_len=past_len, n_sel=n_sel, seqs=seqs)
    per_b = lambda *shape: pl.BlockSpec((seqs,) + shape, lambda b, pt: (b,) + (0,) * len(shape))
    const = lambda *shape: pl.BlockSpec(shape, lambda b, pt: (0,) * len(shape))
    page_specs = [
        pl.BlockSpec((1, 4, page_rows, 128),
                     functools.partial(lambda k, p, b, pt: (pt[seqs * b + k, p], 1, 0, 0), k, p))
        for k in range(seqs) for p in range(n_pages)]
    return pl.pallas_call(
        kern,
        grid_spec=pltpu.PrefetchScalarGridSpec(
            num_scalar_prefetch=1,
            grid=(nb // seqs,),
            in_specs=[per_b(N_HEADS, KV_WIDTH), per_b(N_HEADS, 128), const(N_HEADS, 128),
                      per_b(n_cmp_pad, KV_WIDTH), per_b(n_cmp_pad, KV_WIDTH)]
            + page_specs
            + [per_b(1, 6 * KV_WIDTH), per_b(wb, 2 * KV_WIDTH), const(*expand.shape), const(*cmat.shape),
               const(*rmat.shape)],
            out_specs=[per_b(N_HEADS, KV_WIDTH), per_b(wb, 2 * KV_WIDTH)],
        ),
        out_shape=[jax.ShapeDtypeStruct((nb, N_HEADS, KV_WIDTH), F32),
                   jax.ShapeDtypeStruct((nb, wb, 2 * KV_WIDTH), F32)],
        compiler_params=_params(("parallel",), 48),
        name="attn_sample",
    )(page_table, qbd, gate16, slope16, kc_all, vc_all, *([pages] * (seqs * n_pages)), new_rows, state_win, expand,
      cmat, rmat)


def _alibi_slopes():
    return jnp.exp2(-8.0 * (jnp.arange(N_HEADS, dtype=F32) + 1.0) / N_HEADS)


def _cmp_to_sel(n_cmp_pad, n_cmp, n_sel_pad):
    m = np.zeros((n_cmp_pad, n_sel_pad), np.float32)
    for n in range(n_cmp):
        for k in range(CMP_BLOCK // CMP_STRIDE):
            m[n, (n + k) * CMP_STRIDE // SEL_BLOCK] += 1.0
    return m


def _pos_cols(pos, width):
    cols = np.zeros((pos.shape[0], width), np.float32)
    cols[:, 0] = cols[:, 1] = pos % 64
    cols[:, 2] = cols[:, 3] = pos // 64
    return cols


def kernel(x_prompt, x_sample, state_pool, cache_kv_pages, state_win, page_table, norm_mix, norm_mlp, w_up, w_down,
           pool_w, pool_scale, norm_kv, w_kv, cmp_pe, cmp_w1, cmp_w2, w_qg, b_gate, w_o, norm_final):
    b, t, d = x_prompt.shape
    nb = x_sample.shape[0]
    n_phys, page_rows = cache_kv_pages.shape[:2]
    n_q = N_HEADS * HEAD_DIM
    row = lambda v: v.reshape(1, -1)

    wu = w_up.astype(BF16)
    wd = w_down.astype(BF16)
    pw = pool_w[0].astype(BF16)
    wkv = w_kv.astype(BF16)
    wo = w_o[0].astype(BF16)
    hh = np.arange(N_HEADS)
    gcols = ((hh // Q_PER_KV) * 128 + (hh % Q_PER_KV) * N_BRANCH)[:, None] + np.arange(N_BRANCH)[None, :]
    gcols = gcols.reshape(-1)
    wg = jnp.zeros((d, N_KV_HEADS * 128), F32).at[:, gcols].set(w_qg[0][:, n_q:])
    wqg = jnp.concatenate([w_qg[0][:, :n_q], wg], axis=1).astype(BF16)
    bqg = jnp.zeros((1, n_q + N_KV_HEADS * 128), F32).at[0, n_q + gcols].set(b_gate[0])
    w1_bf = cmp_w1.astype(BF16)
    w1r = w1_bf.reshape(2, 2, CMP_STRIDE, HEAD_DIM, CMP_HIDDEN)
    w1c = jnp.concatenate([w1r[:, 0], w1r[:, 1]], axis=-1)
    w1d = jnp.concatenate([jnp.concatenate([w1c, jnp.zeros_like(w1c)], axis=-1),
                           jnp.concatenate([jnp.zeros_like(w1c), w1c], axis=-1)], axis=2)
    w1d = w1d.reshape(2, CMP_STRIDE // 2, 4 * HEAD_DIM, 4 * CMP_HIDDEN)
    pe8 = jnp.broadcast_to(cmp_pe.reshape(2, 1, CMP_BLOCK * HEAD_DIM), (2, 8, CMP_BLOCK * HEAD_DIM))
    w2_bf = cmp_w2.astype(BF16)

    slopes = _alibi_slopes()
    s_hi = slopes.astype(BF16).astype(F32)
    s_lo = (slopes - s_hi).astype(BF16).astype(F32)
    slope_cols = jnp.stack([s_hi, s_lo, 64.0 * s_hi, 64.0 * s_lo], axis=-1)

    x1, u_last = _pool_prompt(x_prompt, row(norm_mix[0]), pw, row(pool_scale[0]))
    pool_p = u_last[:, None, 16 - POOL_BUF:, :]
    x2 = _mlp(x1.reshape(b * t, d), row(norm_mlp[0]), wu[0], wd[0], row(norm_final), final_norm=False, tm=1024)
    tq = 256
    n_sub_p = t // CMP_STRIDE
    n_sel_p = t // SEL_BLOCK
    tpos = np.arange(t)
    pc_t = jnp.asarray(np.concatenate([np.zeros((t, HEAD_DIM), np.float32), _pos_cols(tpos, HEAD_DIM)], axis=1))
    onehot_t = jnp.asarray(np.eye(n_sel_p, 128, dtype=np.float32)[tpos // SEL_BLOCK], BF16)
    kv_rows, kv_win, q_p, gate_p, ks_aug, vs, kw_aug, vw = _proj(
        x2, row(norm_kv), row(norm_mix[1]), wkv, wqg, bqg, pc_t, onehot_t, b=b, t=t)
    kv_rows_p = kv_rows.reshape(b, t, 2, 2, N_KV_HEADS, HEAD_DIM)
    win_new_p = kv_win.reshape(b, t, 2, N_KV_HEADS, HEAD_DIM)[:, -min(WINDOW, t):]

    dummy_pt = jnp.zeros((b, 1), jnp.int32)
    kc_p, vc_p = _compress(kv_rows.reshape(b, t, ROW_WIDTH), dummy_pt, w1d, w1_bf, pe8, w2_bf, paged=False)
    cache_kv_pages, kc_p = lax.optimization_barrier((cache_kv_pages, kc_p))

    bcast = lambda a: jnp.broadcast_to(a[None, None], (b, N_KV_HEADS) + a.shape)
    pc_c = jnp.asarray(_pos_cols(CMP_STRIDE * np.arange(n_sub_p) + CMP_BLOCK - 1, HEAD_DIM), BF16)
    kc_aug = jnp.concatenate([kc_p.astype(BF16), bcast(pc_c)], axis=-1)
    cmat_t = jnp.asarray(_cmp_to_sel(n_sub_p, n_sub_p - 1, n_sel_p).T, BF16)
    slope_tab = jnp.zeros((N_KV_HEADS, Q_PER_KV, tq, 128), F32).at[:, :, :, HEAD_DIM:HEAD_DIM + 4].set(
        jnp.broadcast_to(slope_cols.reshape(N_KV_HEADS, Q_PER_KV, 1, 4), (N_KV_HEADS, Q_PER_KV, tq, 4)))
    o_p = _attn_prompt(q_p.reshape(b, t, n_q), gate_p.reshape(b, t, N_KV_HEADS * 128), slope_tab, kc_aug,
                       vc_p.astype(BF16), ks_aug, vs, kw_aug, vw, cmat_t, tq=tq)
    x_sample, state_pool, page_table, o_p = lax.optimization_barrier((x_sample, state_pool, page_table, o_p))
    y_prompt = _mlp(x2, row(norm_mlp[1]), wu[1], wd[1], row(norm_final), final_norm=True, tm=1024, tf=1024,
                    attn=o_p.reshape(b * t, n_q), wo_bf=wo, vmem_mib=60).reshape(b, t, d)

    xs0 = x_sample.reshape(nb, d)
    xs1, u_s = _pool_sample(xs0, state_pool[:, 0], row(norm_mix[0]), pw, row(pool_scale[0]))
    pool_s = jnp.concatenate([state_pool[:, 0, 1:], u_s[:, None]], axis=1)[:, None]
    xs2 = _mlp(xs1, row(norm_mlp[0]), wu[0], wd[0], row(norm_final), final_norm=False, tm=nb)
    kv_s, = _lin(xs2, wkv, g=row(norm_kv), tm=nb)
    kv_rows_s = kv_s[:, :ROW_WIDTH].reshape(nb, 1, 2, 2, N_KV_HEADS, HEAD_DIM)

    pages = cache_kv_pages.reshape(n_phys, page_rows, ROW_WIDTH // 128, 128).transpose(0, 2, 1, 3)
    kc_s, vc_s = _compress(pages, page_table, w1d, w1_bf, pe8, w2_bf, paged=True)
    n_sub_s = kc_s.shape[2]
    all_heads = lambda a: a.transpose(0, 2, 1, 3).reshape(nb, n_sub_s, KV_WIDTH).astype(BF16)

    q_s, gate_s = _lin(xs2, wqg, g=row(norm_mix[1]), b=bqg, splits=(n_q, N_KV_HEADS * 128), tm=nb)
    q4 = q_s.reshape(nb, N_KV_HEADS, Q_PER_KV, HEAD_DIM) * (HEAD_DIM ** -0.5)
    qbd = jnp.einsum('bgrd,gh->bgrhd', q4, jnp.eye(N_KV_HEADS, dtype=F32)).reshape(nb, N_HEADS, KV_WIDTH)
    g16 = gate_s.reshape(nb, N_KV_HEADS, 128)[:, :, :Q_PER_KV * N_BRANCH].reshape(nb, N_HEADS, N_BRANCH)
    g16 = jnp.pad(g16, ((0, 0), (0, 0), (0, 128 - N_BRANCH)))
    slope16 = jnp.broadcast_to(slopes[:, None], (N_HEADS, 128))
    past_len = page_table.shape[1] * page_rows
    n_blk_pad = 64
    expand = jnp.asarray(np.eye(n_blk_pad, dtype=np.float32)[:, np.arange(past_len) // SEL_BLOCK], BF16)
    cmat_s = jnp.asarray(_cmp_to_sel(n_sub_s, n_sub_s - 1, n_blk_pad), BF16)
    rmat = jnp.asarray(np.kron(np.eye(N_KV_HEADS), np.ones((Q_PER_KV, Q_PER_KV))), BF16)
    o_s, win_new_s = _attn_sample(
        qbd.astype(BF16), g16, slope16, all_heads(kc_s), all_heads(vc_s), pages, page_table,
        kv_s.reshape(nb, 1, 6 * KV_WIDTH), state_win.reshape(nb, state_win.shape[1], 2 * KV_WIDTH),
        expand, cmat_s, rmat)
    o5 = o_s.reshape(nb, N_KV_HEADS, Q_PER_KV, N_KV_HEADS, HEAD_DIM)
    o_heads = jnp.stack([o5[:, gh, :, gh] for gh in range(N_KV_HEADS)], axis=1).reshape(nb, n_q)
    y_sample = _mlp(xs2, row(norm_mlp[1]), wu[1], wd[1], row(norm_final), final_norm=True, tm=nb,
                    attn=o_heads, wo_bf=wo).reshape(nb, 1, d)

    return (y_prompt, y_sample, pool_p, pool_s, kv_rows_p, kv_rows_s, win_new_p,
            win_new_s.reshape(state_win.shape))
```

```python
import functools

import numpy as np
import jax
import jax.numpy as jnp
from jax import lax
from jax.experimental import pallas as pl
from jax.experimental.pallas import tpu as pltpu

D_MODEL = 1024
POOL_WINDOWS = (2, 4, 8, 16)
POOL_GROUP_DIM = D_MODEL // len(POOL_WINDOWS)
POOL_BUF = max(POOL_WINDOWS) - 1
N_HEADS = 16
HEAD_DIM = 64
N_KV_HEADS = 4
Q_PER_KV = N_HEADS // N_KV_HEADS
N_BRANCH = 3
CMP_BLOCK = 32
CMP_STRIDE = 16
CMP_HIDDEN = 2 * HEAD_DIM
SEL_BLOCK = 64
N_SELECT = 16
WINDOW = 512
RMS_EPS = 1e-6
FORCED_PRIORITY = 1e6
KV_WIDTH = N_KV_HEADS * HEAD_DIM
ROW_WIDTH = 2 * 2 * KV_WIDTH

NEG = -1e30
F32 = jnp.float32
BF16 = jnp.bfloat16
MIB = 1024 * 1024


def _params(semantics, vmem_mib):
    return pltpu.CompilerParams(dimension_semantics=semantics, vmem_limit_bytes=vmem_mib * MIB)


def _rms(x, g):
    return x * lax.rsqrt(jnp.mean(x * x, axis=-1, keepdims=True) + RMS_EPS) * g


def _dot(a, b):
    return jnp.dot(a, b, preferred_element_type=F32)


def _dot_nt(a, b):
    return lax.dot_general(a, b, (((1,), (1,)), ((), ())), preferred_element_type=F32)


def _split_bf16(x):
    hi = x.astype(BF16)
    lo = (x - hi.astype(F32)).astype(BF16)
    return hi, lo


def _sigmoid(x):
    return 1.0 / (1.0 + jnp.exp(-x))


def _pool_prompt_kernel(x_ref, xp_ref, g_ref, w_ref, sc_ref, y_ref, ul_ref, u_scr, *, tt):
    i = pl.program_id(1)
    g = g_ref[...]
    x = x_ref[0]
    u = _rms(x, g)
    up = _rms(xp_ref[0], g) * (i > 0).astype(F32)
    u_scr[0:16, :] = up
    u_scr[16:16 + tt, :] = u
    t = i * tt + lax.broadcasted_iota(jnp.int32, (tt, 1), 0)
    for gi, win in enumerate(POOL_WINDOWS):
        cols = slice(gi * POOL_GROUP_DIM, (gi + 1) * POOL_GROUP_DIM)
        acc = u[:, cols]
        for k in range(1, win):
            acc = acc + u_scr[16 - k:16 - k + tt, cols]
        cnt = jnp.minimum(t + 1, win).astype(F32)
        diff = acc / cnt - u[:, cols]
        yg = _dot(diff.astype(BF16), w_ref[gi])
        y_ref[0, :, cols] = x[:, cols] + yg * sc_ref[:, cols]
    ul_ref[0] = u[tt - 16:, :]


def _pool_prompt(x, g, w_bf, scale, *, tt=512):
    b, t, d = x.shape
    n_t = t // tt
    kern = functools.partial(_pool_prompt_kernel, tt=tt)
    return pl.pallas_call(
        kern,
        grid=(b, n_t),
        in_specs=[
            pl.BlockSpec((1, tt, d), lambda bi, i: (bi, i, 0)),
            pl.BlockSpec((1, 16, d), lambda bi, i: (bi, jnp.maximum(i * (tt // 16) - 1, 0), 0)),
            pl.BlockSpec((1, d), lambda bi, i: (0, 0)),
            pl.BlockSpec((len(POOL_WINDOWS), POOL_GROUP_DIM, POOL_GROUP_DIM), lambda bi, i: (0, 0, 0)),
            pl.BlockSpec((1, d), lambda bi, i: (0, 0)),
        ],
        out_specs=[
            pl.BlockSpec((1, tt, d), lambda bi, i: (bi, i, 0)),
            pl.BlockSpec((1, 16, d), lambda bi, i: (bi, 0, 0)),
        ],
        out_shape=[jax.ShapeDtypeStruct((b, t, d), F32), jax.ShapeDtypeStruct((b, 16, d), F32)],
        scratch_shapes=[pltpu.VMEM((16 + tt, d), F32)],
        compiler_params=_params(("parallel", "arbitrary"), 32),
        name="pool_prompt",
    )(x, x, g, w_bf, scale)


def _pool_sample_kernel(x_ref, past_ref, g_ref, w_ref, sc_ref, y_ref, u_ref):
    x = x_ref[...]
    u = _rms(x, g_ref[...])
    for gi, win in enumerate(POOL_WINDOWS):
        cols = slice(gi * POOL_GROUP_DIM, (gi + 1) * POOL_GROUP_DIM)
        acc = u[:, cols]
        for k in range(1, win):
            acc = acc + past_ref[:, POOL_BUF - k, cols]
        diff = acc / float(win) - u[:, cols]
        yg = _dot(diff.astype(BF16), w_ref[gi])
        y_ref[:, cols] = x[:, cols] + yg * sc_ref[:, cols]
    u_ref[...] = u


def _pool_sample(x, past, g, w_bf, scale, *, bt=32):
    nb, d = x.shape
    return pl.pallas_call(
        _pool_sample_kernel,
        grid=(nb // bt,),
        in_specs=[
            pl.BlockSpec((bt, d), lambda i: (i, 0)),
            pl.BlockSpec((bt, POOL_BUF, d), lambda i: (i, 0, 0)),
            pl.BlockSpec((1, d), lambda i: (0, 0)),
            pl.BlockSpec((len(POOL_WINDOWS), POOL_GROUP_DIM, POOL_GROUP_DIM), lambda i: (0, 0, 0)),
            pl.BlockSpec((1, d), lambda i: (0, 0)),
        ],
        out_specs=[pl.BlockSpec((bt, d), lambda i: (i, 0)), pl.BlockSpec((bt, d), lambda i: (i, 0))],
        out_shape=[jax.ShapeDtypeStruct((nb, d), F32), jax.ShapeDtypeStruct((nb, d), F32)],
        compiler_params=_params(("parallel",), 32),
        name="pool_sample",
    )(x, past, g, w_bf, scale)


def _mlp_kernel(*refs, final_norm, pre_proj):
    if pre_proj:
        x_ref, a_ref, wo_ref, g_ref, wu_ref, wd_ref, gf_ref, o_ref, xn_scr, acc_scr = refs
    else:
        x_ref, g_ref, wu_ref, wd_ref, gf_ref, o_ref, xn_scr, acc_scr = refs
    j = pl.program_id(1)

    @pl.when(j == 0)
    def _():
        x = x_ref[...]
        if pre_proj:
            x = x + _dot(a_ref[...].astype(BF16), wo_ref[...])
            o_ref[...] = x
        xn_scr[...] = _rms(x, g_ref[...]).astype(BF16)
        acc_scr[...] = jnp.zeros_like(acc_scr)

    h = jnp.maximum(_dot(xn_scr[...], wu_ref[...]), 0.0)
    acc_scr[...] += _dot((h * h).astype(BF16), wd_ref[...])

    @pl.when(j == pl.num_programs(1) - 1)
    def _():
        r = (o_ref[...] if pre_proj else x_ref[...]) + acc_scr[...]
        o_ref[...] = _rms(r, gf_ref[...]) if final_norm else r


def _mlp(x, g, wu_bf, wd_bf, gf, *, final_norm, tm, tf=1024, attn=None, wo_bf=None, vmem_mib=48):
    m, d = x.shape
    f = wu_bf.shape[1]
    pre_proj = attn is not None
    kern = functools.partial(_mlp_kernel, final_norm=final_norm, pre_proj=pre_proj)
    rows = pl.BlockSpec((tm, d), lambda i, j: (i, 0))
    vec = pl.BlockSpec((1, d), lambda i, j: (0, 0))
    pre_specs = [pl.BlockSpec((tm, attn.shape[1]), lambda i, j: (i, 0)),
                 pl.BlockSpec(wo_bf.shape, lambda i, j: (0, 0))] if pre_proj else []
    pre_args = [attn, wo_bf] if pre_proj else []
    return pl.pallas_call(
        kern,
        grid=(m // tm, f // tf),
        in_specs=[rows] + pre_specs + [
            vec,
            pl.BlockSpec((d, tf), lambda i, j: (0, j)),
            pl.BlockSpec((tf, d), lambda i, j: (j, 0)),
            vec,
        ],
        out_specs=rows,
        out_shape=jax.ShapeDtypeStruct((m, d), F32),
        scratch_shapes=[pltpu.VMEM((tm, d), BF16), pltpu.VMEM((tm, d), F32)],
        compiler_params=_params(("parallel", "arbitrary"), vmem_mib),
        name="mlp",
    )(x, *pre_args, g, wu_bf, wd_bf, gf)


def _proj_kernel(x_ref, gkv_ref, gq_ref, wkv_ref, wqg_ref, bqg_ref, pc_ref, oh_ref,
                 rows_ref, win_ref, q_ref, gate_ref, ks_ref, vs_ref, kw_ref, vw_ref):
    x = x_ref[...]
    xh = x * lax.rsqrt(jnp.mean(x * x, axis=-1, keepdims=True) + RMS_EPS)
    hkv = _dot((xh * gkv_ref[...]).astype(BF16), wkv_ref[...])
    hq = _dot((xh * gq_ref[...]).astype(BF16), wqg_ref[...]) + bqg_ref[...]
    n_q = N_HEADS * HEAD_DIM
    rows_ref[...] = hkv[:, :ROW_WIDTH]
    win_ref[...] = hkv[:, ROW_WIDTH:]
    q_ref[...] = hq[:, :n_q]
    gate_ref[...] = hq[:, n_q:]

    lane = lax.broadcasted_iota(jnp.int32, (x.shape[0], 128), 1)
    pc = pc_ref[...]
    ones = (lane == HEAD_DIM).astype(F32)

    def head(col0, gh, filler):
        pair = hkv[:, col0 + (gh // 2) * 128:col0 + (gh // 2 + 1) * 128]
        if gh % 2:
            pair = pltpu.roll(pair, HEAD_DIM, 1)
        return jnp.where(lane < HEAD_DIM, pair, filler).astype(BF16)

    for gh in range(N_KV_HEADS):
        ks_ref[0, gh, :, 0:128] = head(2 * KV_WIDTH, gh, pc)
        ks_ref[0, gh, :, 128:256] = oh_ref[...]
        vs_ref[0, gh] = head(3 * KV_WIDTH, gh, ones)
        kw_ref[0, gh] = head(4 * KV_WIDTH, gh, pc)
        vw_ref[0, gh] = head(5 * KV_WIDTH, gh, ones)


def _proj(x, g_kv, g_q, wkv_bf, wqg_bf, bqg, pos_cols, onehot, *, b, t, tm=512):
    m, d = x.shape
    nt = t // tm
    n_kv = wkv_bf.shape[1]
    n_qg = wqg_bf.shape[1]
    n_q = N_HEADS * HEAD_DIM
    const = lambda *shape: pl.BlockSpec(shape, lambda i: (0,) * len(shape))
    rows_of = lambda n: pl.BlockSpec((tm, n), lambda i: (i, 0))
    heads_of = lambda n: pl.BlockSpec((1, N_KV_HEADS, tm, n), lambda i: (i // nt, 0, i % nt, 0))
    heads_shape = lambda n: jax.ShapeDtypeStruct((b, N_KV_HEADS, t, n), BF16)
    return pl.pallas_call(
        _proj_kernel,
        grid=(m // tm,),
        in_specs=[rows_of(d), const(1, d), const(1, d), const(d, n_kv), const(d, n_qg), const(1, n_qg),
                  pl.BlockSpec((tm, 128), lambda i: (i % nt, 0)), pl.BlockSpec((tm, 128), lambda i: (i % nt, 0))],
        out_specs=[rows_of(ROW_WIDTH), rows_of(n_kv - ROW_WIDTH), rows_of(n_q), rows_of(n_qg - n_q),
                   heads_of(256), heads_of(128), heads_of(128), heads_of(128)],
        out_shape=[jax.ShapeDtypeStruct((m, ROW_WIDTH), F32), jax.ShapeDtypeStruct((m, n_kv - ROW_WIDTH), F32),
                   jax.ShapeDtypeStruct((m, n_q), F32), jax.ShapeDtypeStruct((m, n_qg - n_q), F32),
                   heads_shape(256), heads_shape(128), heads_shape(128), heads_shape(128)],
        compiler_params=_params(("parallel",), 56),
        name="proj",
    )(x, g_kv, g_q, wkv_bf, wqg_bf, bqg, pos_cols, onehot)


def _lin_kernel(x_ref, g_ref, w_ref, b_ref, *o_refs, splits):
    h = _dot(_rms(x_ref[...], g_ref[...]).astype(BF16), w_ref[...]) + b_ref[...]
    c0 = 0
    for o_ref, n in zip(o_refs, splits):
        o_ref[...] = h[:, c0:c0 + n]
        c0 += n


def _lin(x, g, w_bf, b, *, splits, tm):
    m, k = x.shape
    n = w_bf.shape[1]
    return pl.pallas_call(
        functools.partial(_lin_kernel, splits=splits),
        grid=(m // tm,),
        in_specs=[pl.BlockSpec((tm, k), lambda i: (i, 0)), pl.BlockSpec((1, k), lambda i: (0, 0)),
                  pl.BlockSpec((k, n), lambda i: (0, 0)), pl.BlockSpec((1, n), lambda i: (0, 0))],
        out_specs=[pl.BlockSpec((tm, s), lambda i: (i, 0)) for s in splits],
        out_shape=[jax.ShapeDtypeStruct((m, s), F32) for s in splits],
        compiler_params=_params(("parallel",), 48),
        name="lin",
    )(x, g, w_bf, b)


def _compress_kernel(pt_ref, *refs, n_page_refs, paged, rows_per_ref, n_sub):
    del pt_ref
    page_refs = refs[:n_page_refs]
    w1d_ref, w1_ref, pe_ref, w2_ref, kc_ref, vc_ref = refs[n_page_refs:]
    n_take = rows_per_ref // CMP_STRIDE

    def tap_rows(c, j):
        take = pl.ds(j, n_take, stride=CMP_STRIDE)
        if paged:
            return jnp.concatenate([r[0, c, take, :] for r in page_refs], axis=0)
        return page_refs[c][0, take, :]

    for kv, o_ref in enumerate((kc_ref, vc_ref)):
        pew = _dot(pe_ref[kv].astype(BF16), w1_ref[kv])[0:1]
        for c2 in range(2):
            acc = None
            for jp in range(CMP_STRIDE // 2):
                lhs = jnp.concatenate([tap_rows(2 * kv + c2, 2 * jp), tap_rows(2 * kv + c2, 2 * jp + 1)], axis=1)
                d = _dot(lhs.astype(BF16), w1d_ref[kv, jp])
                acc = d if acc is None else acc + d
            for e in range(2):
                a = acc[:, 2 * e * CMP_HIDDEN:(2 * e + 1) * CMP_HIDDEN]
                b2 = acc[:, (2 * e + 1) * CMP_HIDDEN:(2 * e + 2) * CMP_HIDDEN]
                hid = a + pltpu.roll(b2, a.shape[0] - 1, 0) + pew
                act = hid * _sigmoid(hid)
                out = _dot(act.astype(BF16), w2_ref[kv])
                for k in range(out.shape[0] // n_sub):
                    o_ref[k, 2 * c2 + e] = out[k * n_sub:(k + 1) * n_sub]


def _compress(pages, page_table, w1d_bf, w1_bf, pe8, w2_bf, *, paged):
    if paged:
        nb, n_pages = page_table.shape
        rows_per_ref = pages.shape[2]
        n_sub = n_pages * rows_per_ref // CMP_STRIDE
        seqs = 4
        page_specs = [
            pl.BlockSpec((1, 4, rows_per_ref, 128),
                         functools.partial(lambda k, p, b, pt: (pt[seqs * b + k, p], 0, 0, 0), k, p))
            for k in range(seqs) for p in range(n_pages)]
    else:
        nb, rows_per_ref, _ = pages.shape
        n_sub = rows_per_ref // CMP_STRIDE
        seqs = 1
        page_specs = [pl.BlockSpec((1, rows_per_ref, 128), functools.partial(lambda c, b, pt: (b, 0, c), c))
                      for c in range(4)]
    page_args = [pages] * len(page_specs)
    kern = functools.partial(_compress_kernel, n_page_refs=len(page_args), paged=paged, rows_per_ref=rows_per_ref,
                             n_sub=n_sub)
    const = lambda *shape: pl.BlockSpec(shape, lambda b, pt: (0,) * len(shape))
    out_spec = pl.BlockSpec((seqs, N_KV_HEADS, n_sub, HEAD_DIM), lambda b, pt: (b, 0, 0, 0))
    return pl.pallas_call(
        kern,
        grid_spec=pltpu.PrefetchScalarGridSpec(
            num_scalar_prefetch=1,
            grid=(nb // seqs,),
            in_specs=page_specs + [
                const(2, CMP_STRIDE // 2, 4 * HEAD_DIM, 4 * CMP_HIDDEN),
                const(2, CMP_BLOCK * HEAD_DIM, CMP_HIDDEN),
                const(2, 8, CMP_BLOCK * HEAD_DIM),
                const(2, CMP_HIDDEN, HEAD_DIM),
            ],
            out_specs=[out_spec, out_spec],
        ),
        out_shape=[jax.ShapeDtypeStruct((nb, N_KV_HEADS, n_sub, HEAD_DIM), F32)] * 2,
        compiler_params=_params(("parallel",), 48),
        name="compress",
    )(page_table, *page_args, w1d_bf, w1_bf, pe8, w2_bf)


def _attn_prompt_kernel(q_ref, gt_ref, st_ref, kc_ref, vc_ref, ks_ref, vs_ref, kw_ref, vw_ref, cm_ref, dm_ref,
                        dc_ref, wb_ref, o_ref, qa_scr, rank_scr, ow_scr, m_scr, ala_scr, alb_scr, sa_scr, sb_scr, pa_scr,
                        pb_scr, acc_scr, *, tq, tk, rc, n_sel):
    c0 = pl.program_id(2) * tq
    rows = Q_PER_KV * tq
    band = WINDOW + tq

    lane = lax.broadcasted_iota(jnp.int32, (tq, 128), 1)
    qb = q_ref[0] * (HEAD_DIM ** -0.5)
    for r in range(Q_PER_KV):
        pair = qb[:, (r // 2) * 128:(r // 2 + 1) * 128]
        if r % 2:
            pair = pltpu.roll(pair, HEAD_DIM, 1)
        qa_scr[r * tq:(r + 1) * tq, 0:128] = jnp.where(lane < HEAD_DIM, pair, st_ref[0, r]).astype(BF16)
    ql = qa_scr[:, 0:128]

    n_cmp_pad = kc_ref.shape[2]
    s = _dot_nt(ql, kc_ref[0, 0])
    mask = jnp.concatenate([dc_ref[...]] * Q_PER_KV, axis=0) <= c0
    s = jnp.where(mask, s, NEG)
    m = jnp.max(s, axis=-1, keepdims=True)
    e = jnp.where(mask, jnp.exp(s - m), 0.0)
    p = e / jnp.maximum(jnp.sum(e, axis=-1, keepdims=True), 1e-30)
    o_c = _dot(p.astype(BF16), vc_ref[0, 0])

    w0 = pl.multiple_of(jnp.maximum(c0 - WINDOW, 0), 128)
    sw = _dot_nt(ql, kw_ref[0, 0, pl.ds(w0, band), :]) + jnp.concatenate([wb_ref[0]] * Q_PER_KV, axis=0)
    pw = jnp.exp(sw - jnp.max(sw, axis=-1, keepdims=True))
    aw = _dot(pw.astype(BF16), vw_ref[0, 0, pl.ds(w0, band), :])
    ow_scr[...] = aw[:, :HEAD_DIM] / aw[:, HEAD_DIM:HEAD_DIM + 1]

    psum = p[0:tq] + p[tq:2 * tq] + p[2 * tq:3 * tq] + p[3 * tq:4 * tq]
    hi, lo = _split_bf16(psum)
    imp = _dot_nt(cm_ref[...], hi) + _dot_nt(cm_ref[...], lo)
    blk = lax.broadcasted_iota(jnp.int32, (n_sel, tq), 0)
    tcol = c0 + lax.broadcasted_iota(jnp.int32, (n_sel, tq), 1)
    cur = tcol >> 6
    forced = (blk == 0) | (blk == cur) | (blk == cur - 1)
    valid = blk * SEL_BLOCK <= tcol
    pri = jnp.where(valid, jnp.where(forced, FORCED_PRIORITY, imp), -1.0)
    n_valid = c0 // SEL_BLOCK + tq // SEL_BLOCK
    rank_scr[...] = jnp.zeros_like(rank_scr)
    sub8 = lax.broadcasted_iota(jnp.int32, (8, tq), 0)
    for g8 in range(n_sel // 8):
        lo8, hi8 = 8 * g8, 8 * g8 + 8

        @pl.when((n_valid > N_SELECT) & (lo8 < n_valid))
        def _(lo8=lo8, hi8=hi8):
            mid = pri[lo8:hi8]
            below = above = None
            inside = jnp.zeros((8, tq), jnp.int32)
            for i in range(8):
                row = mid[i:i + 1, :]
                if lo8:
                    b_i = (row > pri[:lo8]).astype(jnp.int32)
                    below = b_i if below is None else below + b_i
                inside = inside + ((row > mid) | ((row == mid) & (sub8 > i))).astype(jnp.int32)
                if hi8 < n_sel:
                    a_i = (row >= pri[hi8:]).astype(jnp.int32)
                    above = a_i if above is None else above + a_i
            parts = [part for part in (below, inside, above) if part is not None]
            rank_scr[...] += jnp.concatenate(parts, axis=0)

    bias = jnp.where(rank_scr[...] < min(N_SELECT, n_sel), 0.0, NEG).T
    right = jnp.concatenate([bias, jnp.zeros((tq, 128 - n_sel), F32)], axis=1).astype(BF16)
    for r in range(Q_PER_KV):
        qa_scr[r * tq:(r + 1) * tq, 128:256] = right

    m_scr[...] = jnp.full_like(m_scr, NEG)
    acc_scr[...] = jnp.zeros_like(acc_scr)

    def scores(j):
        return _dot_nt(qa_scr[...], ks_ref[0, 0, pl.ds(pl.multiple_of(j * tk, tk), tk), :])

    def soft_pv(s_scr, p_scr, al_scr, j, masked):
        k0 = pl.multiple_of(j * tk, tk)
        for i in range(rows // rc):
            rs = slice(i * rc, (i + 1) * rc)
            qs = (i * rc) % tq
            sc = s_scr[rs, :]
            if masked:
                sc = jnp.where(dm_ref[qs:qs + rc, :] <= c0 - k0, sc, NEG)
            m_old = m_scr[rs]
            m_new = jnp.maximum(m_old, jnp.max(sc, axis=-1, keepdims=True))
            al_scr[rs] = jnp.exp(m_old - m_new)
            p_scr[rs] = jnp.exp(sc - m_new).astype(BF16)
            m_scr[rs] = m_new
        acc_scr[...] = al_scr[...] * acc_scr[...] + _dot(p_scr[...], vs_ref[0, 0, pl.ds(k0, tk), :])

    n_tiles = c0 // tk + 1
    n_loop = (n_tiles - 1) // 2
    sa_scr[...] = scores(0)

    def tile_pair(jj, carry):
        sb_scr[...] = scores(2 * jj + 1)
        soft_pv(sa_scr, pa_scr, ala_scr, 2 * jj, False)
        sa_scr[...] = scores(2 * jj + 2)
        soft_pv(sb_scr, pb_scr, alb_scr, 2 * jj + 1, False)
        return carry

    lax.fori_loop(0, n_loop, tile_pair, 0)

    @pl.when(n_tiles % 2 == 0)
    def _():
        sb_scr[...] = scores(2 * n_loop + 1)
        soft_pv(sa_scr, pa_scr, ala_scr, 2 * n_loop, False)
        soft_pv(sb_scr, pb_scr, alb_scr, 2 * n_loop + 1, True)

    @pl.when(n_tiles % 2 == 1)
    def _():
        soft_pv(sa_scr, pa_scr, ala_scr, 2 * n_loop, True)

    o_w = ow_scr[...]
    acc = acc_scr[...]
    o_s = acc[:, :HEAD_DIM] / acc[:, HEAD_DIM:HEAD_DIM + 1]

    gate = _sigmoid(gt_ref[0])
    outs = []
    for r in range(Q_PER_KV):
        rs = slice(r * tq, (r + 1) * tq)
        outs.append(gate[:, 3 * r:3 * r + 1] * o_c[rs] + gate[:, 3 * r + 1:3 * r + 2] * o_s[rs]
                    + gate[:, 3 * r + 2:3 * r + 3] * o_w[rs])
    o_ref[0] = jnp.concatenate(outs, axis=1)


def _attn_prompt(q, gate_pre, slope_tab, kc_aug, vc, ks_aug, vs, kw_aug, vw, cmat_t, *, tq=128, tk=512, rc=128):
    b, t, _ = q.shape
    n_sel = t // SEL_BLOCK
    n_cmp_pad = kc_aug.shape[2]
    kern = functools.partial(_attn_prompt_kernel, tq=tq, tk=tk, rc=rc, n_sel=n_sel)
    rows = Q_PER_KV * tq
    band = WINDOW + tq
    qi = np.arange(tq)[:, None]
    dmat = jnp.asarray(np.arange(tk)[None, :] - qi, jnp.int32)
    dcmp = jnp.asarray(CMP_STRIDE * np.arange(n_cmp_pad)[None, :] + (CMP_BLOCK - 1) - qi, jnp.int32)
    dist = (np.minimum(np.arange(WINDOW // tq + 1) * tq, WINDOW)[:, None, None] + qi[None]
            - np.arange(band)[None, None, :])
    wbias = jnp.asarray(np.where((dist >= 0) & (dist < WINDOW), 0.0, NEG), F32)
    per_bg = lambda *shape: pl.BlockSpec((1, 1) + shape, lambda bi, gi, ci: (bi, gi, 0, 0))
    return pl.pallas_call(
        kern,
        grid=(b, N_KV_HEADS, t // tq),
        in_specs=[
            pl.BlockSpec((1, tq, KV_WIDTH), lambda bi, gi, ci: (bi, ci, gi)),
            pl.BlockSpec((1, tq, 128), lambda bi, gi, ci: (bi, ci, gi)),
            pl.BlockSpec((1, Q_PER_KV, tq, 128), lambda bi, gi, ci: (gi, 0, 0, 0)),
            per_bg(n_cmp_pad, 128),
            per_bg(n_cmp_pad, HEAD_DIM),
            per_bg(t, 256),
            per_bg(t, 128),
            per_bg(t, 128),
            per_bg(t, 128),
            pl.BlockSpec((n_sel, n_cmp_pad), lambda bi, gi, ci: (0, 0)),
            pl.BlockSpec((tq, tk), lambda bi, gi, ci: (0, 0)),
            pl.BlockSpec((tq, n_cmp_pad), lambda bi, gi, ci: (0, 0)),
            pl.BlockSpec((1, tq, band), lambda bi, gi, ci: (jnp.minimum(ci, WINDOW // tq), 0, 0)),
        ],
        out_specs=pl.BlockSpec((1, tq, KV_WIDTH), lambda bi, gi, ci: (bi, ci, gi)),
        out_shape=jax.ShapeDtypeStruct((b, t, N_HEADS * HEAD_DIM), F32),
        scratch_shapes=[
            pltpu.VMEM((rows, 256), BF16),
            pltpu.VMEM((n_sel, tq), jnp.int32),
            pltpu.VMEM((rows, HEAD_DIM), F32),
            pltpu.VMEM((rows, 1), F32),
            pltpu.VMEM((rows, 1), F32),
            pltpu.VMEM((rows, 1), F32),
            pltpu.VMEM((rows, tk), F32),
            pltpu.VMEM((rows, tk), F32),
            pltpu.VMEM((rows, tk), BF16),
            pltpu.VMEM((rows, tk), BF16),
            pltpu.VMEM((rows, 128), F32),
        ],
        compiler_params=_params(("parallel", "parallel", "arbitrary"), 48),
        name="attn_prompt",
    )(q, gate_pre, slope_tab, kc_aug, vc, ks_aug, vs, kw_aug, vw, cmat_t, dmat, dcmp, wbias)


def _softmax_lanes(s):
    e = jnp.exp(s - jnp.max(s, axis=-1, keepdims=True))
    return e / jnp.sum(e, axis=-1, keepdims=True)


def _attn_sample_kernel(pt_ref, *refs, n_pages, past_len, n_sel, seqs):
    del pt_ref
    page_refs = refs[5:5 + seqs * n_pages]
    for k in range(seqs):
        _attn_sample_one(k, *refs[:5], page_refs[k * n_pages:(k + 1) * n_pages], *refs[5 + seqs * n_pages:],
                         past_len=past_len, n_sel=n_sel)


def _attn_sample_one(k, q_ref, gt_ref, sl_ref, kc_ref, vc_ref, page_refs, new_ref, win_ref, ex_ref, cm_ref, rm_ref,
                     o_ref, wn_ref, *, past_len, n_sel):
    q = q_ref[k]
    sl = sl_ref[:, 0:1]
    new = new_ref[k]

    n_cmp_pad = kc_ref.shape[1]
    dist_c = (past_len - (CMP_BLOCK - 1)
              - CMP_STRIDE * lax.broadcasted_iota(jnp.int32, (N_HEADS, n_cmp_pad), 1)).astype(F32)
    s = _dot_nt(q, kc_ref[k]) - sl * dist_c
    mask = dist_c >= 0
    s = jnp.where(mask, s, NEG)
    e = jnp.where(mask, jnp.exp(s - jnp.max(s, axis=-1, keepdims=True)), 0.0)
    p_c = e / jnp.maximum(jnp.sum(e, axis=-1, keepdims=True), 1e-30)
    o_c = _dot(p_c.astype(BF16), vc_ref[k])

    hi, lo = _split_bf16(p_c)
    imp = _dot(hi, cm_ref[...]) + _dot(lo, cm_ref[...])
    hi, lo = _split_bf16(imp)
    imp = _dot(rm_ref[...], hi) + _dot(rm_ref[...], lo)
    n_blk = imp.shape[1]
    blk = lax.broadcasted_iota(jnp.int32, (N_HEADS, n_blk), 1)
    cur = past_len // SEL_BLOCK
    forced = (blk == 0) | (blk == cur) | (blk == cur - 1)
    valid = blk * SEL_BLOCK <= past_len
    pri = jnp.where(valid, jnp.where(forced, FORCED_PRIORITY, imp), -1.0)
    pri = jnp.where(blk < n_sel, pri, -2.0)
    rank = jnp.zeros((N_HEADS, n_blk), jnp.int32)
    for s2 in range(n_sel):
        col = pri[:, s2:s2 + 1]
        beats = (col > pri) | ((col == pri) & (blk > s2))
        rank = rank + beats.astype(jnp.int32)
    bias = jnp.where(rank < min(N_SELECT, n_sel), 0.0, NEG)

    k_sel = jnp.concatenate([jnp.concatenate([r[0, 0], r[0, 1]], axis=1) for r in page_refs],
                            axis=0).astype(BF16)
    v_sel = jnp.concatenate([jnp.concatenate([r[0, 2], r[0, 3]], axis=1) for r in page_refs],
                            axis=0).astype(BF16)
    dist_s = (past_len - lax.broadcasted_iota(jnp.int32, (N_HEADS, past_len), 1)).astype(F32)
    s = _dot_nt(q, k_sel) - sl * dist_s + _dot(bias.astype(BF16), ex_ref[...])
    qf = q.astype(F32)
    k_new = new[:, 2 * KV_WIDTH:3 * KV_WIDTH].astype(BF16).astype(F32)
    v_new = new[:, 3 * KV_WIDTH:4 * KV_WIDTH].astype(BF16).astype(F32)
    s_new = jnp.sum(qf * k_new, axis=-1, keepdims=True) + bias[:, past_len // SEL_BLOCK:past_len // SEL_BLOCK + 1]
    m = jnp.maximum(jnp.max(s, axis=-1, keepdims=True), s_new)
    e = jnp.exp(s - m)
    e_new = jnp.exp(s_new - m)
    l = jnp.sum(e, axis=-1, keepdims=True) + e_new
    o_s = (_dot(e.astype(BF16), v_sel) + e_new.astype(BF16).astype(F32) * v_new) / l

    wb = win_ref.shape[1]
    rolled = pltpu.roll(win_ref[k], wb - 1, 0)
    rowi = lax.broadcasted_iota(jnp.int32, rolled.shape, 0)
    wn = jnp.where(rowi == wb - 1, new[:, 4 * KV_WIDTH:6 * KV_WIDTH], rolled)
    wn_ref[k] = wn
    dist_w = (wb - 1 - lax.broadcasted_iota(jnp.int32, (N_HEADS, wb), 1)).astype(F32)
    p_w = _softmax_lanes(_dot_nt(q, wn[:, 0:KV_WIDTH].astype(BF16)) - sl * dist_w)
    o_w = _dot(p_w.astype(BF16), wn[:, KV_WIDTH:2 * KV_WIDTH].astype(BF16))

    gate = _sigmoid(gt_ref[k])
    o_ref[k] = gate[:, 0:1] * o_c + gate[:, 1:2] * o_s + gate[:, 2:3] * o_w


def _attn_sample(qbd, gate16, slope16, kc_all, vc_all, pages, page_table, new_rows, state_win, expand, cmat, rmat):
    nb, n_pages = page_table.shape
    page_rows = pages.shape[2]
    past_len = n_pages * page_rows
    n_sel = -(-(past_len + 1) // SEL_BLOCK)
    wb = state_win.shape[1]
    n_cmp_pad = kc_all.shape[1]
    seqs = 2
    kern = functools.partial(_attn_sample_kernel, n_pages=n_pages, past_len=past_len, n_sel=n_sel, seqs=seqs)
    per_b = lambda *shape: pl.BlockSpec((seqs,) + shape, lambda b, pt: (b,) + (0,) * len(shape))
    const = lambda *shape: pl.BlockSpec(shape, lambda b, pt: (0,) * len(shape))
    page_specs = [
        pl.BlockSpec((1, 4, page_rows, 128),
                     functools.partial(lambda k, p, b, pt: (pt[seqs * b + k, p], 1, 0, 0), k, p))
        for k in range(seqs) for p in range(n_pages)]
    return pl.pallas_call(
        kern,
        grid_spec=pltpu.PrefetchScalarGridSpec(
            num_scalar_prefetch=1,
            grid=(nb // seqs,),
            in_specs=[per_b(N_HEADS, KV_WIDTH), per_b(N_HEADS, 128), const(N_HEADS, 128),
                      per_b(n_cmp_pad, KV_WIDTH), per_b(n_cmp_pad, KV_WIDTH)]
            + page_specs
            + [per_b(1, 6 * KV_WIDTH), per_b(wb, 2 * KV_WIDTH), const(*expand.shape), const(*cmat.shape),
               const(*rmat.shape)],
            out_specs=[per_b(N_HEADS, KV_WIDTH), per_b(wb, 2 * KV_WIDTH)],
        ),
        out_shape=[jax.ShapeDtypeStruct((nb, N_HEADS, KV_WIDTH), F32),
                   jax.ShapeDtypeStruct((nb, wb, 2 * KV_WIDTH), F32)],
        compiler_params=_params(("parallel",), 48),
        name="attn_sample",
    )(page_table, qbd, gate16, slope16, kc_all, vc_all, *([pages] * (seqs * n_pages)), new_rows, state_win, expand,
      cmat, rmat)


def _alibi_slopes():
    return jnp.exp2(-8.0 * (jnp.arange(N_HEADS, dtype=F32) + 1.0) / N_HEADS)


def _cmp_to_sel(n_cmp_pad, n_cmp, n_sel_pad):
    m = np.zeros((n_cmp_pad, n_sel_pad), np.float32)
    for n in range(n_cmp):
        for k in range(CMP_BLOCK // CMP_STRIDE):
            m[n, (n + k) * CMP_STRIDE // SEL_BLOCK] += 1.0
    return m


def _pos_cols(pos, width):
    cols = np.zeros((pos.shape[0], width), np.float32)
    cols[:, 0] = cols[:, 1] = pos % 64
    cols[:, 2] = cols[:, 3] = pos // 64
    return cols


def kernel(x_prompt, x_sample, state_pool, cache_kv_pages, state_win, page_table, norm_mix, norm_mlp, w_up, w_down,
           pool_w, pool_scale, norm_kv, w_kv, cmp_pe, cmp_w1, cmp_w2, w_qg, b_gate, w_o, norm_final):
    b, t, d = x_prompt.shape
    nb = x_sample.shape[0]
    n_phys, page_rows = cache_kv_pages.shape[:2]
    n_q = N_HEADS * HEAD_DIM
    row = lambda v: v.reshape(1, -1)

    wu = w_up.astype(BF16)
    wd = w_down.astype(BF16)
    pw = pool_w[0].astype(BF16)
    wkv = w_kv.astype(BF16)
    wo = w_o[0].astype(BF16)
    hh = np.arange(N_HEADS)
    gcols = ((hh // Q_PER_KV) * 128 + (hh % Q_PER_KV) * N_BRANCH)[:, None] + np.arange(N_BRANCH)[None, :]
    gcols = gcols.reshape(-1)
    wg = jnp.zeros((d, N_KV_HEADS * 128), F32).at[:, gcols].set(w_qg[0][:, n_q:])
    wqg = jnp.concatenate([w_qg[0][:, :n_q], wg], axis=1).astype(BF16)
    bqg = jnp.zeros((1, n_q + N_KV_HEADS * 128), F32).at[0, n_q + gcols].set(b_gate[0])
    w1_bf = cmp_w1.astype(BF16)
    w1r = w1_bf.reshape(2, 2, CMP_STRIDE, HEAD_DIM, CMP_HIDDEN)
    w1c = jnp.concatenate([w1r[:, 0], w1r[:, 1]], axis=-1)
    w1d = jnp.concatenate([jnp.concatenate([w1c, jnp.zeros_like(w1c)], axis=-1),
                           jnp.concatenate([jnp.zeros_like(w1c), w1c], axis=-1)], axis=2)
    w1d = w1d.reshape(2, CMP_STRIDE // 2, 4 * HEAD_DIM, 4 * CMP_HIDDEN)
    pe8 = jnp.broadcast_to(cmp_pe.reshape(2, 1, CMP_BLOCK * HEAD_DIM), (2, 8, CMP_BLOCK * HEAD_DIM))
    w2_bf = cmp_w2.astype(BF16)

    slopes = _alibi_slopes()
    s_hi = slopes.astype(BF16).astype(F32)
    s_lo = (slopes - s_hi).astype(BF16).astype(F32)
    slope_cols = jnp.stack([s_hi, s_lo, 64.0 * s_hi, 64.0 * s_lo], axis=-1)

    x1, u_last = _pool_prompt(x_prompt, row(norm_mix[0]), pw, row(pool_scale[0]))
    pool_p = u_last[:, None, 16 - POOL_BUF:, :]
    x2 = _mlp(x1.reshape(b * t, d), row(norm_mlp[0]), wu[0], wd[0], row(norm_final), final_norm=False, tm=1024)
    tq = 256
    n_sub_p = t // CMP_STRIDE
    n_sel_p = t // SEL_BLOCK
    tpos = np.arange(t)
    pc_t = jnp.asarray(np.concatenate([np.zeros((t, HEAD_DIM), np.float32), _pos_cols(tpos, HEAD_DIM)], axis=1))
    onehot_t = jnp.asarray(np.eye(n_sel_p, 128, dtype=np.float32)[tpos // SEL_BLOCK], BF16)
    kv_rows, kv_win, q_p, gate_p, ks_aug, vs, kw_aug, vw = _proj(
        x2, row(norm_kv), row(norm_mix[1]), wkv, wqg, bqg, pc_t, onehot_t, b=b, t=t)
    kv_rows_p = kv_rows.reshape(b, t, 2, 2, N_KV_HEADS, HEAD_DIM)
    win_new_p = kv_win.reshape(b, t, 2, N_KV_HEADS, HEAD_DIM)[:, -min(WINDOW, t):]

    dummy_pt = jnp.zeros((b, 1), jnp.int32)
    kc_p, vc_p = _compress(kv_rows.reshape(b, t, ROW_WIDTH), dummy_pt, w1d, w1_bf, pe8, w2_bf, paged=False)
    cache_kv_pages, kc_p = lax.optimization_barrier((cache_kv_pages, kc_p))

    bcast = lambda a: jnp.broadcast_to(a[None, None], (b, N_KV_HEADS) + a.shape)
    pc_c = jnp.asarray(_pos_cols(CMP_STRIDE * np.arange(n_sub_p) + CMP_BLOCK - 1, HEAD_DIM), BF16)
    kc_aug = jnp.concatenate([kc_p.astype(BF16), bcast(pc_c)], axis=-1)
    cmat_t = jnp.asarray(_cmp_to_sel(n_sub_p, n_sub_p - 1, n_sel_p).T, BF16)
    slope_tab = jnp.zeros((N_KV_HEADS, Q_PER_KV, tq, 128), F32).at[:, :, :, HEAD_DIM:HEAD_DIM + 4].set(
        jnp.broadcast_to(slope_cols.reshape(N_KV_HEADS, Q_PER_KV, 1, 4), (N_KV_HEADS, Q_PER_KV, tq, 4)))
    o_p = _attn_prompt(q_p.reshape(b, t, n_q), gate_p.reshape(b, t, N_KV_HEADS * 128), slope_tab, kc_aug,
                       vc_p.astype(BF16), ks_aug, vs, kw_aug, vw, cmat_t, tq=tq)
    x_sample, state_pool, page_table, o_p = lax.optimization_barrier((x_sample, state_pool, page_table, o_p))
    y_prompt = _mlp(x2, row(norm_mlp[1]), wu[1], wd[1], row(norm_final), final_norm=True, tm=1024, tf=1024,
                    attn=o_p.reshape(b * t, n_q), wo_bf=wo, vmem_mib=60).reshape(b, t, d)

    xs0 = x_sample.reshape(nb, d)
    xs1, u_s = _pool_sample(xs0, state_pool[:, 0], row(norm_mix[0]), pw, row(pool_scale[0]))
    pool_s = jnp.concatenate([state_pool[:, 0, 1:], u_s[:, None]], axis=1)[:, None]
    xs2 = _mlp(xs1, row(norm_mlp[0]), wu[0], wd[0], row(norm_final), final_norm=False, tm=nb)
    kv_s, = _lin(xs2, row(norm_kv), wkv, jnp.zeros((1, wkv.shape[1]), F32), splits=(wkv.shape[1],), tm=nb)
    kv_rows_s = kv_s[:, :ROW_WIDTH].reshape(nb, 1, 2, 2, N_KV_HEADS, HEAD_DIM)

    pages = cache_kv_pages.reshape(n_phys, page_rows, ROW_WIDTH // 128, 128).transpose(0, 2, 1, 3)
    kc_s, vc_s = _compress(pages, page_table, w1d, w1_bf, pe8, w2_bf, paged=True)
    n_sub_s = kc_s.shape[2]
    all_heads = lambda a: a.transpose(0, 2, 1, 3).reshape(nb, n_sub_s, KV_WIDTH).astype(BF16)

    q_s, gate_s = _lin(xs2, row(norm_mix[1]), wqg, bqg, splits=(n_q, N_KV_HEADS * 128), tm=nb)
    q4 = q_s.reshape(nb, N_KV_HEADS, Q_PER_KV, HEAD_DIM) * (HEAD_DIM ** -0.5)
    qbd = jnp.einsum('bgrd,gh->bgrhd', q4, jnp.eye(N_KV_HEADS, dtype=F32)).reshape(nb, N_HEADS, KV_WIDTH)
    g16 = gate_s.reshape(nb, N_KV_HEADS, 128)[:, :, :Q_PER_KV * N_BRANCH].reshape(nb, N_HEADS, N_BRANCH)
    g16 = jnp.pad(g16, ((0, 0), (0, 0), (0, 128 - N_BRANCH)))
    slope16 = jnp.broadcast_to(slopes[:, None], (N_HEADS, 128))
    past_len = page_table.shape[1] * page_rows
    n_blk_pad = 64
    expand = jnp.asarray(np.eye(n_blk_pad, dtype=np.float32)[:, np.arange(past_len) // SEL_BLOCK], BF16)
    cmat_s = jnp.asarray(_cmp_to_sel(n_sub_s, n_sub_s - 1, n_blk_pad), BF16)
    rmat = jnp.asarray(np.kron(np.eye(N_KV_HEADS), np.ones((Q_PER_KV, Q_PER_KV))), BF16)
    o_s, win_new_s = _attn_sample(
        qbd.astype(BF16), g16, slope16, all_heads(kc_s), all_heads(vc_s), pages, page_table,
        kv_s.reshape(nb, 1, 6 * KV_WIDTH), state_win.reshape(nb, state_win.shape[1], 2 * KV_WIDTH),
        expand, cmat_s, rmat)
    o5 = o_s.reshape(nb, N_KV_HEADS, Q_PER_KV, N_KV_HEADS, HEAD_DIM)
    o_heads = jnp.stack([o5[:, gh, :, gh] for gh in range(N_KV_HEADS)], axis=1).reshape(nb, n_q)
    y_sample = _mlp(xs2, row(norm_mlp[1]), wu[1], wd[1], row(norm_final), final_norm=True, tm=nb,
                    attn=o_heads, wo_bf=wo).reshape(nb, 1, d)

    return (y_prompt, y_sample, pool_p, pool_s, kv_rows_p, kv_rows_s, win_new_p,
            win_new_s.reshape(state_win.shape))
```

```python
import functools

import numpy as np
import jax
import jax.numpy as jnp
from jax import lax
from jax.experimental import pallas as pl
from jax.experimental.pallas import tpu as pltpu

D_MODEL = 1024
POOL_WINDOWS = (2, 4, 8, 16)
POOL_GROUP_DIM = D_MODEL // len(POOL_WINDOWS)
POOL_BUF = max(POOL_WINDOWS) - 1
N_HEADS = 16
HEAD_DIM = 64
N_KV_HEADS = 4
Q_PER_KV = N_HEADS // N_KV_HEADS
N_BRANCH = 3
CMP_BLOCK = 32
CMP_STRIDE = 16
CMP_HIDDEN = 2 * HEAD_DIM
SEL_BLOCK = 64
N_SELECT = 16
WINDOW = 512
RMS_EPS = 1e-6
FORCED_PRIORITY = 1e6
KV_WIDTH = N_KV_HEADS * HEAD_DIM
ROW_WIDTH = 2 * 2 * KV_WIDTH

NEG = -1e30
F32 = jnp.float32
BF16 = jnp.bfloat16
MIB = 1024 * 1024


def _params(semantics, vmem_mib):
    return pltpu.CompilerParams(dimension_semantics=semantics, vmem_limit_bytes=vmem_mib * MIB)


def _rms(x, g):
    return x * lax.rsqrt(jnp.mean(x * x, axis=-1, keepdims=True) + RMS_EPS) * g


def _dot(a, b):
    return jnp.dot(a, b, preferred_element_type=F32)


def _dot_nt(a, b):
    return lax.dot_general(a, b, (((1,), (1,)), ((), ())), preferred_element_type=F32)


def _split_bf16(x):
    hi = x.astype(BF16)
    lo = (x - hi.astype(F32)).astype(BF16)
    return hi, lo


def _sigmoid(x):
    return 1.0 / (1.0 + jnp.exp(-x))


def _pool_prompt_kernel(x_ref, xp_ref, g_ref, w_ref, sc_ref, y_ref, ul_ref, u_scr, *, tt):
    i = pl.program_id(1)
    g = g_ref[...]
    x = x_ref[0]
    u = _rms(x, g)
    up = _rms(xp_ref[0], g) * (i > 0).astype(F32)
    u_scr[0:16, :] = up
    u_scr[16:16 + tt, :] = u
    t = i * tt + lax.broadcasted_iota(jnp.int32, (tt, 1), 0)
    for gi, win in enumerate(POOL_WINDOWS):
        cols = slice(gi * POOL_GROUP_DIM, (gi + 1) * POOL_GROUP_DIM)
        acc = u[:, cols]
        for k in range(1, win):
            acc = acc + u_scr[16 - k:16 - k + tt, cols]
        cnt = jnp.minimum(t + 1, win).astype(F32)
        diff = acc / cnt - u[:, cols]
        yg = _dot(diff.astype(BF16), w_ref[gi])
        y_ref[0, :, cols] = x[:, cols] + yg * sc_ref[:, cols]
    ul_ref[0] = u[tt - 16:, :]


def _pool_prompt(x, g, w_bf, scale, *, tt=512):
    b, t, d = x.shape
    n_t = t // tt
    kern = functools.partial(_pool_prompt_kernel, tt=tt)
    return pl.pallas_call(
        kern,
        grid=(b, n_t),
        in_specs=[
            pl.BlockSpec((1, tt, d), lambda bi, i: (bi, i, 0)),
            pl.BlockSpec((1, 16, d), lambda bi, i: (bi, jnp.maximum(i * (tt // 16) - 1, 0), 0)),
            pl.BlockSpec((1, d), lambda bi, i: (0, 0)),
            pl.BlockSpec((len(POOL_WINDOWS), POOL_GROUP_DIM, POOL_GROUP_DIM), lambda bi, i: (0, 0, 0)),
            pl.BlockSpec((1, d), lambda bi, i: (0, 0)),
        ],
        out_specs=[
            pl.BlockSpec((1, tt, d), lambda bi, i: (bi, i, 0)),
            pl.BlockSpec((1, 16, d), lambda bi, i: (bi, 0, 0)),
        ],
        out_shape=[jax.ShapeDtypeStruct((b, t, d), F32), jax.ShapeDtypeStruct((b, 16, d), F32)],
        scratch_shapes=[pltpu.VMEM((16 + tt, d), F32)],
        compiler_params=_params(("parallel", "arbitrary"), 32),
        name="pool_prompt",
    )(x, x, g, w_bf, scale)


def _pool_sample_kernel(x_ref, past_ref, g_ref, w_ref, sc_ref, y_ref, u_ref):
    x = x_ref[...]
    u = _rms(x, g_ref[...])
    for gi, win in enumerate(POOL_WINDOWS):
        cols = slice(gi * POOL_GROUP_DIM, (gi + 1) * POOL_GROUP_DIM)
        acc = u[:, cols]
        for k in range(1, win):
            acc = acc + past_ref[:, POOL_BUF - k, cols]
        diff = acc / float(win) - u[:, cols]
        yg = _dot(diff.astype(BF16), w_ref[gi])
        y_ref[:, cols] = x[:, cols] + yg * sc_ref[:, cols]
    u_ref[...] = u


def _pool_sample(x, past, g, w_bf, scale, *, bt=32):
    nb, d = x.shape
    return pl.pallas_call(
        _pool_sample_kernel,
        grid=(nb // bt,),
        in_specs=[
            pl.BlockSpec((bt, d), lambda i: (i, 0)),
            pl.BlockSpec((bt, POOL_BUF, d), lambda i: (i, 0, 0)),
            pl.BlockSpec((1, d), lambda i: (0, 0)),
            pl.BlockSpec((len(POOL_WINDOWS), POOL_GROUP_DIM, POOL_GROUP_DIM), lambda i: (0, 0, 0)),
            pl.BlockSpec((1, d), lambda i: (0, 0)),
        ],
        out_specs=[pl.BlockSpec((bt, d), lambda i: (i, 0)), pl.BlockSpec((bt, d), lambda i: (i, 0))],
        out_shape=[jax.ShapeDtypeStruct((nb, d), F32), jax.ShapeDtypeStruct((nb, d), F32)],
        compiler_params=_params(("parallel",), 32),
        name="pool_sample",
    )(x, past, g, w_bf, scale)


def _mlp_kernel(*refs, final_norm, pre_proj):
    if pre_proj:
        x_ref, a_ref, wo_ref, g_ref, wu_ref, wd_ref, gf_ref, o_ref, xn_scr, acc_scr = refs
    else:
        x_ref, g_ref, wu_ref, wd_ref, gf_ref, o_ref, xn_scr, acc_scr = refs
    j = pl.program_id(1)

    @pl.when(j == 0)
    def _():
        x = x_ref[...]
        if pre_proj:
            x = x + _dot(a_ref[...].astype(BF16), wo_ref[...])
            o_ref[...] = x
        xn_scr[...] = _rms(x, g_ref[...]).astype(BF16)
        acc_scr[...] = jnp.zeros_like(acc_scr)

    h = jnp.maximum(_dot(xn_scr[...], wu_ref[...]), 0.0)
    acc_scr[...] += _dot((h * h).astype(BF16), wd_ref[...])

    @pl.when(j == pl.num_programs(1) - 1)
    def _():
        r = (o_ref[...] if pre_proj else x_ref[...]) + acc_scr[...]
        o_ref[...] = _rms(r, gf_ref[...]) if final_norm else r


def _mlp(x, g, wu_bf, wd_bf, gf, *, final_norm, tm, tf=1024, attn=None, wo_bf=None, vmem_mib=48):
    m, d = x.shape
    f = wu_bf.shape[1]
    pre_proj = attn is not None
    kern = functools.partial(_mlp_kernel, final_norm=final_norm, pre_proj=pre_proj)
    rows = pl.BlockSpec((tm, d), lambda i, j: (i, 0))
    vec = pl.BlockSpec((1, d), lambda i, j: (0, 0))
    pre_specs = [pl.BlockSpec((tm, attn.shape[1]), lambda i, j: (i, 0)),
                 pl.BlockSpec(wo_bf.shape, lambda i, j: (0, 0))] if pre_proj else []
    pre_args = [attn, wo_bf] if pre_proj else []
    return pl.pallas_call(
        kern,
        grid=(m // tm, f // tf),
        in_specs=[rows] + pre_specs + [
            vec,
            pl.BlockSpec((d, tf), lambda i, j: (0, j)),
            pl.BlockSpec((tf, d), lambda i, j: (j, 0)),
            vec,
        ],
        out_specs=rows,
        out_shape=jax.ShapeDtypeStruct((m, d), F32),
        scratch_shapes=[pltpu.VMEM((tm, d), BF16), pltpu.VMEM((tm, d), F32)],
        compiler_params=_params(("parallel", "arbitrary"), vmem_mib),
        name="mlp",
    )(x, *pre_args, g, wu_bf, wd_bf, gf)


def _proj_kernel(x_ref, gkv_ref, gq_ref, wkv_ref, wqg_ref, bqg_ref, pc_ref, oh_ref,
                 rows_ref, win_ref, q_ref, gate_ref, ks_ref, vs_ref, kw_ref, vw_ref):
    x = x_ref[...]
    xh = x * lax.rsqrt(jnp.mean(x * x, axis=-1, keepdims=True) + RMS_EPS)
    hkv = _dot((xh * gkv_ref[...]).astype(BF16), wkv_ref[...])
    hq = _dot((xh * gq_ref[...]).astype(BF16), wqg_ref[...]) + bqg_ref[...]
    n_q = N_HEADS * HEAD_DIM
    rows_ref[...] = hkv[:, :ROW_WIDTH]
    win_ref[...] = hkv[:, ROW_WIDTH:]
    q_ref[...] = hq[:, :n_q]
    gate_ref[...] = hq[:, n_q:]

    lane = lax.broadcasted_iota(jnp.int32, (x.shape[0], 128), 1)
    pc = pc_ref[...]
    ones = (lane == HEAD_DIM).astype(F32)

    def head(col0, gh, filler):
        pair = hkv[:, col0 + (gh // 2) * 128:col0 + (gh // 2 + 1) * 128]
        if gh % 2:
            pair = pltpu.roll(pair, HEAD_DIM, 1)
        return jnp.where(lane < HEAD_DIM, pair, filler).astype(BF16)

    for gh in range(N_KV_HEADS):
        ks_ref[0, gh, :, 0:128] = head(2 * KV_WIDTH, gh, pc)
        ks_ref[0, gh, :, 128:256] = oh_ref[...]
        vs_ref[0, gh] = head(3 * KV_WIDTH, gh, ones)
        kw_ref[0, gh] = head(4 * KV_WIDTH, gh, pc)
        vw_ref[0, gh] = head(5 * KV_WIDTH, gh, ones)


def _proj(x, g_kv, g_q, wkv_bf, wqg_bf, bqg, pos_cols, onehot, *, b, t, tm=512):
    m, d = x.shape
    nt = t // tm
    n_kv = wkv_bf.shape[1]
    n_qg = wqg_bf.shape[1]
    n_q = N_HEADS * HEAD_DIM
    const = lambda *shape: pl.BlockSpec(shape, lambda i: (0,) * len(shape))
    rows_of = lambda n: pl.BlockSpec((tm, n), lambda i: (i, 0))
    heads_of = lambda n: pl.BlockSpec((1, N_KV_HEADS, tm, n), lambda i: (i // nt, 0, i % nt, 0))
    heads_shape = lambda n: jax.ShapeDtypeStruct((b, N_KV_HEADS, t, n), BF16)
    return pl.pallas_call(
        _proj_kernel,
        grid=(m // tm,),
        in_specs=[rows_of(d), const(1, d), const(1, d), const(d, n_kv), const(d, n_qg), const(1, n_qg),
                  pl.BlockSpec((tm, 128), lambda i: (i % nt, 0)), pl.BlockSpec((tm, 128), lambda i: (i % nt, 0))],
        out_specs=[rows_of(ROW_WIDTH), rows_of(n_kv - ROW_WIDTH), rows_of(n_q), rows_of(n_qg - n_q),
                   heads_of(256), heads_of(128), heads_of(128), heads_of(128)],
        out_shape=[jax.ShapeDtypeStruct((m, ROW_WIDTH), F32), jax.ShapeDtypeStruct((m, n_kv - ROW_WIDTH), F32),
                   jax.ShapeDtypeStruct((m, n_q), F32), jax.ShapeDtypeStruct((m, n_qg - n_q), F32),
                   heads_shape(256), heads_shape(128), heads_shape(128), heads_shape(128)],
        compiler_params=_params(("parallel",), 56),
        name="proj",
    )(x, g_kv, g_q, wkv_bf, wqg_bf, bqg, pos_cols, onehot)


def _lin_kernel(x_ref, g_ref, w_ref, b_ref, *o_refs, splits):
    h = _dot(_rms(x_ref[...], g_ref[...]).astype(BF16), w_ref[...]) + b_ref[...]
    c0 = 0
    for o_ref, n in zip(o_refs, splits):
        o_ref[...] = h[:, c0:c0 + n]
        c0 += n


def _lin(x, g, w_bf, b, *, splits, tm):
    m, k = x.shape
    n = w_bf.shape[1]
    return pl.pallas_call(
        functools.partial(_lin_kernel, splits=splits),
        grid=(m // tm,),
        in_specs=[pl.BlockSpec((tm, k), lambda i: (i, 0)), pl.BlockSpec((1, k), lambda i: (0, 0)),
                  pl.BlockSpec((k, n), lambda i: (0, 0)), pl.BlockSpec((1, n), lambda i: (0, 0))],
        out_specs=[pl.BlockSpec((tm, s), lambda i: (i, 0)) for s in splits],
        out_shape=[jax.ShapeDtypeStruct((m, s), F32) for s in splits],
        compiler_params=_params(("parallel",), 48),
        name="lin",
    )(x, g, w_bf, b)


def _compress_kernel(pt_ref, *refs, n_page_refs, paged, rows_per_ref, n_sub):
    del pt_ref
    page_refs = refs[:n_page_refs]
    w1d_ref, w1_ref, pe_ref, w2_ref, kc_ref, vc_ref = refs[n_page_refs:]
    n_take = rows_per_ref // CMP_STRIDE

    def tap_rows(c, j):
        take = pl.ds(j, n_take, stride=CMP_STRIDE)
        if paged:
            return jnp.concatenate([r[0, c, take, :] for r in page_refs], axis=0)
        return page_refs[c][0, take, :]

    for kv, o_ref in enumerate((kc_ref, vc_ref)):
        pew = _dot(pe_ref[kv].astype(BF16), w1_ref[kv])[0:1]
        for c2 in range(2):
            acc = None
            for jp in range(CMP_STRIDE // 2):
                lhs = jnp.concatenate([tap_rows(2 * kv + c2, 2 * jp), tap_rows(2 * kv + c2, 2 * jp + 1)], axis=1)
                d = _dot(lhs.astype(BF16), w1d_ref[kv, jp])
                acc = d if acc is None else acc + d
            for e in range(2):
                a = acc[:, 2 * e * CMP_HIDDEN:(2 * e + 1) * CMP_HIDDEN]
                b2 = acc[:, (2 * e + 1) * CMP_HIDDEN:(2 * e + 2) * CMP_HIDDEN]
                hid = a + pltpu.roll(b2, a.shape[0] - 1, 0) + pew
                act = hid * _sigmoid(hid)
                out = _dot(act.astype(BF16), w2_ref[kv])
                for k in range(out.shape[0] // n_sub):
                    o_ref[k, 2 * c2 + e] = out[k * n_sub:(k + 1) * n_sub]


def _compress(pages, page_table, w1d_bf, w1_bf, pe8, w2_bf, *, paged):
    if paged:
        nb, n_pages = page_table.shape
        rows_per_ref = pages.shape[2]
        n_sub = n_pages * rows_per_ref // CMP_STRIDE
        seqs = 4
        page_specs = [
            pl.BlockSpec((1, 4, rows_per_ref, 128),
                         functools.partial(lambda k, p, b, pt: (pt[seqs * b + k, p], 0, 0, 0), k, p))
            for k in range(seqs) for p in range(n_pages)]
    else:
        nb, rows_per_ref, _ = pages.shape
        n_sub = rows_per_ref // CMP_STRIDE
        seqs = 1
        page_specs = [pl.BlockSpec((1, rows_per_ref, 128), functools.partial(lambda c, b, pt: (b, 0, c), c))
                      for c in range(4)]
    page_args = [pages] * len(page_specs)
    kern = functools.partial(_compress_kernel, n_page_refs=len(page_args), paged=paged, rows_per_ref=rows_per_ref,
                             n_sub=n_sub)
    const = lambda *shape: pl.BlockSpec(shape, lambda b, pt: (0,) * len(shape))
    out_spec = pl.BlockSpec((seqs, N_KV_HEADS, n_sub, HEAD_DIM), lambda b, pt: (b, 0, 0, 0))
    return pl.pallas_call(
        kern,
        grid_spec=pltpu.PrefetchScalarGridSpec(
            num_scalar_prefetch=1,
            grid=(nb // seqs,),
            in_specs=page_specs + [
                const(2, CMP_STRIDE // 2, 4 * HEAD_DIM, 4 * CMP_HIDDEN),
                const(2, CMP_BLOCK * HEAD_DIM, CMP_HIDDEN),
                const(2, 8, CMP_BLOCK * HEAD_DIM),
                const(2, CMP_HIDDEN, HEAD_DIM),
            ],
            out_specs=[out_spec, out_spec],
        ),
        out_shape=[jax.ShapeDtypeStruct((nb, N_KV_HEADS, n_sub, HEAD_DIM), F32)] * 2,
        compiler_params=_params(("parallel",), 48),
        name="compress",
    )(page_table, *page_args, w1d_bf, w1_bf, pe8, w2_bf)


def _attn_prompt_kernel(q_ref, gt_ref, st_ref, kc_ref, vc_ref, ks_ref, vs_ref, kw_ref, vw_ref, cm_ref, dm_ref,
                        dc_ref, wb_ref, o_ref, qa_scr, rank_scr, m_scr, ala_scr, alb_scr, sa_scr, sb_scr, pa_scr,
                        pb_scr, acc_scr, *, tq, tk, rc, n_sel):
    c0 = pl.program_id(2) * tq
    rows = Q_PER_KV * tq
    band = WINDOW + tq

    lane = lax.broadcasted_iota(jnp.int32, (tq, 128), 1)
    qb = q_ref[0] * (HEAD_DIM ** -0.5)
    for r in range(Q_PER_KV):
        pair = qb[:, (r // 2) * 128:(r // 2 + 1) * 128]
        if r % 2:
            pair = pltpu.roll(pair, HEAD_DIM, 1)
        qa_scr[r * tq:(r + 1) * tq, 0:128] = jnp.where(lane < HEAD_DIM, pair, st_ref[0, r]).astype(BF16)
    ql = qa_scr[:, 0:128]

    n_cmp_pad = kc_ref.shape[2]
    s = _dot_nt(ql, kc_ref[0, 0])
    mask = jnp.concatenate([dc_ref[...]] * Q_PER_KV, axis=0) <= c0
    s = jnp.where(mask, s, NEG)
    m = jnp.max(s, axis=-1, keepdims=True)
    e = jnp.where(mask, jnp.exp(s - m), 0.0)
    p = e / jnp.maximum(jnp.sum(e, axis=-1, keepdims=True), 1e-30)
    o_c = _dot(p.astype(BF16), vc_ref[0, 0])

    w0 = pl.multiple_of(jnp.maximum(c0 - WINDOW, 0), 128)
    sw = _dot_nt(ql, kw_ref[0, 0, pl.ds(w0, band), :]) + jnp.concatenate([wb_ref[0]] * Q_PER_KV, axis=0)
    pw = jnp.exp(sw - jnp.max(sw, axis=-1, keepdims=True))
    aw = _dot(pw.astype(BF16), vw_ref[0, 0, pl.ds(w0, band), :])
    o_w = aw[:, :HEAD_DIM] / aw[:, HEAD_DIM:HEAD_DIM + 1]

    gate = _sigmoid(gt_ref[0])

    def heads_to_lanes(per_head):
        return jnp.concatenate([per_head(r, slice(r * tq, (r + 1) * tq)) for r in range(Q_PER_KV)], axis=1)

    o_ref[0] = heads_to_lanes(lambda r, rs: gate[:, 3 * r:3 * r + 1] * o_c[rs] + gate[:, 3 * r + 2:3 * r + 3] * o_w[rs])

    psum = p[0:tq] + p[tq:2 * tq] + p[2 * tq:3 * tq] + p[3 * tq:4 * tq]
    hi, lo = _split_bf16(psum)
    imp = _dot_nt(cm_ref[...], hi) + _dot_nt(cm_ref[...], lo)
    blk = lax.broadcasted_iota(jnp.int32, (n_sel, tq), 0)
    tcol = c0 + lax.broadcasted_iota(jnp.int32, (n_sel, tq), 1)
    cur = tcol >> (SEL_BLOCK.bit_length() - 1)
    forced = (blk == 0) | (blk == cur) | (blk == cur - 1)
    valid = blk * SEL_BLOCK <= tcol
    pri = jnp.where(valid, jnp.where(forced, FORCED_PRIORITY, imp), -1.0)
    rank_scr[...] = jnp.zeros_like(rank_scr)
    sub8 = lax.broadcasted_iota(jnp.int32, (8, tq), 0)
    for g8 in range(n_sel // 8):
        lo8, hi8 = 8 * g8, 8 * g8 + 8

        def walk_group(lo8=lo8, hi8=hi8):
            mid = pri[lo8:hi8]
            below = above = None
            inside = jnp.zeros((8, tq), jnp.int32)
            for i in range(8):
                row = mid[i:i + 1, :]
                if lo8:
                    b_i = (row > pri[:lo8]).astype(jnp.int32)
                    below = b_i if below is None else below + b_i
                inside = inside + ((row > mid) | ((row == mid) & (sub8 > i))).astype(jnp.int32)
                if hi8 < n_sel:
                    a_i = (row >= pri[hi8:]).astype(jnp.int32)
                    above = a_i if above is None else above + a_i
            parts = [part for part in (below, inside, above) if part is not None]
            rank_scr[...] += jnp.concatenate(parts, axis=0)

        walk_group()

    bias = jnp.where(rank_scr[...] < min(N_SELECT, n_sel), 0.0, NEG).T
    right = jnp.concatenate([bias, jnp.zeros((tq, 128 - n_sel), F32)], axis=1).astype(BF16)
    for r in range(Q_PER_KV):
        qa_scr[r * tq:(r + 1) * tq, 128:256] = right

    m_scr[...] = jnp.full_like(m_scr, NEG)
    acc_scr[...] = jnp.zeros_like(acc_scr)

    def scores(j):
        return _dot_nt(qa_scr[...], ks_ref[0, 0, pl.ds(pl.multiple_of(j * tk, tk), tk), :])

    def soft_pv(s_scr, p_scr, al_scr, j, masked):
        k0 = pl.multiple_of(j * tk, tk)
        for i in range(rows // rc):
            rs = slice(i * rc, (i + 1) * rc)
            qs = (i * rc) % tq
            sc = s_scr[rs, :]
            if masked:
                sc = jnp.where(dm_ref[qs:qs + rc, :] <= c0 - k0, sc, NEG)
            m_old = m_scr[rs]
            m_new = jnp.maximum(m_old, jnp.max(sc, axis=-1, keepdims=True))
            al_scr[rs] = jnp.exp(m_old - m_new)
            p_scr[rs] = jnp.exp(sc - m_new).astype(BF16)
            m_scr[rs] = m_new
        acc_scr[...] = al_scr[...] * acc_scr[...] + _dot(p_scr[...], vs_ref[0, 0, pl.ds(k0, tk), :])

    n_tiles = c0 // tk + 1
    n_loop = (n_tiles - 1) // 2
    sa_scr[...] = scores(0)

    def tile_pair(jj, carry):
        sb_scr[...] = scores(2 * jj + 1)
        soft_pv(sa_scr, pa_scr, ala_scr, 2 * jj, False)
        sa_scr[...] = scores(2 * jj + 2)
        soft_pv(sb_scr, pb_scr, alb_scr, 2 * jj + 1, False)
        return carry

    lax.fori_loop(0, n_loop, tile_pair, 0)

    @pl.when(n_tiles % 2 == 0)
    def _():
        sb_scr[...] = scores(2 * n_loop + 1)
        soft_pv(sa_scr, pa_scr, ala_scr, 2 * n_loop, False)
        soft_pv(sb_scr, pb_scr, alb_scr, 2 * n_loop + 1, True)

    @pl.when(n_tiles % 2 == 1)
    def _():
        soft_pv(sa_scr, pa_scr, ala_scr, 2 * n_loop, True)

    acc = acc_scr[...]
    o_s = acc[:, :HEAD_DIM] / acc[:, HEAD_DIM:HEAD_DIM + 1]
    gate_s = _sigmoid(gt_ref[0])
    o_ref[0] += heads_to_lanes(lambda r, rs: gate_s[:, 3 * r + 1:3 * r + 2] * o_s[rs])


def _attn_prompt(q, gate_pre, slope_tab, kc_aug, vc, ks_aug, vs, kw_aug, vw, cmat_t, *, tq, tk=512, rc=128):
    b, t, _ = q.shape
    n_sel = t // SEL_BLOCK
    n_cmp_pad = kc_aug.shape[2]
    kern = functools.partial(_attn_prompt_kernel, tq=tq, tk=tk, rc=rc, n_sel=n_sel)
    rows = Q_PER_KV * tq
    band = WINDOW + tq
    qi = np.arange(tq)[:, None]
    dmat = jnp.asarray(np.arange(tk)[None, :] - qi, jnp.int32)
    dcmp = jnp.asarray(CMP_STRIDE * np.arange(n_cmp_pad)[None, :] + (CMP_BLOCK - 1) - qi, jnp.int32)
    dist = (np.minimum(np.arange(WINDOW // tq + 1) * tq, WINDOW)[:, None, None] + qi[None]
            - np.arange(band)[None, None, :])
    wbias = jnp.asarray(np.where((dist >= 0) & (dist < WINDOW), 0.0, NEG), F32)
    per_bg = lambda *shape: pl.BlockSpec((1, 1) + shape, lambda bi, gi, ci: (bi, gi, 0, 0))
    return pl.pallas_call(
        kern,
        grid=(b, N_KV_HEADS, t // tq),
        in_specs=[
            pl.BlockSpec((1, tq, KV_WIDTH), lambda bi, gi, ci: (bi, ci, gi)),
            pl.BlockSpec((1, tq, 128), lambda bi, gi, ci: (bi, ci, gi)),
            pl.BlockSpec((1, Q_PER_KV, tq, 128), lambda bi, gi, ci: (gi, 0, 0, 0)),
            per_bg(n_cmp_pad, 128),
            per_bg(n_cmp_pad, HEAD_DIM),
            per_bg(t, 256),
            per_bg(t, 128),
            per_bg(t, 128),
            per_bg(t, 128),
            pl.BlockSpec((n_sel, n_cmp_pad), lambda bi, gi, ci: (0, 0)),
            pl.BlockSpec((tq, tk), lambda bi, gi, ci: (0, 0)),
            pl.BlockSpec((tq, n_cmp_pad), lambda bi, gi, ci: (0, 0)),
            pl.BlockSpec((1, tq, band), lambda bi, gi, ci: (jnp.minimum(ci, WINDOW // tq), 0, 0)),
        ],
        out_specs=pl.BlockSpec((1, tq, KV_WIDTH), lambda bi, gi, ci: (bi, ci, gi)),
        out_shape=jax.ShapeDtypeStruct((b, t, N_HEADS * HEAD_DIM), F32),
        scratch_shapes=[
            pltpu.VMEM((rows, 256), BF16),
            pltpu.VMEM((n_sel, tq), jnp.int32),
            pltpu.VMEM((rows, 1), F32),
            pltpu.VMEM((rows, 1), F32),
            pltpu.VMEM((rows, 1), F32),
            pltpu.VMEM((rows, tk), F32),
            pltpu.VMEM((rows, tk), F32),
            pltpu.VMEM((rows, tk), BF16),
            pltpu.VMEM((rows, tk), BF16),
            pltpu.VMEM((rows, 128), F32),
        ],
        compiler_params=_params(("parallel", "parallel", "arbitrary"), 48),
        name="attn_prompt",
    )(q, gate_pre, slope_tab, kc_aug, vc, ks_aug, vs, kw_aug, vw, cmat_t, dmat, dcmp, wbias)


def _softmax_lanes(s):
    e = jnp.exp(s - jnp.max(s, axis=-1, keepdims=True))
    return e / jnp.sum(e, axis=-1, keepdims=True)


def _attn_sample_kernel(pt_ref, *refs, n_pages, past_len, n_sel, seqs):
    del pt_ref
    page_refs = refs[5:5 + seqs * n_pages]
    for k in range(seqs):
        _attn_sample_one(k, *refs[:5], page_refs[k * n_pages:(k + 1) * n_pages], *refs[5 + seqs * n_pages:],
                         past_len=past_len, n_sel=n_sel)


def _attn_sample_one(k, q_ref, gt_ref, sl_ref, kc_ref, vc_ref, page_refs, new_ref, win_ref, ex_ref, cm_ref, rm_ref,
                     o_ref, wn_ref, *, past_len, n_sel):
    q = q_ref[k]
    sl = sl_ref[:, 0:1]
    new = new_ref[k]

    n_cmp_pad = kc_ref.shape[1]
    dist_c = (past_len - (CMP_BLOCK - 1)
              - CMP_STRIDE * lax.broadcasted_iota(jnp.int32, (N_HEADS, n_cmp_pad), 1)).astype(F32)
    s = _dot_nt(q, kc_ref[k]) - sl * dist_c
    mask = dist_c >= 0
    s = jnp.where(mask, s, NEG)
    e = jnp.where(mask, jnp.exp(s - jnp.max(s, axis=-1, keepdims=True)), 0.0)
    p_c = e / jnp.maximum(jnp.sum(e, axis=-1, keepdims=True), 1e-30)
    o_c = _dot(p_c.astype(BF16), vc_ref[k])

    hi, lo = _split_bf16(p_c)
    imp = _dot(hi, cm_ref[...]) + _dot(lo, cm_ref[...])
    hi, lo = _split_bf16(imp)
    imp = _dot(rm_ref[...], hi) + _dot(rm_ref[...], lo)
    n_blk = imp.shape[1]
    blk = lax.broadcasted_iota(jnp.int32, (N_HEADS, n_blk), 1)
    cur = past_len // SEL_BLOCK
    forced = (blk == 0) | (blk == cur) | (blk == cur - 1)
    valid = blk * SEL_BLOCK <= past_len
    pri = jnp.where(valid, jnp.where(forced, FORCED_PRIORITY, imp), -1.0)
    pri = jnp.where(blk < n_sel, pri, -2.0)
    rank = jnp.zeros((N_HEADS, n_blk), jnp.int32)
    for s2 in range(n_sel):
        col = pri[:, s2:s2 + 1]
        beats = (col > pri) | ((col == pri) & (blk > s2))
        rank = rank + beats.astype(jnp.int32)
    bias = jnp.where(rank < min(N_SELECT, n_sel), 0.0, NEG)

    k_sel = jnp.concatenate([jnp.concatenate([r[0, 0], r[0, 1]], axis=1) for r in page_refs],
                            axis=0).astype(BF16)
    v_sel = jnp.concatenate([jnp.concatenate([r[0, 2], r[0, 3]], axis=1) for r in page_refs],
                            axis=0).astype(BF16)
    dist_s = (past_len - lax.broadcasted_iota(jnp.int32, (N_HEADS, past_len), 1)).astype(F32)
    s = _dot_nt(q, k_sel) - sl * dist_s + _dot(bias.astype(BF16), ex_ref[...])
    qf = q.astype(F32)
    k_new = new[:, 2 * KV_WIDTH:3 * KV_WIDTH].astype(BF16).astype(F32)
    v_new = new[:, 3 * KV_WIDTH:4 * KV_WIDTH].astype(BF16).astype(F32)
    s_new = jnp.sum(qf * k_new, axis=-1, keepdims=True) + bias[:, past_len // SEL_BLOCK:past_len // SEL_BLOCK + 1]
    m = jnp.maximum(jnp.max(s, axis=-1, keepdims=True), s_new)
    e = jnp.exp(s - m)
    e_new = jnp.exp(s_new - m)
    l = jnp.sum(e, axis=-1, keepdims=True) + e_new
    o_s = (_dot(e.astype(BF16), v_sel) + e_new.astype(BF16).astype(F32) * v_new) / l

    wb = win_ref.shape[1]
    rolled = pltpu.roll(win_ref[k], wb - 1, 0)
    rowi = lax.broadcasted_iota(jnp.int32, rolled.shape, 0)
    wn = jnp.where(rowi == wb - 1, new[:, 4 * KV_WIDTH:6 * KV_WIDTH], rolled)
    wn_ref[k] = wn
    dist_w = (wb - 1 - lax.broadcasted_iota(jnp.int32, (N_HEADS, wb), 1)).astype(F32)
    p_w = _softmax_lanes(_dot_nt(q, wn[:, 0:KV_WIDTH].astype(BF16)) - sl * dist_w)
    o_w = _dot(p_w.astype(BF16), wn[:, KV_WIDTH:2 * KV_WIDTH].astype(BF16))

    gate = _sigmoid(gt_ref[k])
    o_ref[k] = gate[:, 0:1] * o_c + gate[:, 1:2] * o_s + gate[:, 2:3] * o_w


def _attn_sample(qbd, gate16, slope16, kc_all, vc_all, pages, page_table, new_rows, state_win, expand, cmat, rmat):
    nb, n_pages = page_table.shape
    page_rows = pages.shape[2]
    past_len = n_pages * page_rows
    n_sel = -(-(past_len + 1) // SEL_BLOCK)
    wb = state_win.shape[1]
    n_cmp_pad = kc_all.shape[1]
    seqs = 2
    kern = functools.partial(_attn_sample_kernel, n_pages=n_pages, past_len=past_len, n_sel=n_sel, seqs=seqs)
    per_b = lambda *shape: pl.BlockSpec((seqs,) + shape, lambda b, pt: (b,) + (0,) * len(shape))
    const = lambda *shape: pl.BlockSpec(shape, lambda b, pt: (0,) * len(shape))
    page_specs = [
        pl.BlockSpec((1, 4, page_rows, 128),
                     functools.partial(lambda k, p, b, pt: (pt[seqs * b + k, p], 1, 0, 0), k, p))
        for k in range(seqs) for p in range(n_pages)]
    return pl.pallas_call(
        kern,
        grid_spec=pltpu.PrefetchScalarGridSpec(
            num_scalar_prefetch=1,
            grid=(nb // seqs,),
            in_specs=[per_b(N_HEADS, KV_WIDTH), per_b(N_HEADS, 128), const(N_HEADS, 128),
                      per_b(n_cmp_pad, KV_WIDTH), per_b(n_cmp_pad, KV_WIDTH)]
            + page_specs
            + [per_b(1, 6 * KV_WIDTH), per_b(wb, 2 * KV_WIDTH), const(*expand.shape), const(*cmat.shape),
               const(*rmat.shape)],
            out_specs=[per_b(N_HEADS, KV_WIDTH), per_b(wb, 2 * KV_WIDTH)],
        ),
        out_shape=[jax.ShapeDtypeStruct((nb, N_HEADS, KV_WIDTH), F32),
                   jax.ShapeDtypeStruct((nb, wb, 2 * KV_WIDTH), F32)],
        compiler_params=_params(("parallel",), 48),
        name="attn_sample",
    )(page_table, qbd, gate16, slope16, kc_all, vc_all, *([pages] * (seqs * n_pages)), new_rows, state_win, expand,
      cmat, rmat)


def _alibi_slopes():
    return jnp.exp2(-8.0 * (jnp.arange(N_HEADS, dtype=F32) + 1.0) / N_HEADS)


def _cmp_to_sel(n_cmp_pad, n_cmp, n_sel_pad):
    m = np.zeros((n_cmp_pad, n_sel_pad), np.float32)
    for n in range(n_cmp):
        for k in range(CMP_BLOCK // CMP_STRIDE):
            m[n, (n + k) * CMP_STRIDE // SEL_BLOCK] += 1.0
    return m


def _pos_cols(pos, width):
    cols = np.zeros((pos.shape[0], width), np.float32)
    cols[:, 0] = cols[:, 1] = pos % 64
    cols[:, 2] = cols[:, 3] = pos // 64
    return cols


def kernel(x_prompt, x_sample, state_pool, cache_kv_pages, state_win, page_table, norm_mix, norm_mlp, w_up, w_down,
           pool_w, pool_scale, norm_kv, w_kv, cmp_pe, cmp_w1, cmp_w2, w_qg, b_gate, w_o, norm_final):
    b, t, d = x_prompt.shape
    nb = x_sample.shape[0]
    n_phys, page_rows = cache_kv_pages.shape[:2]
    n_q = N_HEADS * HEAD_DIM
    row = lambda v: v.reshape(1, -1)

    wu = w_up.astype(BF16)
    wd = w_down.astype(BF16)
    pw = pool_w[0].astype(BF16)
    wkv = w_kv.astype(BF16)
    wo = w_o[0].astype(BF16)
    hh = np.arange(N_HEADS)
    gcols = ((hh // Q_PER_KV) * 128 + (hh % Q_PER_KV) * N_BRANCH)[:, None] + np.arange(N_BRANCH)[None, :]
    gcols = gcols.reshape(-1)
    wg = jnp.zeros((d, N_KV_HEADS * 128), F32).at[:, gcols].set(w_qg[0][:, n_q:])
    wqg = jnp.concatenate([w_qg[0][:, :n_q], wg], axis=1).astype(BF16)
    bqg = jnp.zeros((1, n_q + N_KV_HEADS * 128), F32).at[0, n_q + gcols].set(b_gate[0])
    w1_bf = cmp_w1.astype(BF16)
    w1r = w1_bf.reshape(2, 2, CMP_STRIDE, HEAD_DIM, CMP_HIDDEN)
    w1c = jnp.concatenate([w1r[:, 0], w1r[:, 1]], axis=-1)
    w1d = jnp.concatenate([jnp.concatenate([w1c, jnp.zeros_like(w1c)], axis=-1),
                           jnp.concatenate([jnp.zeros_like(w1c), w1c], axis=-1)], axis=2)
    w1d = w1d.reshape(2, CMP_STRIDE // 2, 4 * HEAD_DIM, 4 * CMP_HIDDEN)
    pe8 = jnp.broadcast_to(cmp_pe.reshape(2, 1, CMP_BLOCK * HEAD_DIM), (2, 8, CMP_BLOCK * HEAD_DIM))
    w2_bf = cmp_w2.astype(BF16)

    slopes = _alibi_slopes()
    s_hi = slopes.astype(BF16).astype(F32)
    s_lo = (slopes - s_hi).astype(BF16).astype(F32)
    slope_cols = jnp.stack([s_hi, s_lo, 64.0 * s_hi, 64.0 * s_lo], axis=-1)

    x1, u_last = _pool_prompt(x_prompt, row(norm_mix[0]), pw, row(pool_scale[0]))
    pool_p = u_last[:, None, 16 - POOL_BUF:, :]
    x2 = _mlp(x1.reshape(b * t, d), row(norm_mlp[0]), wu[0], wd[0], row(norm_final), final_norm=False, tm=1024)
    tq = 256
    n_sub_p = t // CMP_STRIDE
    n_sel_p = t // SEL_BLOCK
    tpos = np.arange(t)
    pc_t = jnp.asarray(np.concatenate([np.zeros((t, HEAD_DIM), np.float32), _pos_cols(tpos, HEAD_DIM)], axis=1))
    onehot_t = jnp.asarray(np.eye(n_sel_p, 128, dtype=np.float32)[tpos // SEL_BLOCK], BF16)
    kv_rows, kv_win, q_p, gate_p, ks_aug, vs, kw_aug, vw = _proj(
        x2, row(norm_kv), row(norm_mix[1]), wkv, wqg, bqg, pc_t, onehot_t, b=b, t=t)
    kv_rows_p = kv_rows.reshape(b, t, 2, 2, N_KV_HEADS, HEAD_DIM)
    win_new_p = kv_win.reshape(b, t, 2, N_KV_HEADS, HEAD_DIM)[:, -min(WINDOW, t):]

    dummy_pt = jnp.zeros((b, 1), jnp.int32)
    kc_p, vc_p = _compress(kv_rows.reshape(b, t, ROW_WIDTH), dummy_pt, w1d, w1_bf, pe8, w2_bf, paged=False)
    cache_kv_pages, kc_p = lax.optimization_barrier((cache_kv_pages, kc_p))

    bcast = lambda a: jnp.broadcast_to(a[None, None], (b, N_KV_HEADS) + a.shape)
    pc_c = jnp.asarray(_pos_cols(CMP_STRIDE * np.arange(n_sub_p) + CMP_BLOCK - 1, HEAD_DIM), BF16)
    kc_aug = jnp.concatenate([kc_p.astype(BF16), bcast(pc_c)], axis=-1)
    cmat_t = jnp.asarray(_cmp_to_sel(n_sub_p, n_sub_p - 1, n_sel_p).T, BF16)
    slope_tab = jnp.zeros((N_KV_HEADS, Q_PER_KV, tq, 128), F32).at[:, :, :, HEAD_DIM:HEAD_DIM + 4].set(
        jnp.broadcast_to(slope_cols.reshape(N_KV_HEADS, Q_PER_KV, 1, 4), (N_KV_HEADS, Q_PER_KV, tq, 4)))
    o_p = _attn_prompt(q_p.reshape(b, t, n_q), gate_p.reshape(b, t, N_KV_HEADS * 128), slope_tab, kc_aug,
                       vc_p.astype(BF16), ks_aug, vs, kw_aug, vw, cmat_t, tq=tq)
    x_sample, state_pool, page_table, o_p = lax.optimization_barrier((x_sample, state_pool, page_table, o_p))
    y_prompt = _mlp(x2, row(norm_mlp[1]), wu[1], wd[1], row(norm_final), final_norm=True, tm=1024, tf=1024,
                    attn=o_p.reshape(b * t, n_q), wo_bf=wo, vmem_mib=60).reshape(b, t, d)

    xs0 = x_sample.reshape(nb, d)
    xs1, u_s = _pool_sample(xs0, state_pool[:, 0], row(norm_mix[0]), pw, row(pool_scale[0]))
    pool_s = jnp.concatenate([state_pool[:, 0, 1:], u_s[:, None]], axis=1)[:, None]
    xs2 = _mlp(xs1, row(norm_mlp[0]), wu[0], wd[0], row(norm_final), final_norm=False, tm=nb)
    kv_s, = _lin(xs2, row(norm_kv), wkv, jnp.zeros((1, wkv.shape[1]), F32), splits=(wkv.shape[1],), tm=nb)
    kv_rows_s = kv_s[:, :ROW_WIDTH].reshape(nb, 1, 2, 2, N_KV_HEADS, HEAD_DIM)

    pages = cache_kv_pages.reshape(n_phys, page_rows, ROW_WIDTH // 128, 128).transpose(0, 2, 1, 3)
    kc_s, vc_s = _compress(pages, page_table, w1d, w1_bf, pe8, w2_bf, paged=True)
    n_sub_s = kc_s.shape[2]
    all_heads = lambda a: a.transpose(0, 2, 1, 3).reshape(nb, n_sub_s, KV_WIDTH).astype(BF16)

    q_s, gate_s = _lin(xs2, row(norm_mix[1]), wqg, bqg, splits=(n_q, N_KV_HEADS * 128), tm=nb)
    q4 = q_s.reshape(nb, N_KV_HEADS, Q_PER_KV, HEAD_DIM) * (HEAD_DIM ** -0.5)
    qbd = jnp.einsum('bgrd,gh->bgrhd', q4, jnp.eye(N_KV_HEADS, dtype=F32)).reshape(nb, N_HEADS, KV_WIDTH)
    g16 = gate_s.reshape(nb, N_KV_HEADS, 128)[:, :, :Q_PER_KV * N_BRANCH].reshape(nb, N_HEADS, N_BRANCH)
    g16 = jnp.pad(g16, ((0, 0), (0, 0), (0, 128 - N_BRANCH)))
    slope16 = jnp.broadcast_to(slopes[:, None], (N_HEADS, 128))
    past_len = page_table.shape[1] * page_rows
    n_blk_pad = 64
    expand = jnp.asarray(np.eye(n_blk_pad, dtype=np.float32)[:, np.arange(past_len) // SEL_BLOCK], BF16)
    cmat_s = jnp.asarray(_cmp_to_sel(n_sub_s, n_sub_s - 1, n_blk_pad), BF16)
    rmat = jnp.asarray(np.kron(np.eye(N_KV_HEADS), np.ones((Q_PER_KV, Q_PER_KV))), BF16)
    o_s, win_new_s = _attn_sample(
        qbd.astype(BF16), g16, slope16, all_heads(kc_s), all_heads(vc_s), pages, page_table,
        kv_s.reshape(nb, 1, 6 * KV_WIDTH), state_win.reshape(nb, state_win.shape[1], 2 * KV_WIDTH),
        expand, cmat_s, rmat)
    o5 = o_s.reshape(nb, N_KV_HEADS, Q_PER_KV, N_KV_HEADS, HEAD_DIM)
    o_heads = jnp.stack([o5[:, gh, :, gh] for gh in range(N_KV_HEADS)], axis=1).reshape(nb, n_q)
    y_sample = _mlp(xs2, row(norm_mlp[1]), wu[1], wd[1], row(norm_final), final_norm=True, tm=nb,
                    attn=o_heads, wo_bf=wo).reshape(nb, 1, d)

    return (y_prompt, y_sample, pool_p, pool_s, kv_rows_p, kv_rows_s, win_new_p,
            win_new_s.reshape(state_win.shape))
```

```python
import functools

import numpy as np
import jax
import jax.numpy as jnp
from jax import lax
from jax.experimental import pallas as pl
from jax.experimental.pallas import tpu as pltpu

D_MODEL = 1024
POOL_WINDOWS = (2, 4, 8, 16)
POOL_GROUP_DIM = D_MODEL // len(POOL_WINDOWS)
POOL_BUF = max(POOL_WINDOWS) - 1
N_HEADS = 16
HEAD_DIM = 64
N_KV_HEADS = 4
Q_PER_KV = N_HEADS // N_KV_HEADS
N_BRANCH = 3
CMP_BLOCK = 32
CMP_STRIDE = 16
CMP_HIDDEN = 2 * HEAD_DIM
SEL_BLOCK = 64
N_SELECT = 16
WINDOW = 512
RMS_EPS = 1e-6
FORCED_PRIORITY = 1e6
KV_WIDTH = N_KV_HEADS * HEAD_DIM
ROW_WIDTH = 2 * 2 * KV_WIDTH

NEG = -1e30
F32 = jnp.float32
BF16 = jnp.bfloat16
MIB = 1024 * 1024


def _params(semantics, vmem_mib):
    return pltpu.CompilerParams(dimension_semantics=semantics, vmem_limit_bytes=vmem_mib * MIB)


def _rms(x, g):
    return x * lax.rsqrt(jnp.mean(x * x, axis=-1, keepdims=True) + RMS_EPS) * g


def _dot(a, b):
    return jnp.dot(a, b, preferred_element_type=F32)


def _dot_nt(a, b):
    return lax.dot_general(a, b, (((1,), (1,)), ((), ())), preferred_element_type=F32)


def _split_bf16(x):
    hi = x.astype(BF16)
    lo = (x - hi.astype(F32)).astype(BF16)
    return hi, lo


def _sigmoid(x):
    return 1.0 / (1.0 + jnp.exp(-x))


def _pool_prompt_kernel(x_ref, xp_ref, g_ref, w_ref, sc_ref, y_ref, ul_ref, u_scr, *, tt):
    i = pl.program_id(1)
    g = g_ref[...]
    x = x_ref[0]
    u = _rms(x, g)
    up = _rms(xp_ref[0], g) * (i > 0).astype(F32)
    u_scr[0:16, :] = up
    u_scr[16:16 + tt, :] = u
    t = i * tt + lax.broadcasted_iota(jnp.int32, (tt, 1), 0)
    for gi, win in enumerate(POOL_WINDOWS):
        cols = slice(gi * POOL_GROUP_DIM, (gi + 1) * POOL_GROUP_DIM)
        acc = u[:, cols]
        for k in range(1, win):
            acc = acc + u_scr[16 - k:16 - k + tt, cols]
        cnt = jnp.minimum(t + 1, win).astype(F32)
        diff = acc / cnt - u[:, cols]
        yg = _dot(diff.astype(BF16), w_ref[gi])
        y_ref[0, :, cols] = x[:, cols] + yg * sc_ref[:, cols]
    ul_ref[0] = u[tt - 16:, :]


def _pool_prompt(x, g, w_bf, scale, *, tt=512):
    b, t, d = x.shape
    n_t = t // tt
    kern = functools.partial(_pool_prompt_kernel, tt=tt)
    return pl.pallas_call(
        kern,
        grid=(b, n_t),
        in_specs=[
            pl.BlockSpec((1, tt, d), lambda bi, i: (bi, i, 0)),
            pl.BlockSpec((1, 16, d), lambda bi, i: (bi, jnp.maximum(i * (tt // 16) - 1, 0), 0)),
            pl.BlockSpec((1, d), lambda bi, i: (0, 0)),
            pl.BlockSpec((len(POOL_WINDOWS), POOL_GROUP_DIM, POOL_GROUP_DIM), lambda bi, i: (0, 0, 0)),
            pl.BlockSpec((1, d), lambda bi, i: (0, 0)),
        ],
        out_specs=[
            pl.BlockSpec((1, tt, d), lambda bi, i: (bi, i, 0)),
            pl.BlockSpec((1, 16, d), lambda bi, i: (bi, 0, 0)),
        ],
        out_shape=[jax.ShapeDtypeStruct((b, t, d), F32), jax.ShapeDtypeStruct((b, 16, d), F32)],
        scratch_shapes=[pltpu.VMEM((16 + tt, d), F32)],
        compiler_params=_params(("parallel", "arbitrary"), 32),
        name="pool_prompt",
    )(x, x, g, w_bf, scale)


def _pool_sample_kernel(x_ref, past_ref, g_ref, w_ref, sc_ref, y_ref, u_ref):
    x = x_ref[...]
    u = _rms(x, g_ref[...])
    for gi, win in enumerate(POOL_WINDOWS):
        cols = slice(gi * POOL_GROUP_DIM, (gi + 1) * POOL_GROUP_DIM)
        acc = u[:, cols]
        for k in range(1, win):
            acc = acc + past_ref[:, POOL_BUF - k, cols]
        diff = acc / float(win) - u[:, cols]
        yg = _dot(diff.astype(BF16), w_ref[gi])
        y_ref[:, cols] = x[:, cols] + yg * sc_ref[:, cols]
    u_ref[...] = u


def _pool_sample(x, past, g, w_bf, scale, *, bt=32):
    nb, d = x.shape
    return pl.pallas_call(
        _pool_sample_kernel,
        grid=(nb // bt,),
        in_specs=[
            pl.BlockSpec((bt, d), lambda i: (i, 0)),
            pl.BlockSpec((bt, POOL_BUF, d), lambda i: (i, 0, 0)),
            pl.BlockSpec((1, d), lambda i: (0, 0)),
            pl.BlockSpec((len(POOL_WINDOWS), POOL_GROUP_DIM, POOL_GROUP_DIM), lambda i: (0, 0, 0)),
            pl.BlockSpec((1, d), lambda i: (0, 0)),
        ],
        out_specs=[pl.BlockSpec((bt, d), lambda i: (i, 0)), pl.BlockSpec((bt, d), lambda i: (i, 0))],
        out_shape=[jax.ShapeDtypeStruct((nb, d), F32), jax.ShapeDtypeStruct((nb, d), F32)],
        compiler_params=_params(("parallel",), 32),
        name="pool_sample",
    )(x, past, g, w_bf, scale)


def _mlp_kernel(*refs, final_norm, pre_proj):
    if pre_proj:
        x_ref, a_ref, wo_ref, g_ref, wu_ref, wd_ref, gf_ref, o_ref, xn_scr, acc_scr = refs
    else:
        x_ref, g_ref, wu_ref, wd_ref, gf_ref, o_ref, xn_scr, acc_scr = refs
    j = pl.program_id(1)

    @pl.when(j == 0)
    def _():
        x = x_ref[...]
        if pre_proj:
            x = x + _dot(a_ref[...].astype(BF16), wo_ref[...])
            o_ref[...] = x
        xn_scr[...] = _rms(x, g_ref[...]).astype(BF16)
        acc_scr[...] = jnp.zeros_like(acc_scr)

    h = jnp.maximum(_dot(xn_scr[...], wu_ref[...]), 0.0)
    acc_scr[...] += _dot((h * h).astype(BF16), wd_ref[...])

    @pl.when(j == pl.num_programs(1) - 1)
    def _():
        r = (o_ref[...] if pre_proj else x_ref[...]) + acc_scr[...]
        o_ref[...] = _rms(r, gf_ref[...]) if final_norm else r


def _mlp(x, g, wu_bf, wd_bf, gf, *, final_norm, tm, tf=1024, attn=None, wo_bf=None, vmem_mib=48):
    m, d = x.shape
    f = wu_bf.shape[1]
    pre_proj = attn is not None
    kern = functools.partial(_mlp_kernel, final_norm=final_norm, pre_proj=pre_proj)
    rows = pl.BlockSpec((tm, d), lambda i, j: (i, 0))
    vec = pl.BlockSpec((1, d), lambda i, j: (0, 0))
    pre_specs = [pl.BlockSpec((tm, attn.shape[1]), lambda i, j: (i, 0)),
                 pl.BlockSpec(wo_bf.shape, lambda i, j: (0, 0))] if pre_proj else []
    pre_args = [attn, wo_bf] if pre_proj else []
    return pl.pallas_call(
        kern,
        grid=(m // tm, f // tf),
        in_specs=[rows] + pre_specs + [
            vec,
            pl.BlockSpec((d, tf), lambda i, j: (0, j)),
            pl.BlockSpec((tf, d), lambda i, j: (j, 0)),
            vec,
        ],
        out_specs=rows,
        out_shape=jax.ShapeDtypeStruct((m, d), F32),
        scratch_shapes=[pltpu.VMEM((tm, d), BF16), pltpu.VMEM((tm, d), F32)],
        compiler_params=_params(("parallel", "arbitrary"), vmem_mib),
        name="mlp",
    )(x, *pre_args, g, wu_bf, wd_bf, gf)


def _proj_kernel(x_ref, gkv_ref, gq_ref, wkv_ref, wqg_ref, bqg_ref, pc_ref, oh_ref,
                 rows_ref, win_ref, q_ref, gate_ref, ks_ref, vs_ref, kw_ref, vw_ref):
    x = x_ref[...]
    xh = x * lax.rsqrt(jnp.mean(x * x, axis=-1, keepdims=True) + RMS_EPS)
    hkv = _dot((xh * gkv_ref[...]).astype(BF16), wkv_ref[...])
    hq = _dot((xh * gq_ref[...]).astype(BF16), wqg_ref[...]) + bqg_ref[...]
    n_q = N_HEADS * HEAD_DIM
    rows_ref[...] = hkv[:, :ROW_WIDTH]
    win_ref[...] = hkv[:, ROW_WIDTH:]
    q_ref[...] = hq[:, :n_q]
    gate_ref[...] = hq[:, n_q:]

    lane = lax.broadcasted_iota(jnp.int32, (x.shape[0], 128), 1)
    pc = pc_ref[...]
    ones = (lane == HEAD_DIM).astype(F32)

    def head(col0, gh, filler):
        pair = hkv[:, col0 + (gh // 2) * 128:col0 + (gh // 2 + 1) * 128]
        if gh % 2:
            pair = pltpu.roll(pair, HEAD_DIM, 1)
        return jnp.where(lane < HEAD_DIM, pair, filler).astype(BF16)

    for gh in range(N_KV_HEADS):
        ks_ref[0, gh, :, 0:128] = head(2 * KV_WIDTH, gh, pc)
        ks_ref[0, gh, :, 128:256] = oh_ref[...]
        vs_ref[0, gh] = head(3 * KV_WIDTH, gh, ones)
        kw_ref[0, gh] = head(4 * KV_WIDTH, gh, pc)
        vw_ref[0, gh] = head(5 * KV_WIDTH, gh, ones)


def _proj(x, g_kv, g_q, wkv_bf, wqg_bf, bqg, pos_cols, onehot, *, b, t, tm=512):
    m, d = x.shape
    nt = t // tm
    n_kv = wkv_bf.shape[1]
    n_qg = wqg_bf.shape[1]
    n_q = N_HEADS * HEAD_DIM
    const = lambda *shape: pl.BlockSpec(shape, lambda i: (0,) * len(shape))
    rows_of = lambda n: pl.BlockSpec((tm, n), lambda i: (i, 0))
    heads_of = lambda n: pl.BlockSpec((1, N_KV_HEADS, tm, n), lambda i: (i // nt, 0, i % nt, 0))
    heads_shape = lambda n: jax.ShapeDtypeStruct((b, N_KV_HEADS, t, n), BF16)
    return pl.pallas_call(
        _proj_kernel,
        grid=(m // tm,),
        in_specs=[rows_of(d), const(1, d), const(1, d), const(d, n_kv), const(d, n_qg), const(1, n_qg),
                  pl.BlockSpec((tm, 128), lambda i: (i % nt, 0)), pl.BlockSpec((tm, 128), lambda i: (i % nt, 0))],
        out_specs=[rows_of(ROW_WIDTH), rows_of(n_kv - ROW_WIDTH), rows_of(n_q), rows_of(n_qg - n_q),
                   heads_of(256), heads_of(128), heads_of(128), heads_of(128)],
        out_shape=[jax.ShapeDtypeStruct((m, ROW_WIDTH), F32), jax.ShapeDtypeStruct((m, n_kv - ROW_WIDTH), F32),
                   jax.ShapeDtypeStruct((m, n_q), F32), jax.ShapeDtypeStruct((m, n_qg - n_q), F32),
                   heads_shape(256), heads_shape(128), heads_shape(128), heads_shape(128)],
        compiler_params=_params(("parallel",), 56),
        name="proj",
    )(x, g_kv, g_q, wkv_bf, wqg_bf, bqg, pos_cols, onehot)


def _lin_kernel(x_ref, g_ref, w_ref, b_ref, *o_refs, splits):
    h = _dot(_rms(x_ref[...], g_ref[...]).astype(BF16), w_ref[...]) + b_ref[...]
    c0 = 0
    for o_ref, n in zip(o_refs, splits):
        o_ref[...] = h[:, c0:c0 + n]
        c0 += n


def _lin(x, g, w_bf, b, *, splits, tm):
    m, k = x.shape
    n = w_bf.shape[1]
    return pl.pallas_call(
        functools.partial(_lin_kernel, splits=splits),
        grid=(m // tm,),
        in_specs=[pl.BlockSpec((tm, k), lambda i: (i, 0)), pl.BlockSpec((1, k), lambda i: (0, 0)),
                  pl.BlockSpec((k, n), lambda i: (0, 0)), pl.BlockSpec((1, n), lambda i: (0, 0))],
        out_specs=[pl.BlockSpec((tm, s), lambda i: (i, 0)) for s in splits],
        out_shape=[jax.ShapeDtypeStruct((m, s), F32) for s in splits],
        compiler_params=_params(("parallel",), 48),
        name="lin",
    )(x, g, w_bf, b)


def _compress_kernel(pt_ref, *refs, n_page_refs, paged, rows_per_ref, n_sub):
    del pt_ref
    page_refs = refs[:n_page_refs]
    w1d_ref, w1_ref, pe_ref, w2_ref, kc_ref, vc_ref = refs[n_page_refs:]
    n_take = rows_per_ref // CMP_STRIDE

    def tap_rows(c, j):
        take = pl.ds(j, n_take, stride=CMP_STRIDE)
        if paged:
            return jnp.concatenate([r[0, c, take, :] for r in page_refs], axis=0)
        return page_refs[c][0, take, :]

    for kv, o_ref in enumerate((kc_ref, vc_ref)):
        pew = _dot(pe_ref[kv].astype(BF16), w1_ref[kv])[0:1]
        for c2 in range(2):
            acc = None
            for jp in range(CMP_STRIDE // 2):
                lhs = jnp.concatenate([tap_rows(2 * kv + c2, 2 * jp), tap_rows(2 * kv + c2, 2 * jp + 1)], axis=1)
                d = _dot(lhs.astype(BF16), w1d_ref[kv, jp])
                acc = d if acc is None else acc + d
            for e in range(2):
                a = acc[:, 2 * e * CMP_HIDDEN:(2 * e + 1) * CMP_HIDDEN]
                b2 = acc[:, (2 * e + 1) * CMP_HIDDEN:(2 * e + 2) * CMP_HIDDEN]
                hid = a + pltpu.roll(b2, a.shape[0] - 1, 0) + pew
                act = hid * _sigmoid(hid)
                out = _dot(act.astype(BF16), w2_ref[kv])
                for k in range(out.shape[0] // n_sub):
                    o_ref[k, 2 * c2 + e] = out[k * n_sub:(k + 1) * n_sub]


def _compress(pages, page_table, w1d_bf, w1_bf, pe8, w2_bf, *, paged):
    if paged:
        nb, n_pages = page_table.shape
        rows_per_ref = pages.shape[2]
        n_sub = n_pages * rows_per_ref // CMP_STRIDE
        seqs = 4
        page_specs = [
            pl.BlockSpec((1, 4, rows_per_ref, 128),
                         functools.partial(lambda k, p, b, pt: (pt[seqs * b + k, p], 0, 0, 0), k, p))
            for k in range(seqs) for p in range(n_pages)]
    else:
        nb, rows_per_ref, _ = pages.shape
        n_sub = rows_per_ref // CMP_STRIDE
        seqs = 1
        page_specs = [pl.BlockSpec((1, rows_per_ref, 128), functools.partial(lambda c, b, pt: (b, 0, c), c))
                      for c in range(4)]
    page_args = [pages] * len(page_specs)
    kern = functools.partial(_compress_kernel, n_page_refs=len(page_args), paged=paged, rows_per_ref=rows_per_ref,
                             n_sub=n_sub)
    const = lambda *shape: pl.BlockSpec(shape, lambda b, pt: (0,) * len(shape))
    out_spec = pl.BlockSpec((seqs, N_KV_HEADS, n_sub, HEAD_DIM), lambda b, pt: (b, 0, 0, 0))
    return pl.pallas_call(
        kern,
        grid_spec=pltpu.PrefetchScalarGridSpec(
            num_scalar_prefetch=1,
            grid=(nb // seqs,),
            in_specs=page_specs + [
                const(2, CMP_STRIDE // 2, 4 * HEAD_DIM, 4 * CMP_HIDDEN),
                const(2, CMP_BLOCK * HEAD_DIM, CMP_HIDDEN),
                const(2, 8, CMP_BLOCK * HEAD_DIM),
                const(2, CMP_HIDDEN, HEAD_DIM),
            ],
            out_specs=[out_spec, out_spec],
        ),
        out_shape=[jax.ShapeDtypeStruct((nb, N_KV_HEADS, n_sub, HEAD_DIM), F32)] * 2,
        compiler_params=_params(("parallel",), 48),
        name="compress",
    )(page_table, *page_args, w1d_bf, w1_bf, pe8, w2_bf)


def _attn_prompt_kernel(q_ref, gt_ref, st_ref, kc_ref, vc_ref, ks_ref, vs_ref, kw_ref, vw_ref, cm_ref, dm_ref,
                        dc_ref, wb_ref, o_ref, qa_scr, rank_scr, m_scr, ala_scr, alb_scr, sa_scr, sb_scr, pa_scr,
                        pb_scr, acc_scr, *, tq, tk, rc, n_sel):
    c0 = pl.program_id(2) * tq
    rows = Q_PER_KV * tq
    band = WINDOW + tq

    lane = lax.broadcasted_iota(jnp.int32, (tq, 128), 1)
    qb = q_ref[0] * (HEAD_DIM ** -0.5)
    for r in range(Q_PER_KV):
        pair = qb[:, (r // 2) * 128:(r // 2 + 1) * 128]
        if r % 2:
            pair = pltpu.roll(pair, HEAD_DIM, 1)
        qa_scr[r * tq:(r + 1) * tq, 0:128] = jnp.where(lane < HEAD_DIM, pair, st_ref[0, r]).astype(BF16)
    ql = qa_scr[:, 0:128]

    n_cmp_pad = kc_ref.shape[2]
    s = _dot_nt(ql, kc_ref[0, 0])
    mask = jnp.concatenate([dc_ref[...]] * Q_PER_KV, axis=0) <= c0
    s = jnp.where(mask, s, NEG)
    m = jnp.max(s, axis=-1, keepdims=True)
    e = jnp.where(mask, jnp.exp(s - m), 0.0)
    p = e / jnp.maximum(jnp.sum(e, axis=-1, keepdims=True), 1e-30)
    o_c = _dot(p.astype(BF16), vc_ref[0, 0])

    w0 = pl.multiple_of(jnp.maximum(c0 - WINDOW, 0), 128)
    sw = _dot_nt(ql, kw_ref[0, 0, pl.ds(w0, band), :]) + jnp.concatenate([wb_ref[0]] * Q_PER_KV, axis=0)
    pw = jnp.exp(sw - jnp.max(sw, axis=-1, keepdims=True))
    aw = _dot(pw.astype(BF16), vw_ref[0, 0, pl.ds(w0, band), :])
    o_w = aw[:, :HEAD_DIM] / aw[:, HEAD_DIM:HEAD_DIM + 1]

    gate = _sigmoid(gt_ref[0])

    def heads_to_lanes(per_head):
        return jnp.concatenate([per_head(r, slice(r * tq, (r + 1) * tq)) for r in range(Q_PER_KV)], axis=1)

    o_ref[0] = heads_to_lanes(lambda r, rs: gate[:, 3 * r:3 * r + 1] * o_c[rs] + gate[:, 3 * r + 2:3 * r + 3] * o_w[rs])

    psum = p[0:tq] + p[tq:2 * tq] + p[2 * tq:3 * tq] + p[3 * tq:4 * tq]
    hi, lo = _split_bf16(psum)
    imp = _dot_nt(cm_ref[...], hi) + _dot_nt(cm_ref[...], lo)
    blk = lax.broadcasted_iota(jnp.int32, (n_sel, tq), 0)
    tcol = c0 + lax.broadcasted_iota(jnp.int32, (n_sel, tq), 1)
    cur = tcol >> (SEL_BLOCK.bit_length() - 1)
    forced = (blk == 0) | (blk == cur) | (blk == cur - 1)
    valid = blk * SEL_BLOCK <= tcol
    pri = jnp.where(valid, jnp.where(forced, FORCED_PRIORITY, imp), -1.0)
    rank_scr[...] = jnp.zeros_like(rank_scr)
    sub8 = lax.broadcasted_iota(jnp.int32, (8, tq), 0)
    for g8 in range(n_sel // 8):
        lo8, hi8 = 8 * g8, 8 * g8 + 8

        def walk_group(lo8=lo8, hi8=hi8):
            mid = pri[lo8:hi8]
            below = above = None
            inside = jnp.zeros((8, tq), jnp.int32)
            for i in range(8):
                row = mid[i:i + 1, :]
                if lo8:
                    b_i = (row > pri[:lo8]).astype(jnp.int32)
                    below = b_i if below is None else below + b_i
                inside = inside + ((row > mid) | ((row == mid) & (sub8 > i))).astype(jnp.int32)
                if hi8 < n_sel:
                    a_i = (row >= pri[hi8:]).astype(jnp.int32)
                    above = a_i if above is None else above + a_i
            parts = [part for part in (below, inside, above) if part is not None]
            rank_scr[...] += jnp.concatenate(parts, axis=0)

        walk_group()

    bias = jnp.where(rank_scr[...] < min(N_SELECT, n_sel), 0.0, NEG).T
    right = jnp.concatenate([bias, jnp.zeros((tq, 128 - n_sel), F32)], axis=1).astype(BF16)
    for r in range(Q_PER_KV):
        qa_scr[r * tq:(r + 1) * tq, 128:256] = right

    m_scr[...] = jnp.full_like(m_scr, NEG)
    acc_scr[...] = jnp.zeros_like(acc_scr)

    def scores(j):
        return _dot_nt(qa_scr[...], ks_ref[0, 0, pl.ds(pl.multiple_of(j * tk, tk), tk), :])

    def soft_pv(s_scr, p_scr, al_scr, j, masked):
        k0 = pl.multiple_of(j * tk, tk)
        for i in range(rows // rc):
            rs = slice(i * rc, (i + 1) * rc)
            qs = (i * rc) % tq
            sc = s_scr[rs, :]
            if masked:
                sc = jnp.where(dm_ref[qs:qs + rc, :] <= c0 - k0, sc, NEG)
            m_old = m_scr[rs]
            m_new = jnp.maximum(m_old, jnp.max(sc, axis=-1, keepdims=True))
            al_scr[rs] = jnp.exp(m_old - m_new)
            p_scr[rs] = jnp.exp(sc - m_new).astype(BF16)
            m_scr[rs] = m_new
        acc_scr[...] = al_scr[...] * acc_scr[...] + _dot(p_scr[...], vs_ref[0, 0, pl.ds(k0, tk), :])

    n_tiles = c0 // tk + 1
    n_loop = (n_tiles - 1) // 2
    sa_scr[...] = scores(0)

    def tile_pair(jj, carry):
        sb_scr[...] = scores(2 * jj + 1)
        soft_pv(sa_scr, pa_scr, ala_scr, 2 * jj, False)
        sa_scr[...] = scores(2 * jj + 2)
        soft_pv(sb_scr, pb_scr, alb_scr, 2 * jj + 1, False)
        return carry

    lax.fori_loop(0, n_loop, tile_pair, 0)

    @pl.when(n_tiles % 2 == 0)
    def _():
        sb_scr[...] = scores(2 * n_loop + 1)
        soft_pv(sa_scr, pa_scr, ala_scr, 2 * n_loop, False)
        soft_pv(sb_scr, pb_scr, alb_scr, 2 * n_loop + 1, True)

    @pl.when(n_tiles % 2 == 1)
    def _():
        soft_pv(sa_scr, pa_scr, ala_scr, 2 * n_loop, True)

    acc = acc_scr[...]
    o_s = acc[:, :HEAD_DIM] / acc[:, HEAD_DIM:HEAD_DIM + 1]
    gate_s = _sigmoid(gt_ref[0])
    o_ref[0] += heads_to_lanes(lambda r, rs: gate_s[:, 3 * r + 1:3 * r + 2] * o_s[rs])


def _attn_prompt(q, gate_pre, slope_tab, kc_aug, vc, ks_aug, vs, kw_aug, vw, cmat_t, *, tq, tk=512, rc=128):
    b, t, _ = q.shape
    n_sel = t // SEL_BLOCK
    n_cmp_pad = kc_aug.shape[2]
    kern = functools.partial(_attn_prompt_kernel, tq=tq, tk=tk, rc=rc, n_sel=n_sel)
    rows = Q_PER_KV * tq
    band = WINDOW + tq
    qi = np.arange(tq)[:, None]
    dmat = jnp.asarray(np.arange(tk)[None, :] - qi, jnp.int32)
    dcmp = jnp.asarray(CMP_STRIDE * np.arange(n_cmp_pad)[None, :] + (CMP_BLOCK - 1) - qi, jnp.int32)
    dist = (np.minimum(np.arange(WINDOW // tq + 1) * tq, WINDOW)[:, None, None] + qi[None]
            - np.arange(band)[None, None, :])
    wbias = jnp.asarray(np.where((dist >= 0) & (dist < WINDOW), 0.0, NEG), F32)
    per_bg = lambda *shape: pl.BlockSpec((1, 1) + shape, lambda bi, gi, ci: (bi, gi, 0, 0))
    return pl.pallas_call(
        kern,
        grid=(b, N_KV_HEADS, t // tq),
        in_specs=[
            pl.BlockSpec((1, tq, KV_WIDTH), lambda bi, gi, ci: (bi, ci, gi)),
            pl.BlockSpec((1, tq, 128), lambda bi, gi, ci: (bi, ci, gi)),
            pl.BlockSpec((1, Q_PER_KV, tq, 128), lambda bi, gi, ci: (gi, 0, 0, 0)),
            per_bg(n_cmp_pad, 128),
            per_bg(n_cmp_pad, HEAD_DIM),
            per_bg(t, 256),
            per_bg(t, 128),
            per_bg(t, 128),
            per_bg(t, 128),
            pl.BlockSpec((n_sel, n_cmp_pad), lambda bi, gi, ci: (0, 0)),
            pl.BlockSpec((tq, tk), lambda bi, gi, ci: (0, 0)),
            pl.BlockSpec((tq, n_cmp_pad), lambda bi, gi, ci: (0, 0)),
            pl.BlockSpec((1, tq, band), lambda bi, gi, ci: (jnp.minimum(ci, WINDOW // tq), 0, 0)),
        ],
        out_specs=pl.BlockSpec((1, tq, KV_WIDTH), lambda bi, gi, ci: (bi, ci, gi)),
        out_shape=jax.ShapeDtypeStruct((b, t, N_HEADS * HEAD_DIM), F32),
        scratch_shapes=[
            pltpu.VMEM((rows, 256), BF16),
            pltpu.VMEM((n_sel, tq), jnp.int32),
            pltpu.VMEM((rows, 1), F32),
            pltpu.VMEM((rows, 1), F32),
            pltpu.VMEM((rows, 1), F32),
            pltpu.VMEM((rows, tk), F32),
            pltpu.VMEM((rows, tk), F32),
            pltpu.VMEM((rows, tk), BF16),
            pltpu.VMEM((rows, tk), BF16),
            pltpu.VMEM((rows, 128), F32),
        ],
        compiler_params=_params(("parallel", "parallel", "arbitrary"), 48),
        name="attn_prompt",
    )(q, gate_pre, slope_tab, kc_aug, vc, ks_aug, vs, kw_aug, vw, cmat_t, dmat, dcmp, wbias)


def _softmax_lanes(s):
    e = jnp.exp(s - jnp.max(s, axis=-1, keepdims=True))
    return e / jnp.sum(e, axis=-1, keepdims=True)


def _attn_sample_kernel(pt_ref, *refs, n_pages, past_len, n_sel, seqs):
    del pt_ref
    page_refs = refs[5:5 + seqs * n_pages]
    for k in range(seqs):
        _attn_sample_one(k, *refs[:5], page_refs[k * n_pages:(k + 1) * n_pages], *refs[5 + seqs * n_pages:],
                         past_len=past_len, n_sel=n_sel)


def _attn_sample_one(k, q_ref, gt_ref, sl_ref, kc_ref, vc_ref, page_refs, new_ref, win_ref, ex_ref, cm_ref, rm_ref,
                     o_ref, wn_ref, *, past_len, n_sel):
    q = q_ref[k]
    sl = sl_ref[:, 0:1]
    new = new_ref[k]

    n_cmp_pad = kc_ref.shape[1]
    dist_c = (past_len - (CMP_BLOCK - 1)
              - CMP_STRIDE * lax.broadcasted_iota(jnp.int32, (N_HEADS, n_cmp_pad), 1)).astype(F32)
    s = _dot_nt(q, kc_ref[k]) - sl * dist_c
    mask = dist_c >= 0
    s = jnp.where(mask, s, NEG)
    e = jnp.where(mask, jnp.exp(s - jnp.max(s, axis=-1, keepdims=True)), 0.0)
    p_c = e / jnp.maximum(jnp.sum(e, axis=-1, keepdims=True), 1e-30)
    o_c = _dot(p_c.astype(BF16), vc_ref[k])

    hi, lo = _split_bf16(p_c)
    imp = _dot(hi, cm_ref[...]) + _dot(lo, cm_ref[...])
    hi, lo = _split_bf16(imp)
    imp = _dot(rm_ref[...], hi) + _dot(rm_ref[...], lo)
    n_blk = imp.shape[1]
    blk = lax.broadcasted_iota(jnp.int32, (N_HEADS, n_blk), 1)
    cur = past_len // SEL_BLOCK
    forced = (blk == 0) | (blk == cur) | (blk == cur - 1)
    valid = blk * SEL_BLOCK <= past_len
    pri = jnp.where(valid, jnp.where(forced, FORCED_PRIORITY, imp), -1.0)
    pri = jnp.where(blk < n_sel, pri, -2.0)
    rank = jnp.zeros((N_HEADS, n_blk), jnp.int32)
    for s2 in range(n_sel):
        col = pri[:, s2:s2 + 1]
        beats = (col > pri) | ((col == pri) & (blk > s2))
        rank = rank + beats.astype(jnp.int32)
    bias = jnp.where(rank < min(N_SELECT, n_sel), 0.0, NEG)

    k_sel = jnp.concatenate([jnp.concatenate([r[0, 0], r[0, 1]], axis=1) for r in page_refs],
                            axis=0).astype(BF16)
    v_sel = jnp.concatenate([jnp.concatenate([r[0, 2], r[0, 3]], axis=1) for r in page_refs],
                            axis=0).astype(BF16)
    dist_s = (past_len - lax.broadcasted_iota(jnp.int32, (N_HEADS, past_len), 1)).astype(F32)
    s = _dot_nt(q, k_sel) - sl * dist_s + _dot(bias.astype(BF16), ex_ref[...])
    qf = q.astype(F32)
    k_new = new[:, 2 * KV_WIDTH:3 * KV_WIDTH].astype(BF16).astype(F32)
    v_new = new[:, 3 * KV_WIDTH:4 * KV_WIDTH].astype(BF16).astype(F32)
    s_new = jnp.sum(qf * k_new, axis=-1, keepdims=True) + bias[:, past_len // SEL_BLOCK:past_len // SEL_BLOCK + 1]
    m = jnp.maximum(jnp.max(s, axis=-1, keepdims=True), s_new)
    e = jnp.exp(s - m)
    e_new = jnp.exp(s_new - m)
    l = jnp.sum(e, axis=-1, keepdims=True) + e_new
    o_s = (_dot(e.astype(BF16), v_sel) + e_new.astype(BF16).astype(F32) * v_new) / l

    wb = win_ref.shape[2]
    rowi = lax.broadcasted_iota(jnp.int32, (wb, 128), 0)
    wn = []
    for c in range(win_ref.shape[1]):
        new_c = new[:, 4 * KV_WIDTH + c * 128:4 * KV_WIDTH + (c + 1) * 128]
        wn.append(jnp.where(rowi == wb - 1, new_c, pltpu.roll(win_ref[k, c], wb - 1, 0)))
        wn_ref[k, c] = wn[c]
    dist_w = (wb - 1 - lax.broadcasted_iota(jnp.int32, (N_HEADS, wb), 1)).astype(F32)
    k_win = jnp.concatenate(wn[0:2], axis=1).astype(BF16)
    v_win = jnp.concatenate(wn[2:4], axis=1).astype(BF16)
    p_w = _softmax_lanes(_dot_nt(q, k_win) - sl * dist_w)
    o_w = _dot(p_w.astype(BF16), v_win)

    gate = _sigmoid(gt_ref[k])
    o_ref[k] = gate[:, 0:1] * o_c + gate[:, 1:2] * o_s + gate[:, 2:3] * o_w


def _attn_sample(qbd, gate16, slope16, kc_all, vc_all, pages, page_table, new_rows, state_win, expand, cmat, rmat):
    nb, n_pages = page_table.shape
    page_rows = pages.shape[2]
    past_len = n_pages * page_rows
    n_sel = -(-(past_len + 1) // SEL_BLOCK)
    n_wblk, wb = state_win.shape[1:3]
    n_cmp_pad = kc_all.shape[1]
    seqs = 2
    kern = functools.partial(_attn_sample_kernel, n_pages=n_pages, past_len=past_len, n_sel=n_sel, seqs=seqs)
    per_b = lambda *shape: pl.BlockSpec((seqs,) + shape, lambda b, pt: (b,) + (0,) * len(shape))
    const = lambda *shape: pl.BlockSpec(shape, lambda b, pt: (0,) * len(shape))
    page_specs = [
        pl.BlockSpec((1, 4, page_rows, 128),
                     functools.partial(lambda k, p, b, pt: (pt[seqs * b + k, p], 1, 0, 0), k, p))
        for k in range(seqs) for p in range(n_pages)]
    return pl.pallas_call(
        kern,
        grid_spec=pltpu.PrefetchScalarGridSpec(
            num_scalar_prefetch=1,
            grid=(nb // seqs,),
            in_specs=[per_b(N_HEADS, KV_WIDTH), per_b(N_HEADS, 128), const(N_HEADS, 128),
                      per_b(n_cmp_pad, KV_WIDTH), per_b(n_cmp_pad, KV_WIDTH)]
            + page_specs
            + [per_b(1, 6 * KV_WIDTH), per_b(n_wblk, wb, 128), const(*expand.shape), const(*cmat.shape),
               const(*rmat.shape)],
            out_specs=[per_b(N_HEADS, KV_WIDTH), per_b(n_wblk, wb, 128)],
        ),
        out_shape=[jax.ShapeDtypeStruct((nb, N_HEADS, KV_WIDTH), F32),
                   jax.ShapeDtypeStruct((nb, n_wblk, wb, 128), F32)],
        compiler_params=_params(("parallel",), 48),
        name="attn_sample",
    )(page_table, qbd, gate16, slope16, kc_all, vc_all, *([pages] * (seqs * n_pages)), new_rows, state_win, expand,
      cmat, rmat)


def _alibi_slopes():
    return jnp.exp2(-8.0 * (jnp.arange(N_HEADS, dtype=F32) + 1.0) / N_HEADS)


def _cmp_to_sel(n_cmp_pad, n_cmp, n_sel_pad):
    m = np.zeros((n_cmp_pad, n_sel_pad), np.float32)
    for n in range(n_cmp):
        for k in range(CMP_BLOCK // CMP_STRIDE):
            m[n, (n + k) * CMP_STRIDE // SEL_BLOCK] += 1.0
    return m


def _pos_cols(pos, width):
    cols = np.zeros((pos.shape[0], width), np.float32)
    cols[:, 0] = cols[:, 1] = pos % 64
    cols[:, 2] = cols[:, 3] = pos // 64
    return cols


def kernel(x_prompt, x_sample, state_pool, cache_kv_pages, state_win, page_table, norm_mix, norm_mlp, w_up, w_down,
           pool_w, pool_scale, norm_kv, w_kv, cmp_pe, cmp_w1, cmp_w2, w_qg, b_gate, w_o, norm_final):
    b, t, d = x_prompt.shape
    nb = x_sample.shape[0]
    n_phys, page_rows = cache_kv_pages.shape[:2]
    n_q = N_HEADS * HEAD_DIM
    row = lambda v: v.reshape(1, -1)

    wu = w_up.astype(BF16)
    wd = w_down.astype(BF16)
    pw = pool_w[0].astype(BF16)
    wkv = w_kv.astype(BF16)
    wo = w_o[0].astype(BF16)
    hh = np.arange(N_HEADS)
    gcols = ((hh // Q_PER_KV) * 128 + (hh % Q_PER_KV) * N_BRANCH)[:, None] + np.arange(N_BRANCH)[None, :]
    gcols = gcols.reshape(-1)
    wg = jnp.zeros((d, N_KV_HEADS * 128), F32).at[:, gcols].set(w_qg[0][:, n_q:])
    wqg = jnp.concatenate([w_qg[0][:, :n_q], wg], axis=1).astype(BF16)
    bqg = jnp.zeros((1, n_q + N_KV_HEADS * 128), F32).at[0, n_q + gcols].set(b_gate[0])
    w1_bf = cmp_w1.astype(BF16)
    w1r = w1_bf.reshape(2, 2, CMP_STRIDE, HEAD_DIM, CMP_HIDDEN)
    w1c = jnp.concatenate([w1r[:, 0], w1r[:, 1]], axis=-1)
    w1d = jnp.concatenate([jnp.concatenate([w1c, jnp.zeros_like(w1c)], axis=-1),
                           jnp.concatenate([jnp.zeros_like(w1c), w1c], axis=-1)], axis=2)
    w1d = w1d.reshape(2, CMP_STRIDE // 2, 4 * HEAD_DIM, 4 * CMP_HIDDEN)
    pe8 = jnp.broadcast_to(cmp_pe.reshape(2, 1, CMP_BLOCK * HEAD_DIM), (2, 8, CMP_BLOCK * HEAD_DIM))
    w2_bf = cmp_w2.astype(BF16)

    slopes = _alibi_slopes()
    s_hi = slopes.astype(BF16).astype(F32)
    s_lo = (slopes - s_hi).astype(BF16).astype(F32)
    slope_cols = jnp.stack([s_hi, s_lo, 64.0 * s_hi, 64.0 * s_lo], axis=-1)

    x1, u_last = _pool_prompt(x_prompt, row(norm_mix[0]), pw, row(pool_scale[0]))
    pool_p = u_last[:, None, 16 - POOL_BUF:, :]
    x2 = _mlp(x1.reshape(b * t, d), row(norm_mlp[0]), wu[0], wd[0], row(norm_final), final_norm=False, tm=1024)
    tq = 256
    n_sub_p = t // CMP_STRIDE
    n_sel_p = t // SEL_BLOCK
    tpos = np.arange(t)
    pc_t = jnp.asarray(np.concatenate([np.zeros((t, HEAD_DIM), np.float32), _pos_cols(tpos, HEAD_DIM)], axis=1))
    onehot_t = jnp.asarray(np.eye(n_sel_p, 128, dtype=np.float32)[tpos // SEL_BLOCK], BF16)
    kv_rows, kv_win, q_p, gate_p, ks_aug, vs, kw_aug, vw = _proj(
        x2, row(norm_kv), row(norm_mix[1]), wkv, wqg, bqg, pc_t, onehot_t, b=b, t=t)
    kv_rows_p = kv_rows.reshape(b, t, 2, 2, N_KV_HEADS, HEAD_DIM)
    win_new_p = kv_win.reshape(b, t, 2, N_KV_HEADS, HEAD_DIM)[:, -min(WINDOW, t):]

    dummy_pt = jnp.zeros((b, 1), jnp.int32)
    kc_p, vc_p = _compress(kv_rows.reshape(b, t, ROW_WIDTH), dummy_pt, w1d, w1_bf, pe8, w2_bf, paged=False)
    cache_kv_pages, kc_p = lax.optimization_barrier((cache_kv_pages, kc_p))

    bcast = lambda a: jnp.broadcast_to(a[None, None], (b, N_KV_HEADS) + a.shape)
    pc_c = jnp.asarray(_pos_cols(CMP_STRIDE * np.arange(n_sub_p) + CMP_BLOCK - 1, HEAD_DIM), BF16)
    kc_aug = jnp.concatenate([kc_p.astype(BF16), bcast(pc_c)], axis=-1)
    cmat_t = jnp.asarray(_cmp_to_sel(n_sub_p, n_sub_p - 1, n_sel_p).T, BF16)
    slope_tab = jnp.zeros((N_KV_HEADS, Q_PER_KV, tq, 128), F32).at[:, :, :, HEAD_DIM:HEAD_DIM + 4].set(
        jnp.broadcast_to(slope_cols.reshape(N_KV_HEADS, Q_PER_KV, 1, 4), (N_KV_HEADS, Q_PER_KV, tq, 4)))
    o_p = _attn_prompt(q_p.reshape(b, t, n_q), gate_p.reshape(b, t, N_KV_HEADS * 128), slope_tab, kc_aug,
                       vc_p.astype(BF16), ks_aug, vs, kw_aug, vw, cmat_t, tq=tq)
    x_sample, state_pool, page_table, o_p = lax.optimization_barrier((x_sample, state_pool, page_table, o_p))
    y_prompt = _mlp(x2, row(norm_mlp[1]), wu[1], wd[1], row(norm_final), final_norm=True, tm=1024, tf=1024,
                    attn=o_p.reshape(b * t, n_q), wo_bf=wo, vmem_mib=60).reshape(b, t, d)

    xs0 = x_sample.reshape(nb, d)
    xs1, u_s = _pool_sample(xs0, state_pool[:, 0], row(norm_mix[0]), pw, row(pool_scale[0]))
    pool_s = jnp.concatenate([state_pool[:, 0, 1:], u_s[:, None]], axis=1)[:, None]
    xs2 = _mlp(xs1, row(norm_mlp[0]), wu[0], wd[0], row(norm_final), final_norm=False, tm=nb)
    kv_s, = _lin(xs2, row(norm_kv), wkv, jnp.zeros((1, wkv.shape[1]), F32), splits=(wkv.shape[1],), tm=nb)
    kv_rows_s = kv_s[:, :ROW_WIDTH].reshape(nb, 1, 2, 2, N_KV_HEADS, HEAD_DIM)

    pages = cache_kv_pages.reshape(n_phys, page_rows, ROW_WIDTH // 128, 128).transpose(0, 2, 1, 3)
    kc_s, vc_s = _compress(pages, page_table, w1d, w1_bf, pe8, w2_bf, paged=True)
    n_sub_s = kc_s.shape[2]
    all_heads = lambda a: a.transpose(0, 2, 1, 3).reshape(nb, n_sub_s, KV_WIDTH).astype(BF16)

    q_s, gate_s = _lin(xs2, row(norm_mix[1]), wqg, bqg, splits=(n_q, N_KV_HEADS * 128), tm=nb)
    q4 = q_s.reshape(nb, N_KV_HEADS, Q_PER_KV, HEAD_DIM) * (HEAD_DIM ** -0.5)
    qbd = jnp.einsum('bgrd,gh->bgrhd', q4, jnp.eye(N_KV_HEADS, dtype=F32)).reshape(nb, N_HEADS, KV_WIDTH)
    g16 = gate_s.reshape(nb, N_KV_HEADS, 128)[:, :, :Q_PER_KV * N_BRANCH].reshape(nb, N_HEADS, N_BRANCH)
    g16 = jnp.pad(g16, ((0, 0), (0, 0), (0, 128 - N_BRANCH)))
    slope16 = jnp.broadcast_to(slopes[:, None], (N_HEADS, 128))
    past_len = page_table.shape[1] * page_rows
    n_blk_pad = 64
    expand = jnp.asarray(np.eye(n_blk_pad, dtype=np.float32)[:, np.arange(past_len) // SEL_BLOCK], BF16)
    cmat_s = jnp.asarray(_cmp_to_sel(n_sub_s, n_sub_s - 1, n_blk_pad), BF16)
    rmat = jnp.asarray(np.kron(np.eye(N_KV_HEADS), np.ones((Q_PER_KV, Q_PER_KV))), BF16)
    o_s, win_new_s = _attn_sample(
        qbd.astype(BF16), g16, slope16, all_heads(kc_s), all_heads(vc_s), pages, page_table,
        kv_s.reshape(nb, 1, 6 * KV_WIDTH),
        state_win.reshape(nb, state_win.shape[1], 2 * KV_WIDTH // 128, 128).transpose(0, 2, 1, 3),
        expand, cmat_s, rmat)
    o5 = o_s.reshape(nb, N_KV_HEADS, Q_PER_KV, N_KV_HEADS, HEAD_DIM)
    o_heads = jnp.stack([o5[:, gh, :, gh] for gh in range(N_KV_HEADS)], axis=1).reshape(nb, n_q)
    y_sample = _mlp(xs2, row(norm_mlp[1]), wu[1], wd[1], row(norm_final), final_norm=True, tm=nb,
                    attn=o_heads, wo_bf=wo).reshape(nb, 1, d)

    return (y_prompt, y_sample, pool_p, pool_s, kv_rows_p, kv_rows_s, win_new_p,
            win_new_s.transpose(0, 2, 1, 3).reshape(state_win.shape))
```
